```python
import math, functools
import jax, jax.numpy as jnp
from jax import lax
import numpy as np

D_MODEL = 1024
BATCH = 2
SEQ = 16384
DEPTH = 2
DEC_BATCH = 16
DEC_SEQ = 32
PAST_LEN = 1024

CHUNK = 64
HEAD_DIM = 64
A_HEADS = 8
A_PREV_CHUNKS = 8
A_BAND = A_PREV_CHUNKS * CHUNK
A_REL_CLIP = 128
B_Q_HEADS = 8
B_KV_HEADS = 2
B_GROUP = B_Q_HEADS // B_KV_HEADS
B_WINDOW = 128
B_PREV_CHUNKS = B_WINDOW // CHUNK
T5_BUCKETS = 32
T5_MAX_DIST = 128
C_HEADS = 16
C_Q_LORA = 384
C_KV_LORA = 256
C_NOPE = 64
C_ROPE = 32
C_V = 64
ROPE_BASE = 10000.0
MLA_Q_BLOCK = 128
D_FF = 2816
N_EVEN = (DEPTH + 1) // 2
N_ODD = DEPTH // 2
EPS = 1e-6
NEG = -1e30
A_W = A_HEADS * HEAD_DIM
BQ_W = B_Q_HEADS * HEAD_DIM
BKV_W = B_KV_HEADS * HEAD_DIM
AB_SPLITS = (A_W, 2 * A_W, 3 * A_W, 3 * A_W + BQ_W, 3 * A_W + BQ_W + BKV_W)
AB_WIDTH = 3 * A_W + BQ_W + 2 * BKV_W
AB_OUT = A_W + BQ_W
C_IN_WIDTH = C_Q_LORA + C_KV_LORA + C_ROPE

kernel_name = "hybrid_chunk_streaming_encoder_step"


def rmsnorm(x, g):
    xf = x.astype(jnp.float32)
    y = xf * lax.rsqrt(jnp.mean(xf * xf, axis=-1, keepdims=True) + EPS)
    return (y * g.astype(jnp.float32)).astype(x.dtype)


def modulate(h, shift, scale):
    return h * (1 + scale[:, None, :]) + shift[:, None, :]


def swiglu(h, wg, wu, wd):
    return (jax.nn.silu(h @ wg) * (h @ wu)) @ wd


def attend(q, k, v, bias=None, mask=None, sink=None):
    scale = q.shape[-1] ** -0.5
    s = jnp.einsum('...qhgd,...khd->...hgqk', q, k).astype(jnp.float32) * scale
    if bias is not None:
        s = s + bias
    if mask is not None:
        s = jnp.where(mask, s, NEG)
    m = jnp.max(s, axis=-1, keepdims=True)
    if sink is not None:
        sk = sink.astype(jnp.float32)[:, :, None, None]
        m = jnp.maximum(m, sk)
    p = jnp.exp(s - m)
    den = jnp.sum(p, axis=-1, keepdims=True)
    if sink is not None:
        den = den + jnp.exp(sk - m)
    p = (p / den).astype(v.dtype)
    return jnp.einsum('...hgqk,...khd->...qhgd', p, v)


def chunk_band(t, n_prev):
    b, s, h, d = t.shape
    nc = s // CHUNK
    tp = jnp.pad(t.reshape(b, nc, CHUNK, h, d), ((0, 0), (n_prev, 0), (0, 0), (0, 0), (0, 0)))
    band = jnp.stack([tp[:, i:i + nc] for i in range(n_prev + 1)], axis=2)
    return band.reshape(b, nc, (n_prev + 1) * CHUNK, h, d)


def band_mask(nc, n_prev):
    kb = (n_prev + 1) * CHUNK
    valid = (jnp.arange(nc)[:, None] - n_prev + jnp.arange(kb)[None, :] // CHUNK) >= 0
    return valid.reshape(nc, 1, 1, 1, kb)


def band_rel(n_prev):
    return jnp.arange((n_prev + 1) * CHUNK)[None, :] - n_prev * CHUNK - jnp.arange(CHUNK)[:, None]


def stream_rel(n_cache, t):
    return jnp.arange(n_cache + t)[None, :] - n_cache - jnp.arange(t)[:, None]


def clipped_bias(table, rel):
    idx = jnp.clip(rel, -A_REL_CLIP, A_REL_CLIP) + A_REL_CLIP
    return jnp.transpose(table[idx].astype(jnp.float32), (2, 0, 1))[:, None]


def t5_bucket(rel):
    nb = T5_BUCKETS // 2
    ret = jnp.where(rel > 0, nb, 0)
    n = jnp.abs(rel)
    max_exact = nb // 2
    large = max_exact + (jnp.log(jnp.maximum(n, 1).astype(jnp.float32) / max_exact)
                         / math.log(T5_MAX_DIST / max_exact) * (nb - max_exact)).astype(jnp.int32)
    large = jnp.minimum(large, nb - 1)
    return ret + jnp.where(n < max_exact, n, large)


def t5_bias_fn(table, rel):
    tq, tk = rel.shape
    b = table[t5_bucket(rel)].astype(jnp.float32)
    return jnp.transpose(b, (2, 0, 1)).reshape(B_KV_HEADS, B_GROUP, tq, tk)


def rope(x, pos):
    half = x.shape[-1] // 2
    inv = ROPE_BASE ** (-jnp.arange(half, dtype=jnp.float32) / half)
    ang = pos.astype(jnp.float32)[:, None] * inv[None, :]
    cos = jnp.cos(ang)[:, None, :]
    sin = jnp.sin(ang)[:, None, :]
    xf = x.astype(jnp.float32)
    x1, x2 = xf[..., :half], xf[..., half:]
    return jnp.concatenate([x1 * cos - x2 * sin, x1 * sin + x2 * cos], axis=-1).astype(x.dtype)


def heads_ab(h, w_in):
    b, t, _ = h.shape
    qa, ka, va, qb, kb, vb = jnp.split(h @ w_in, AB_SPLITS, axis=-1)
    r = lambda z, n: z.reshape(b, t, n, HEAD_DIM)
    return (r(qa, A_HEADS), r(ka, A_HEADS), r(va, A_HEADS),
            r(qb, B_Q_HEADS), r(kb, B_KV_HEADS), r(vb, B_KV_HEADS))


def mixer_ab_prompt(h, w_in, w_out, rel_tab, t5_tab, sinks):
    b, s, _ = h.shape
    nc = s // CHUNK
    qa, ka, va, qb, kb, vb = heads_ab(h, w_in)
    oa = attend(qa.reshape(b, nc, CHUNK, A_HEADS, 1, HEAD_DIM),
                chunk_band(ka, A_PREV_CHUNKS), chunk_band(va, A_PREV_CHUNKS),
                bias=clipped_bias(rel_tab, band_rel(A_PREV_CHUNKS)),
                mask=band_mask(nc, A_PREV_CHUNKS))
    ob = attend(qb.reshape(b, nc, CHUNK, B_KV_HEADS, B_GROUP, HEAD_DIM),
                chunk_band(kb, B_PREV_CHUNKS), chunk_band(vb, B_PREV_CHUNKS),
                bias=t5_bias_fn(t5_tab, band_rel(B_PREV_CHUNKS)),
                mask=band_mask(nc, B_PREV_CHUNKS),
                sink=sinks.reshape(B_KV_HEADS, B_GROUP))
    y = jnp.concatenate([oa.reshape(b, s, A_W), ob.reshape(b, s, BQ_W)], axis=-1) @ w_out
    la, lb = min(A_BAND, s), min(B_WINDOW, s)
    return y, (ka[:, s - la:], va[:, s - la:], kb[:, s - lb:], vb[:, s - lb:])


def mixer_ab_sample(h, cak, cav, cbk, cbv, w_in, w_out, rel_tab, t5_tab, sinks):
    b, t, _ = h.shape
    qa, ka, va, qb, kb, vb = heads_ab(h, w_in)
    la = cak.shape[1]
    oa = attend(qa[:, :, :, None], jnp.concatenate([cak, ka], axis=1), jnp.concatenate([cav, va], axis=1),
                bias=clipped_bias(rel_tab, stream_rel(la, t)))
    lb = cbk.shape[1]
    ob = attend(qb.reshape(b, t, B_KV_HEADS, B_GROUP, HEAD_DIM),
                jnp.concatenate([cbk, kb], axis=1), jnp.concatenate([cbv, vb], axis=1),
                bias=t5_bias_fn(t5_tab, stream_rel(lb, t)),
                sink=sinks.reshape(B_KV_HEADS, B_GROUP))
    y = jnp.concatenate([oa.reshape(b, t, A_W), ob.reshape(b, t, BQ_W)], axis=-1) @ w_out
    return y, (ka, va, kb, vb)


def mla_project(h, pos, w_in, qn_g, kvn_g, w_qb):
    b, t, _ = h.shape
    q_lat, kv_lat, k_r = jnp.split(h @ w_in, (C_Q_LORA, C_Q_LORA + C_KV_LORA), axis=-1)
    q = (rmsnorm(q_lat, qn_g) @ w_qb).reshape(b, t, C_HEADS, C_NOPE + C_ROPE)
    q = jnp.concatenate([q[..., :C_NOPE], rope(q[..., C_NOPE:], pos)], axis=-1)
    kv_lat = rmsnorm(kv_lat, kvn_g)
    k_r = rope(k_r[:, :, None, :], pos)[:, :, 0]
    return q, kv_lat, k_r


def mla_keys(kv_lat, k_r, w_kvb):
    b, l, _ = kv_lat.shape
    kv = (kv_lat @ w_kvb).reshape(b, l, C_HEADS, C_NOPE + C_V)
    k = jnp.concatenate([kv[..., :C_NOPE], jnp.broadcast_to(k_r[:, :, None, :], (b, l, C_HEADS, C_ROPE))], axis=-1)
    return k, kv[..., C_NOPE:]


def mixer_c_prompt(h, w_in, qn_g, kvn_g, w_qb, w_kvb, w_out):
    b, s, _ = h.shape
    q, kv_lat, k_r = mla_project(h, jnp.arange(s), w_in, qn_g, kvn_g, w_qb)
    k, v = mla_keys(kv_lat, k_r, w_kvb)
    nb = s // MLA_Q_BLOCK
    qb = q.reshape(b, nb, MLA_Q_BLOCK, C_HEADS, 1, C_NOPE + C_ROPE).transpose(1, 0, 2, 3, 4, 5)
    kchunk = jnp.arange(s) // CHUNK

    def block(args):
        qi, i = args
        qchunk = (i * MLA_Q_BLOCK + jnp.arange(MLA_Q_BLOCK)) // CHUNK
        return attend(qi, k, v, mask=kchunk[None, :] <= qchunk[:, None])

    o = lax.map(block, (qb, jnp.arange(nb)))
    o = o.transpose(1, 0, 2, 3, 4, 5).reshape(b, s, C_HEADS * C_V)
    return o @ w_out, (kv_lat, k_r)


def mixer_c_sample(h, ckv, ckr, w_in, qn_g, kvn_g, w_qb, w_kvb, w_out):
    b, t, _ = h.shape
    past = ckv.shape[1]
    q, kv_lat, k_r = mla_project(h, past + jnp.arange(t), w_in, qn_g, kvn_g, w_qb)
    k, v = mla_keys(jnp.concatenate([ckv, kv_lat], axis=1), jnp.concatenate([ckr, k_r], axis=1), w_kvb)
    o = attend(q[:, :, :, None], k, v).reshape(b, t, C_HEADS * C_V)
    return o @ w_out, (kv_lat, k_r)


def macaron_layer(x, c, mix, w_ada_l, b_ada_l, norm_g_l, wg, wu, wd):
    mods = jnp.split(jax.nn.silu(c) @ w_ada_l + b_ada_l, 9, axis=-1)
    h = modulate(rmsnorm(x, norm_g_l[0]), mods[0], mods[1])
    x = x + 0.5 * mods[2][:, None, :] * swiglu(h, wg[0], wu[0], wd[0])
    h = modulate(rmsnorm(x, norm_g_l[1]), mods[3], mods[4])
    y, st = mix(h)
    x = x + mods[5][:, None, :] * y
    h = modulate(rmsnorm(x, norm_g_l[2]), mods[6], mods[7])
    x = x + 0.5 * mods[8][:, None, :] * swiglu(h, wg[1], wu[1], wd[1])
    return x, st


def setup_inputs(seed: int = 0) -> dict:
    key = jax.random.key(seed)
    ks = iter(jax.random.split(key, 40))
    nrm = lambda shape, s=1.0: jax.random.normal(next(ks), shape, jnp.float32) * s
    la, lb = min(A_BAND, PAST_LEN), min(B_WINDOW, PAST_LEN)
    return {
        "x_prompt": nrm((BATCH, SEQ, D_MODEL)),
        "x_sample": nrm((DEC_BATCH, DEC_SEQ, D_MODEL)),
        "c_prompt": nrm((BATCH, D_MODEL)),
        "c_sample": nrm((DEC_BATCH, D_MODEL)),
        "cache_a_k": nrm((N_EVEN, DEC_BATCH, la, A_HEADS, HEAD_DIM)),
        "cache_a_v": nrm((N_EVEN, DEC_BATCH, la, A_HEADS, HEAD_DIM)),
        "cache_b_k": nrm((N_EVEN, DEC_BATCH, lb, B_KV_HEADS, HEAD_DIM)),
        "cache_b_v": nrm((N_EVEN, DEC_BATCH, lb, B_KV_HEADS, HEAD_DIM)),
        "cache_c_kv": nrm((N_ODD, DEC_BATCH, PAST_LEN, C_KV_LORA)),
        "cache_c_kr": nrm((N_ODD, DEC_BATCH, PAST_LEN, C_ROPE)),
        "w_ada": nrm((DEPTH, D_MODEL, 9 * D_MODEL), 0.3 * D_MODEL ** -0.5),
        "b_ada": nrm((DEPTH, 9 * D_MODEL), 0.02),
        "norm_g": 1.0 + nrm((DEPTH, 3, D_MODEL), 0.02),
        "final_norm_g": 1.0 + nrm((D_MODEL,), 0.02),
        "ffn_w_gate": nrm((DEPTH, 2, D_MODEL, D_FF), D_MODEL ** -0.5),
        "ffn_w_up": nrm((DEPTH, 2, D_MODEL, D_FF), D_MODEL ** -0.5),
        "ffn_w_down": nrm((DEPTH, 2, D_FF, D_MODEL), D_FF ** -0.5),
        "w_in_ab": nrm((N_EVEN, D_MODEL, AB_WIDTH), D_MODEL ** -0.5),
        "w_out_ab": nrm((N_EVEN, AB_OUT, D_MODEL), AB_OUT ** -0.5),
        "rel_bias_a": nrm((N_EVEN, 2 * A_REL_CLIP + 1, A_HEADS), 0.1),
        "t5_bias": nrm((T5_BUCKETS, B_Q_HEADS), 0.1),
        "sinks_b": nrm((N_EVEN, B_Q_HEADS), 0.5),
        "w_in_c": nrm((N_ODD, D_MODEL, C_IN_WIDTH), D_MODEL ** -0.5),
        "c_q_norm_g": 1.0 + nrm((N_ODD, C_Q_LORA), 0.02),
        "c_kv_norm_g": 1.0 + nrm((N_ODD, C_KV_LORA), 0.02),
        "w_qb": nrm((N_ODD, C_Q_LORA, C_HEADS * (C_NOPE + C_ROPE)), C_Q_LORA ** -0.5),
        "w_kvb": nrm((N_ODD, C_KV_LORA, C_HEADS * (C_NOPE + C_V)), C_KV_LORA ** -0.5),
        "w_out_c": nrm((N_ODD, C_HEADS * C_V, D_MODEL), (C_HEADS * C_V) ** -0.5),
    }


def reference(x_prompt, x_sample, c_prompt, c_sample, cache_a_k, cache_a_v, cache_b_k, cache_b_v,
              cache_c_kv, cache_c_kr, w_ada, b_ada, norm_g, final_norm_g, ffn_w_gate, ffn_w_up, ffn_w_down,
              w_in_ab, w_out_ab, rel_bias_a, t5_bias, sinks_b, w_in_c, c_q_norm_g, c_kv_norm_g, w_qb, w_kvb,
              w_out_c):
    xp, xs = x_prompt, x_sample
    ab_p, ab_s, c_p, c_s = [], [], [], []
    for l in range(DEPTH):
        lw = (w_ada[l], b_ada[l], norm_g[l], ffn_w_gate[l], ffn_w_up[l], ffn_w_down[l])
        if l % 2 == 0:
            e = l // 2
            shared = dict(w_in=w_in_ab[e], w_out=w_out_ab[e], rel_tab=rel_bias_a[e], t5_tab=t5_bias, sinks=sinks_b[e])
            mix_p = functools.partial(mixer_ab_prompt, **shared)
            mix_s = functools.partial(mixer_ab_sample, cak=cache_a_k[e], cav=cache_a_v[e],
                                      cbk=cache_b_k[e], cbv=cache_b_v[e], **shared)
            xp, st = macaron_layer(xp, c_prompt, mix_p, *lw)
            ab_p.append(st)
            xs, st = macaron_layer(xs, c_sample, mix_s, *lw)
            ab_s.append(st)
        else:
            o = l // 2
            shared = dict(w_in=w_in_c[o], qn_g=c_q_norm_g[o], kvn_g=c_kv_norm_g[o], w_qb=w_qb[o],
                          w_kvb=w_kvb[o], w_out=w_out_c[o])
            mix_p = functools.partial(mixer_c_prompt, **shared)
            mix_s = functools.partial(mixer_c_sample, ckv=cache_c_kv[o], ckr=cache_c_kr[o], **shared)
            xp, st = macaron_layer(xp, c_prompt, mix_p, *lw)
            c_p.append(st)
            xs, st = macaron_layer(xs, c_sample, mix_s, *lw)
            c_s.append(st)
    y_prompt = rmsnorm(xp, final_norm_g)
    y_sample = rmsnorm(xs, final_norm_g)
    stk = lambda lst, i: jnp.stack([st[i] for st in lst], axis=0)
    return (y_prompt, y_sample,
            stk(ab_p, 0), stk(ab_p, 1), stk(ab_p, 2), stk(ab_p, 3), stk(c_p, 0), stk(c_p, 1),
            stk(ab_s, 0), stk(ab_s, 1), stk(ab_s, 2), stk(ab_s, 3), stk(c_s, 0), stk(c_s, 1))
```

```python
import functools
import math

import numpy as np
import jax
import jax.numpy as jnp
from jax import lax
from jax.experimental import pallas as pl
from jax.experimental.pallas import tpu as pltpu

F32 = jnp.float32
BF = jnp.bfloat16

CHUNK = 64
HEAD_DIM = 64
A_HEADS = 8
A_PREV_CHUNKS = 8
A_REL_CLIP = 128
B_Q_HEADS = 8
B_KV_HEADS = 2
B_PREV_CHUNKS = 2
T5_BUCKETS = 32
T5_MAX_DIST = 128
C_HEADS = 16
C_Q_LORA = 384
C_KV_LORA = 256
C_NOPE = 64
C_ROPE = 32
C_V = 64
ROPE_BASE = 10000.0
EPS = 1e-6
NEG = -1e30
A_W = A_HEADS * HEAD_DIM
BQ_W = B_Q_HEADS * HEAD_DIM
BKV_W = B_KV_HEADS * HEAD_DIM

LANES = 128
VMEM_LIMIT = 56 * 1024 * 1024

ROW_TILE = 512
FF_CHUNK = 256
BAND_TQ = 256
MLA_T = 1024
C_IN_EXT = 768
C_HEAD_W = 128


def _cparams(n_axes, vmem=VMEM_LIMIT):
    return pltpu.CompilerParams(dimension_semantics=("arbitrary",) * n_axes,
                                vmem_limit_bytes=vmem)


def _const_spec(shape):
    n = len(shape)
    return pl.BlockSpec(shape, lambda *_: (0,) * n)


def _rms(x):
    return x * lax.rsqrt(jnp.mean(x * x, axis=-1, keepdims=True) + EPS)


def _norm_mod(x, g, shift, scale):
    return (_rms(x) * g) * (1.0 + scale) + shift


def _mod_spec(G, blk_off, k, D):
    return pl.BlockSpec((G, 1, D), lambda i, j: (blk_off + i, 0, k))


def _adaln_body(c_ref, w_ref, b_ref, o_ref):
    c = c_ref[...]
    s = (c * jax.nn.sigmoid(c)).astype(BF)
    o_ref[0] = jnp.dot(s, w_ref[0].astype(BF), preferred_element_type=F32) + b_ref[0]


def _adaln(c_all, w_ada, b_ada):
    L, D, N = w_ada.shape
    R = c_all.shape[0]
    tn = 1024
    return pl.pallas_call(
        _adaln_body,
        grid=(L, N // tn),
        in_specs=[pl.BlockSpec((R, D), lambda l, j: (0, 0)),
                  pl.BlockSpec((1, D, tn), lambda l, j: (l, 0, j)),
                  pl.BlockSpec((1, 1, tn), lambda l, j: (l, 0, j))],
        out_specs=pl.BlockSpec((1, R, tn), lambda l, j: (l, 0, j)),
        out_shape=jax.ShapeDtypeStruct((L, R, N), F32),
        compiler_params=_cparams(2),
        name="adaln",
    )(c_all, w_ada, b_ada.reshape(L, 1, N))


def _ffn_body(x_ref, sh_ref, sc_ref, gt_ref, g_ref, wg_ref, wu_ref, wd_ref, *rest, final):
    if final:
        fg_ref, o_ref, a_ref = rest
    else:
        o_ref, a_ref = rest
    x = x_ref[...]
    G, R, D = x.shape
    F = wg_ref.shape[1]
    hb = _norm_mod(x, g_ref[...], sh_ref[...], sc_ref[...]).reshape(G * R, D).astype(BF)
    for c in range(F // FF_CHUNK):
        lo, hi = c * FF_CHUNK, (c + 1) * FF_CHUNK
        g = jnp.dot(hb, wg_ref[:, lo:hi], preferred_element_type=F32)
        u = jnp.dot(hb, wu_ref[:, lo:hi], preferred_element_type=F32)
        a_ref[:, lo:hi] = (g * jax.nn.sigmoid(g) * u).astype(BF)
    ff = jnp.dot(a_ref[...], wd_ref[...], preferred_element_type=F32)
    y = x + (0.5 * gt_ref[...]) * ff.reshape(G, R, D)
    if final:
        y = _rms(y) * fg_ref[...]
    o_ref[...] = y


def _ffn(x, mods, blk_off, kbase, g, wg, wu, wd, G, R, final_g=None):
    nseq, T, D = x.shape
    F = wg.shape[1]
    final = final_g is not None
    in_specs = [pl.BlockSpec((G, R, D), lambda i, j: (i, j, 0)),
                _mod_spec(G, blk_off, kbase, D), _mod_spec(G, blk_off, kbase + 1, D),
                _mod_spec(G, blk_off, kbase + 2, D),
                _const_spec((1, D)), _const_spec((D, F)), _const_spec((D, F)), _const_spec((F, D))]
    args = [x, mods, mods, mods, g.reshape(1, D), wg, wu, wd]
    if final:
        in_specs.append(_const_spec((1, D)))
        args.append(final_g.reshape(1, D))
    return pl.pallas_call(
        functools.partial(_ffn_body, final=final),
        grid=(nseq // G, T // R),
        in_specs=in_specs,
        out_specs=pl.BlockSpec((G, R, D), lambda i, j: (i, j, 0)),
        out_shape=jax.ShapeDtypeStruct((nseq, T, D), F32),
        scratch_shapes=[pltpu.VMEM((G * R, F), BF)],
        compiler_params=_cparams(2),
        name="ffn",
    )(*args)


def _dup_halves(k):
    rolled = pltpu.roll(k, HEAD_DIM, 1)
    lo = lax.broadcasted_iota(jnp.int32, k.shape, 1) < HEAD_DIM
    return jnp.concatenate([jnp.where(lo, k, rolled), jnp.where(lo, rolled, k)], axis=1)


def _proj_ab_body(x_ref, sh_ref, sc_ref, g_ref, w_ref,
                  qa_ref, ka_ref, va_ref, qb_ref, kb_ref, vb_ref,
                  ka32_ref, va32_ref, kb32_ref, vb32_ref, *, ta, tb):
    x = x_ref[...]
    G, R, D = x.shape
    hb = _norm_mod(x, g_ref[...], sh_ref[...], sc_ref[...]).reshape(G * R, D).astype(BF)
    res = jnp.dot(hb, w_ref[...], preferred_element_type=F32)
    o_ka, o_va, o_qb, o_kb, o_vb = A_W, 2 * A_W, 3 * A_W, 3 * A_W + BQ_W, 3 * A_W + BQ_W + BKV_W
    qa_ref[...] = res[:, :o_ka].astype(BF).reshape(G, R, A_W)
    ka_ref[...] = res[:, o_ka:o_va].astype(BF).reshape(G, R, A_W)
    va_ref[...] = res[:, o_va:o_qb].astype(BF).reshape(G, R, A_W)
    qb_ref[...] = res[:, o_qb:o_kb].astype(BF).reshape(G, R, BQ_W)
    kb = res[:, o_kb:o_vb]
    vb = res[:, o_vb:o_vb + BKV_W]
    kb_ref[...] = _dup_halves(kb).astype(BF).reshape(G, R, 2 * BKV_W)
    vb_ref[...] = _dup_halves(vb).astype(BF).reshape(G, R, 2 * BKV_W)

    @pl.when(pl.program_id(1) == pl.num_programs(1) - 1)
    def _():
        ka32_ref[...] = res[:, o_ka:o_va].reshape(G, R, A_W)[:, R - ta:, :]
        va32_ref[...] = res[:, o_va:o_qb].reshape(G, R, A_W)[:, R - ta:, :]
        kb32_ref[...] = kb.reshape(G, R, BKV_W)[:, R - tb:, :]
        vb32_ref[...] = vb.reshape(G, R, BKV_W)[:, R - tb:, :]


def _proj_ab(x, mods, blk_off, g, w, G, R, ta, tb):
    nseq, T, D = x.shape
    row = lambda W: pl.BlockSpec((G, R, W), lambda i, j: (i, j, 0))
    tail = lambda t, W: pl.BlockSpec((G, t, W), lambda i, j: (i, 0, 0))
    bshape = lambda W: jax.ShapeDtypeStruct((nseq, T, W), BF)
    return pl.pallas_call(
        functools.partial(_proj_ab_body, ta=ta, tb=tb),
        grid=(nseq // G, T // R),
        in_specs=[row(D), _mod_spec(G, blk_off, 3, D), _mod_spec(G, blk_off, 4, D),
                  _const_spec((1, D)), _const_spec(w.shape)],
        out_specs=[row(A_W), row(A_W), row(A_W), row(BQ_W), row(2 * BKV_W), row(2 * BKV_W),
                   tail(ta, A_W), tail(ta, A_W), tail(tb, BKV_W), tail(tb, BKV_W)],
        out_shape=[bshape(A_W), bshape(A_W), bshape(A_W), bshape(BQ_W), bshape(2 * BKV_W), bshape(2 * BKV_W),
                   jax.ShapeDtypeStruct((nseq, ta, A_W), F32), jax.ShapeDtypeStruct((nseq, ta, A_W), F32),
                   jax.ShapeDtypeStruct((nseq, tb, BKV_W), F32), jax.ShapeDtypeStruct((nseq, tb, BKV_W), F32)],
        compiler_params=_cparams(2),
        name="proj_ab",
    )(x, mods, mods, g.reshape(1, D), w)


def _t5_bucket_np(rel):
    nb = T5_BUCKETS // 2
    ret = np.where(rel > 0, nb, 0)
    n = np.abs(rel)
    max_exact = nb // 2
    ratio = np.maximum(n, 1).astype(np.float32) / np.float32(max_exact)
    large = max_exact + (np.log(ratio).astype(np.float32) / np.float32(math.log(T5_MAX_DIST / max_exact))
                         * np.float32(nb - max_exact)).astype(np.int32)
    large = np.minimum(large, nb - 1)
    return ret + np.where(n < max_exact, n, large)


def _bias_body(thi_ref, tlo_ref, idx_ref, o_ref, *, Tq, span, prev_chunks, band_chunks):
    E = thi_ref.shape[1]
    L = idx_ref.shape[1]
    onehot = (lax.broadcasted_iota(jnp.int32, (E, L), 0) == idx_ref[...]).astype(BF)
    t = (jnp.dot(thi_ref[...], onehot, preferred_element_type=F32)
         + jnp.dot(tlo_ref[...], onehot, preferred_element_type=F32))
    if band_chunks is not None:
        qc = lax.broadcasted_iota(jnp.int32, (Tq, span), 0) // CHUNK
        kc = lax.broadcasted_iota(jnp.int32, (Tq, span), 1) // CHUNK - prev_chunks
        valid = (kc <= qc) & (kc >= qc - band_chunks)
    for h in range(o_ref.shape[0]):
        x = jnp.broadcast_to(t[h:h + 1, :], (Tq, L))
        y = pltpu.roll(x, L - Tq + 1, 1, stride=1, stride_axis=0)[:, :span]
        if band_chunks is not None:
            y = jnp.where(valid, y, NEG)
        o_ref[h] = y


def _bias_tile(table, kind, Tq, span, n_prev_rows, band_chunks):
    H = table.shape[1]
    L = -(-(Tq + span - 1) // LANES) * LANES
    rel = np.arange(L) - (Tq - 1) - n_prev_rows
    if kind == "clip":
        idx = np.clip(rel, -A_REL_CLIP, A_REL_CLIP) + A_REL_CLIP
    else:
        idx = _t5_bucket_np(rel)
    E = -(-table.shape[0] // LANES) * LANES
    tt = jnp.zeros((H, E), F32).at[:, :table.shape[0]].set(table.T.astype(F32))
    thi = tt.astype(BF)
    tlo = (tt - thi.astype(F32)).astype(BF)
    return pl.pallas_call(
        functools.partial(_bias_body, Tq=Tq, span=span, prev_chunks=n_prev_rows // CHUNK,
                          band_chunks=band_chunks),
        out_shape=jax.ShapeDtypeStruct((H, Tq, span), F32),
        compiler_params=pltpu.CompilerParams(vmem_limit_bytes=VMEM_LIMIT),
        name="bias_tile",
    )(thi, tlo, jnp.asarray(idx.reshape(1, L), jnp.int32))


def _band_body(*refs, n_parts, n_prev_rows, pair_map, has_sink, Tq):
    q_ref = refs[0]
    k_refs = refs[1:1 + n_parts]
    v_refs = refs[1 + n_parts:1 + 2 * n_parts]
    bias_ref = refs[1 + 2 * n_parts]
    sink_ref = refs[2 + 2 * n_parts] if has_sink else None
    o_ref = refs[-1]
    i = pl.program_id(1)
    k = jnp.concatenate([r[0].astype(BF) for r in k_refs], axis=0)
    v = jnp.concatenate([r[0].astype(BF) for r in v_refs], axis=0)
    q = q_ref[0]
    span = k.shape[0]
    lane_lo = lax.broadcasted_iota(jnp.int32, (1, LANES), 1) < HEAD_DIM
    if n_prev_rows:
        col_ok = lax.broadcasted_iota(jnp.int32, (1, span), 1) >= (n_prev_rows - i * Tq)
    for p in range(q.shape[1] // LANES):
        qp = q[:, p * LANES:(p + 1) * LANES]
        kp = k[:, pair_map[p] * LANES:(pair_map[p] + 1) * LANES]
        vp = v[:, pair_map[p] * LANES:(pair_map[p] + 1) * LANES]
        halves = []
        for hh in range(2):
            h = 2 * p + hh
            qm = jnp.where(lane_lo if hh == 0 else jnp.logical_not(lane_lo), qp, jnp.zeros_like(qp))
            s = lax.dot_general(qm, kp, (((1,), (1,)), ((), ())), preferred_element_type=F32)
            s = s + bias_ref[h]
            if n_prev_rows:
                s = jnp.where(col_ok, s, NEG)
            m = jnp.max(s, axis=-1, keepdims=True)
            if has_sink:
                m = jnp.maximum(m, sink_ref[h])
            e = jnp.exp(s - m)
            den = jnp.sum(e, axis=-1, keepdims=True)
            if has_sink:
                den = den + jnp.exp(sink_ref[h] - m)
            o = jnp.dot(e.astype(BF), vp, preferred_element_type=F32)
            halves.append(o * (1.0 / den))
        o_ref[0, :, p * LANES:(p + 1) * LANES] = jnp.where(lane_lo, halves[0], halves[1]).astype(BF)


def _band_attn(q, kparts, vparts, bias, sinks, *, Tq, n_prev_rows, pair_map):
    nseq, T, W = q.shape

    def part_spec(arr, rows, off):
        if off is None:
            return pl.BlockSpec((1, rows, arr.shape[2]), lambda b, i: (b, 0, 0))
        return pl.BlockSpec((1, rows, arr.shape[2]), lambda b, i: (b, jnp.maximum(i + off, 0), 0))

    in_specs = [pl.BlockSpec((1, Tq, W), lambda b, i: (b, i, 0))]
    in_specs += [part_spec(*p) for p in kparts] + [part_spec(*p) for p in vparts]
    in_specs.append(_const_spec(bias.shape))
    args = [q] + [p[0] for p in kparts] + [p[0] for p in vparts] + [bias]
    if sinks is not None:
        in_specs.append(pl.BlockSpec(memory_space=pltpu.SMEM))
        args.append(sinks.astype(F32))
    return pl.pallas_call(
        functools.partial(_band_body, n_parts=len(kparts), n_prev_rows=n_prev_rows,
                          pair_map=pair_map, has_sink=sinks is not None, Tq=Tq),
        grid=(nseq, T // Tq),
        in_specs=in_specs,
        out_specs=pl.BlockSpec((1, Tq, W), lambda b, i: (b, i, 0)),
        out_shape=jax.ShapeDtypeStruct((nseq, T, W), BF),
        compiler_params=_cparams(2),
        name="band_attn",
    )(*args)


def _out_proj_body(*refs, n_in):
    o_refs = refs[:n_in]
    w_refs = refs[n_in:2 * n_in]
    x_ref, gt_ref, out_ref = refs[2 * n_in:]
    x = x_ref[...]
    G, R, D = x.shape
    acc = None
    for o_ref, w_ref in zip(o_refs, w_refs):
        y = jnp.dot(o_ref[...].reshape(G * R, o_ref.shape[2]), w_ref[...], preferred_element_type=F32)
        acc = y if acc is None else acc + y
    out_ref[...] = x + gt_ref[...] * acc.reshape(G, R, D)


def _out_proj(os_, ws, x, mods, blk_off, G, R):
    nseq, T, D = x.shape
    row = lambda W: pl.BlockSpec((G, R, W), lambda i, j: (i, j, 0))
    return pl.pallas_call(
        functools.partial(_out_proj_body, n_in=len(os_)),
        grid=(nseq // G, T // R),
        in_specs=[row(o.shape[2]) for o in os_] + [_const_spec(w.shape) for w in ws]
                 + [row(D), _mod_spec(G, blk_off, 5, D)],
        out_specs=row(D),
        out_shape=jax.ShapeDtypeStruct((nseq, T, D), F32),
        compiler_params=_cparams(2),
        name="out_proj",
    )(*os_, *ws, x, mods)


def _proj_c_body(x_ref, sh_ref, sc_ref, g_ref, win_ref, qn_ref, kvn_ref, wqb_ref, cos_ref, sin_ref,
                 q_ref, kv32_ref, kr32_ref):
    x = x_ref[...]
    G, R, D = x.shape
    hb = _norm_mod(x, g_ref[...], sh_ref[...], sc_ref[...]).reshape(G * R, D).astype(BF)
    res = jnp.dot(hb, win_ref[...], preferred_element_type=F32)
    q_lat = res[:, :C_Q_LORA]
    kv_lat = res[:, C_Q_LORA:C_Q_LORA + C_KV_LORA]
    krg = res[:, C_Q_LORA + C_KV_LORA:]
    cos = cos_ref[...]
    sin = sin_ref[...]
    qn = (_rms(q_lat) * qn_ref[...]).astype(BF)
    qr = jnp.dot(qn, wqb_ref[...], preferred_element_type=F32)
    nw = qr.shape[1]
    qrot = pltpu.roll(qr, nw - C_ROPE, 1)
    scale = (C_NOPE + C_ROPE) ** -0.5
    for h in range(nw // C_HEAD_W):
        a = qr[:, h * C_HEAD_W:(h + 1) * C_HEAD_W].reshape(G, R, C_HEAD_W)
        b = qrot[:, h * C_HEAD_W:(h + 1) * C_HEAD_W].reshape(G, R, C_HEAD_W)
        q_ref[:, h] = ((a * cos + b * sin) * scale).astype(BF)
    kv32_ref[...] = (_rms(kv_lat) * kvn_ref[...]).reshape(G, R, C_KV_LORA)
    krot = pltpu.roll(krg, C_HEAD_W - C_ROPE, 1)
    krf = krg.reshape(G, R, C_HEAD_W) * cos + krot.reshape(G, R, C_HEAD_W) * sin
    kr32_ref[...] = krf[:, :, C_NOPE:C_NOPE + C_ROPE]


def _proj_c(x, mods, blk_off, g, win, qn_g, kvn_g, wqb, cos_t, sin_t, G, R):
    nseq, T, D = x.shape
    nh = wqb.shape[1] // C_HEAD_W
    tab = pl.BlockSpec((1, R, C_HEAD_W), lambda i, j: (0, j, 0))
    return pl.pallas_call(
        _proj_c_body,
        grid=(nseq // G, T // R),
        in_specs=[pl.BlockSpec((G, R, D), lambda i, j: (i, j, 0)),
                  _mod_spec(G, blk_off, 3, D), _mod_spec(G, blk_off, 4, D), _const_spec((1, D)),
                  _const_spec(win.shape), _const_spec((1, C_Q_LORA)), _const_spec((1, C_KV_LORA)),
                  _const_spec(wqb.shape), tab, tab],
        out_specs=[pl.BlockSpec((G, nh, R, C_HEAD_W), lambda i, j: (i, 0, j, 0)),
                   pl.BlockSpec((G, R, C_KV_LORA), lambda i, j: (i, j, 0)),
                   pl.BlockSpec((G, R, C_ROPE), lambda i, j: (i, j, 0))],
        out_shape=[jax.ShapeDtypeStruct((nseq, nh, T, C_HEAD_W), BF),
                   jax.ShapeDtypeStruct((nseq, T, C_KV_LORA), F32),
                   jax.ShapeDtypeStruct((nseq, T, C_ROPE), F32)],
        compiler_params=_cparams(2),
        name="proj_c",
    )(x, mods, mods, g.reshape(1, D), win, qn_g.reshape(1, -1), kvn_g.reshape(1, -1), wqb, cos_t, sin_t)


def _kvexp_body(kv_ref, kr_ref, wk_ref, wvt_ref, place_ref, k_ref, vt_ref):
    kv = kv_ref[0].astype(BF)
    krf = jnp.dot(kr_ref[0].astype(BF), place_ref[...], preferred_element_type=F32)
    kx = jnp.dot(kv, wk_ref[...], preferred_element_type=F32)
    for h in range(k_ref.shape[1]):
        k_ref[0, h] = (kx[:, h * C_HEAD_W:(h + 1) * C_HEAD_W] + krf).astype(BF)
    vt_ref[0] = lax.dot_general(wvt_ref[...], kv, (((1,), (1,)), ((), ())),
                                preferred_element_type=F32).astype(BF)


def _kv_expand(kv, kr, wk, wvt, R):
    nseq, T, _ = kv.shape
    nh = wk.shape[1] // C_HEAD_W
    place = np.zeros((C_ROPE, C_HEAD_W), np.float32)
    place[np.arange(C_ROPE), C_NOPE + np.arange(C_ROPE)] = 1.0
    return pl.pallas_call(
        _kvexp_body,
        grid=(nseq, T // R),
        in_specs=[pl.BlockSpec((1, R, C_KV_LORA), lambda b, j: (b, j, 0)),
                  pl.BlockSpec((1, R, C_ROPE), lambda b, j: (b, j, 0)),
                  _const_spec(wk.shape), _const_spec(wvt.shape), _const_spec(place.shape)],
        out_specs=[pl.BlockSpec((1, nh, R, C_HEAD_W), lambda b, j: (b, 0, j, 0)),
                   pl.BlockSpec((1, nh * C_V, R), lambda b, j: (b, 0, j))],
        out_shape=[jax.ShapeDtypeStruct((nseq, nh, T, C_HEAD_W), BF),
                   jax.ShapeDtypeStruct((nseq, nh * C_V, T), BF)],
        compiler_params=_cparams(2),
        name="kv_expand",
    )(kv, kr, wk, wvt, jnp.asarray(place, BF))


def _flash_body(qi_ref, kj_ref, q_ref, k_ref, vt_ref, o_ref, acc_ref, m_ref, l_ref,
                *, Tq, Tk, n_kv, causal, valid_len):
    t = pl.program_id(1)
    i = qi_ref[t]
    j = kj_ref[t]
    nh = q_ref.shape[1]

    @pl.when(j == 0)
    def _():
        m_ref[...] = jnp.full(m_ref.shape, NEG, F32)
        l_ref[...] = jnp.zeros(l_ref.shape, F32)
        acc_ref[...] = jnp.zeros(acc_ref.shape, F32)

    def sweep(mask_kind):
        def head(h, carry):
            s = lax.dot_general(k_ref[0, h], q_ref[0, h], (((1,), (1,)), ((), ())),
                                preferred_element_type=F32)
            if mask_kind == "chunk":
                kc = (lax.broadcasted_iota(jnp.int32, (Tk, 1), 0) + j * Tk) // CHUNK
                qc = (lax.broadcasted_iota(jnp.int32, (1, Tq), 1) + i * Tq) // CHUNK
                s = jnp.where(kc <= qc, s, NEG)
            elif mask_kind == "len":
                s = jnp.where(lax.broadcasted_iota(jnp.int32, (Tk, 1), 0) + j * Tk < valid_len, s, NEG)
            m_prev = m_ref[h]
            m_new = jnp.maximum(m_prev, jnp.max(s, axis=0, keepdims=True))
            p = jnp.exp(s - m_new)
            alpha = jnp.exp(m_prev - m_new)
            l_ref[h] = alpha * l_ref[h] + jnp.sum(p, axis=0, keepdims=True)
            acc_ref[h] = alpha * acc_ref[h] + jnp.dot(vt_ref[0, h], p.astype(BF), preferred_element_type=F32)
            m_ref[h] = m_new
            return carry
        lax.fori_loop(0, nh, head, 0)

    if causal:
        needs_mask = (j + 1) * Tk > i * Tq + CHUNK
        pl.when(needs_mask)(lambda: sweep("chunk"))
        pl.when(jnp.logical_not(needs_mask))(lambda: sweep(None))
        last = ((i + 1) * Tq - 1) // Tk
    else:
        sweep("len" if valid_len is not None else None)
        last = n_kv - 1

    @pl.when(j == last)
    def _():
        for p in range(nh // 2):
            o2 = jnp.concatenate([acc_ref[2 * p] * (1.0 / l_ref[2 * p]),
                                  acc_ref[2 * p + 1] * (1.0 / l_ref[2 * p + 1])], axis=0)
            o_ref[0, :, p * LANES:(p + 1) * LANES] = o2.T.astype(BF)


def _mla_attn(q, k, vt, *, Tq, Tk, causal, valid_len=None):
    nseq, nh, T, _ = q.shape
    S = k.shape[2]
    nq, nk = T // Tq, S // Tk
    if causal:
        pairs = [(i, j) for i in range(nq) for j in range(nk) if j * Tk < (i + 1) * Tq]
    else:
        pairs = [(i, j) for i in range(nq) for j in range(nk)]
    qi = jnp.asarray([p[0] for p in pairs], jnp.int32)
    kj = jnp.asarray([p[1] for p in pairs], jnp.int32)
    grid_spec = pltpu.PrefetchScalarGridSpec(
        num_scalar_prefetch=2,
        grid=(nseq, len(pairs)),
        in_specs=[pl.BlockSpec((1, nh, Tq, C_HEAD_W), lambda b, t, qi, kj: (b, 0, qi[t], 0)),
                  pl.BlockSpec((1, nh, Tk, C_HEAD_W), lambda b, t, qi, kj: (b, 0, kj[t], 0)),
                  pl.BlockSpec((1, nh, C_V, Tk), lambda b, t, qi, kj: (b, 0, 0, kj[t]))],
        out_specs=pl.BlockSpec((1, Tq, nh * C_V), lambda b, t, qi, kj: (b, qi[t], 0)),
        scratch_shapes=[pltpu.VMEM((nh, C_V, Tq), F32), pltpu.VMEM((nh, 1, Tq), F32),
                        pltpu.VMEM((nh, 1, Tq), F32)],
    )
    return pl.pallas_call(
        functools.partial(_flash_body, Tq=Tq, Tk=Tk, n_kv=nk, causal=causal, valid_len=valid_len),
        grid_spec=grid_spec,
        out_shape=jax.ShapeDtypeStruct((nseq, T, nh * C_V), BF),
        compiler_params=_cparams(2),
        name="mla_attn",
    )(qi, kj, q, k, vt)


def _rope_tables(pos):
    half = C_ROPE // 2
    inv = (np.float32(ROPE_BASE) ** (-np.arange(half, dtype=np.float32) / np.float32(half))).astype(np.float32)
    ang = (pos.astype(np.float32)[:, None] * inv[None, :]).astype(np.float32).astype(np.float64)
    n = pos.shape[0]
    cos = np.zeros((1, n, C_HEAD_W), np.float32)
    sin = np.zeros((1, n, C_HEAD_W), np.float32)
    cos[0, :, :C_NOPE] = 1.0
    cos[0, :, C_NOPE:C_NOPE + C_ROPE] = np.concatenate([np.cos(ang), np.cos(ang)], axis=1)
    sin[0, :, C_NOPE:C_NOPE + C_ROPE] = np.concatenate([np.sin(ang), np.sin(ang)], axis=1)
    return jnp.asarray(cos), jnp.asarray(sin)


def _rot_half_cols(w):
    half = w.shape[-1] // 2
    return jnp.concatenate([-w[..., half:], w[..., :half]], axis=-1)


def kernel(x_prompt, x_sample, c_prompt, c_sample, cache_a_k, cache_a_v, cache_b_k, cache_b_v, cache_c_kv, cache_c_kr, w_ada, b_ada, norm_g, final_norm_g, ffn_w_gate, ffn_w_up, ffn_w_down, w_in_ab, w_out_ab, rel_bias_a, t5_bias, sinks_b, w_in_c, c_q_norm_g, c_kv_norm_g, w_qb, w_kvb, w_out_c):
    nb, S, D = x_prompt.shape
    ns, TS, _ = x_sample.shape
    la_c, lb_c = cache_a_k.shape[2], cache_b_k.shape[2]
    past = cache_c_kv.shape[2]
    la_p, lb_p = min(A_PREV_CHUNKS * CHUNK, S), min(B_PREV_CHUNKS * CHUNK, S)
    assert la_p <= ROW_TILE and S % MLA_T == 0 and S % ROW_TILE == 0 and TS % 8 == 0

    n_cond = ns + nb
    n_cond_pad = -(-n_cond // 8) * 8
    c_all = jnp.zeros((n_cond_pad, D), F32).at[:ns].set(c_sample).at[ns:n_cond].set(c_prompt)
    mods_all = _adaln(c_all, w_ada, b_ada)
    groups = {
        "p": dict(x=x_prompt, blk_off=ns, G=1, R=ROW_TILE),
        "s": dict(x=x_sample, blk_off=0, G=ns, R=TS),
    }

    wg, wu, wd = (w.astype(BF) for w in (ffn_w_gate, ffn_w_up, ffn_w_down))
    qscale = HEAD_DIM ** -0.5
    col_scale = jnp.concatenate([jnp.full((A_W,), qscale, F32), jnp.ones((2 * A_W,), F32),
                                 jnp.full((BQ_W,), qscale, F32), jnp.ones((2 * BKV_W,), F32)])
    w_ab = (w_in_ab[0] * col_scale[None, :]).astype(BF)
    w_oa, w_ob = w_out_ab[0, :A_W].astype(BF), w_out_ab[0, A_W:].astype(BF)
    pair_a = tuple(range(A_HEADS // 2))
    pair_b = tuple(p // (B_Q_HEADS // B_KV_HEADS // 2) for p in range(B_Q_HEADS // 2))
    a_prev_rows = 2 * BAND_TQ
    b_prev_rows = BAND_TQ
    assert a_prev_rows == A_PREV_CHUNKS * CHUNK and b_prev_rows >= B_PREV_CHUNKS * CHUNK
    bias_ap = _bias_tile(rel_bias_a[0], "clip", BAND_TQ, a_prev_rows + BAND_TQ, a_prev_rows, A_PREV_CHUNKS)
    bias_bp = _bias_tile(t5_bias, "t5", BAND_TQ, b_prev_rows + BAND_TQ, b_prev_rows, B_PREV_CHUNKS)
    bias_as = _bias_tile(rel_bias_a[0], "clip", TS, la_c + TS, la_c, None)
    bias_bs = _bias_tile(t5_bias, "t5", TS, lb_c + TS, lb_c, None)
    dup = lambda c: jnp.repeat(c, 2, axis=2).reshape(c.shape[0], c.shape[1], 2 * BKV_W)
    cak = cache_a_k[0].reshape(ns, la_c, A_W)
    cav = cache_a_v[0].reshape(ns, la_c, A_W)
    cbk, cbv = dup(cache_b_k[0]), dup(cache_b_v[0])

    mods = mods_all[0].reshape(n_cond_pad, 1, -1)
    state0 = {}
    for name, gr in groups.items():
        x, off, G, R = gr["x"], gr["blk_off"] // gr["G"], gr["G"], gr["R"]
        x = _ffn(x, mods, off, 0, norm_g[0, 0], wg[0, 0], wu[0, 0], wd[0, 0], G, R)
        ta, tb = (la_p, lb_p) if name == "p" else (TS, TS)
        qa, ka, va, qb, kb, vb, ka32, va32, kb32, vb32 = _proj_ab(x, mods, off, norm_g[0, 1], w_ab, G, R, ta, tb)
        if name == "p":
            prev = lambda arr, n: [(arr, BAND_TQ, d - n) for d in range(n + 1)]
            oa = _band_attn(qa, prev(ka, 2), prev(va, 2), bias_ap, None,
                            Tq=BAND_TQ, n_prev_rows=a_prev_rows, pair_map=pair_a)
            ob = _band_attn(qb, prev(kb, 1), prev(vb, 1), bias_bp, sinks_b[0],
                            Tq=BAND_TQ, n_prev_rows=b_prev_rows, pair_map=pair_b)
        else:
            oa = _band_attn(qa, [(cak, la_c, None), (ka, TS, None)], [(cav, la_c, None), (va, TS, None)],
                            bias_as, None, Tq=TS, n_prev_rows=0, pair_map=pair_a)
            ob = _band_attn(qb, [(cbk, lb_c, None), (kb, TS, None)], [(cbv, lb_c, None), (vb, TS, None)],
                            bias_bs, sinks_b[0], Tq=TS, n_prev_rows=0, pair_map=pair_b)
        x = _out_proj([oa, ob], [w_oa, w_ob], x, mods, off, G, R)
        x = _ffn(x, mods, off, 6, norm_g[0, 2], wg[0, 1], wu[0, 1], wd[0, 1], G, R)
        gr["x"] = x
        n_out = x.shape[0]
        state0[name] = (ka32.reshape(1, n_out, ta, A_HEADS, HEAD_DIM), va32.reshape(1, n_out, ta, A_HEADS, HEAD_DIM),
                        kb32.reshape(1, n_out, tb, B_KV_HEADS, HEAD_DIM), vb32.reshape(1, n_out, tb, B_KV_HEADS, HEAD_DIM))

    hw = C_NOPE + C_ROPE
    w_in = w_in_c[0]
    w_kr = w_in[:, C_Q_LORA + C_KV_LORA:]
    win_ext = jnp.concatenate([w_in[:, :C_Q_LORA + C_KV_LORA], jnp.zeros((D, C_NOPE), F32),
                               w_kr, _rot_half_cols(w_kr)], axis=1).astype(BF)
    wq3 = w_qb[0].reshape(C_Q_LORA, C_HEADS, hw)
    wqb_ext = jnp.concatenate([wq3, _rot_half_cols(wq3[..., C_NOPE:])], axis=-1
                              ).reshape(C_Q_LORA, C_HEADS * C_HEAD_W).astype(BF)
    wkv3 = w_kvb[0].reshape(C_KV_LORA, C_HEADS, C_NOPE + C_V)
    wk_ext = jnp.concatenate([wkv3[..., :C_NOPE], jnp.zeros((C_KV_LORA, C_HEADS, C_HEAD_W - C_NOPE), F32)],
                             axis=-1).reshape(C_KV_LORA, C_HEADS * C_HEAD_W).astype(BF)
    wvt = wkv3[..., C_NOPE:].reshape(C_KV_LORA, C_HEADS * C_V).T.astype(BF)
    w_oc = w_out_c[0].astype(BF)
    cos_p, sin_p = _rope_tables(np.arange(S))
    cos_s, sin_s = _rope_tables(past + np.arange(TS))

    mods = mods_all[1].reshape(n_cond_pad, 1, -1)
    state1 = {}
    for name, gr in groups.items():
        x, off, G, R = gr["x"], gr["blk_off"] // gr["G"], gr["G"], gr["R"]
        x = _ffn(x, mods, off, 0, norm_g[1, 0], wg[1, 0], wu[1, 0], wd[1, 0], G, R)
        cos_t, sin_t = (cos_p, sin_p) if name == "p" else (cos_s, sin_s)
        q, kv32, kr32 = _proj_c(x, mods, off, norm_g[1, 1], win_ext, c_q_norm_g[0], c_kv_norm_g[0],
                                wqb_ext, cos_t, sin_t, G, R)
        if name == "p":
            k, vt = _kv_expand(kv32, kr32, wk_ext, wvt, ROW_TILE)
            o = _mla_attn(q, k, vt.reshape(nb, C_HEADS, C_V, S), Tq=MLA_T, Tk=MLA_T, causal=True)
        else:
            n_keys = past + TS
            n_pad = -(-n_keys // LANES) * LANES
            kv_all = jnp.zeros((ns, n_pad, C_KV_LORA), F32).at[:, :past].set(cache_c_kv[0]).at[:, past:n_keys].set(kv32)
            kr_all = jnp.zeros((ns, n_pad, C_ROPE), F32).at[:, :past].set(cache_c_kr[0]).at[:, past:n_keys].set(kr32)
            k, vt = _kv_expand(kv_all, kr_all, wk_ext, wvt, n_pad)
            o = _mla_attn(q, k, vt.reshape(ns, C_HEADS, C_V, n_pad), Tq=TS, Tk=n_pad, causal=False,
                          valid_len=n_keys)
        x = _out_proj([o], [w_oc], x, mods, off, G, R)
        x = _ffn(x, mods, off, 6, norm_g[1, 2], wg[1, 1], wu[1, 1], wd[1, 1], G, R, final_g=final_norm_g)
        gr["x"] = x
        state1[name] = (kv32[None], kr32[None])

    return (groups["p"]["x"], groups["s"]["x"],
            *state0["p"], *state1["p"], *state0["s"], *state1["s"])
```

```python
import functools
import math

import numpy as np
import jax
import jax.numpy as jnp
from jax import lax
from jax.experimental import pallas as pl
from jax.experimental.pallas import tpu as pltpu

F32 = jnp.float32
BF = jnp.bfloat16

CHUNK = 64
HEAD_DIM = 64
A_HEADS = 8
A_PREV_CHUNKS = 8
A_REL_CLIP = 128
B_Q_HEADS = 8
B_KV_HEADS = 2
B_PREV_CHUNKS = 2
T5_BUCKETS = 32
T5_MAX_DIST = 128
C_HEADS = 16
C_Q_LORA = 384
C_KV_LORA = 256
C_NOPE = 64
C_ROPE = 32
C_V = 64
ROPE_BASE = 10000.0
EPS = 1e-6
NEG = -1e30
A_W = A_HEADS * HEAD_DIM
BQ_W = B_Q_HEADS * HEAD_DIM
BKV_W = B_KV_HEADS * HEAD_DIM

LANES = 128
VMEM_LIMIT = 56 * 1024 * 1024

ROW_TILE = 512
FF_CHUNK = 256
BAND_TQ = 256
MLA_T = 1024
MLA_SK = 128
MLA_QW = 256
MLA_AHEAD = 7
C_VT_ROWS = 80
C_IN_EXT = 768
C_HEAD_W = 128


def _cparams(n_axes, vmem=VMEM_LIMIT):
    return pltpu.CompilerParams(dimension_semantics=("arbitrary",) * n_axes,
                                vmem_limit_bytes=vmem)


def _const_spec(shape):
    n = len(shape)
    return pl.BlockSpec(shape, lambda *_: (0,) * n)


def _rms(x):
    return x * lax.rsqrt(jnp.mean(x * x, axis=-1, keepdims=True) + EPS)


def _norm_mod(x, g, shift, scale):
    return (_rms(x) * g) * (1.0 + scale) + shift


def _mod_spec(G, blk_off, k, D):
    return pl.BlockSpec((G, 1, D), lambda i, j: (blk_off + i, 0, k))


def _adaln_body(c_ref, w_ref, b_ref, o_ref):
    c = c_ref[...]
    s = (c * jax.nn.sigmoid(c)).astype(BF)
    o_ref[0] = jnp.dot(s, w_ref[0].astype(BF), preferred_element_type=F32) + b_ref[0]


def _adaln(c_all, w_ada, b_ada):
    L, D, N = w_ada.shape
    R = c_all.shape[0]
    tn = 1024
    return pl.pallas_call(
        _adaln_body,
        grid=(L, N // tn),
        in_specs=[pl.BlockSpec((R, D), lambda l, j: (0, 0)),
                  pl.BlockSpec((1, D, tn), lambda l, j: (l, 0, j)),
                  pl.BlockSpec((1, 1, tn), lambda l, j: (l, 0, j))],
        out_specs=pl.BlockSpec((1, R, tn), lambda l, j: (l, 0, j)),
        out_shape=jax.ShapeDtypeStruct((L, R, N), F32),
        compiler_params=_cparams(2),
        name="adaln",
    )(c_all, w_ada, b_ada.reshape(L, 1, N))


def _ffn_body(x_ref, sh_ref, sc_ref, gt_ref, g_ref, wg_ref, wu_ref, wd_ref, *rest, final):
    if final:
        fg_ref, o_ref, a_ref = rest
    else:
        o_ref, a_ref = rest
    x = x_ref[...]
    G, R, D = x.shape
    F = wg_ref.shape[1]
    hb = _norm_mod(x, g_ref[...], sh_ref[...], sc_ref[...]).reshape(G * R, D).astype(BF)
    for c in range(F // FF_CHUNK):
        lo, hi = c * FF_CHUNK, (c + 1) * FF_CHUNK
        g = jnp.dot(hb, wg_ref[:, lo:hi], preferred_element_type=F32)
        u = jnp.dot(hb, wu_ref[:, lo:hi], preferred_element_type=F32)
        a_ref[:, lo:hi] = (g * jax.nn.sigmoid(g) * u).astype(BF)
    ff = jnp.dot(a_ref[...], wd_ref[...], preferred_element_type=F32)
    y = x + (0.5 * gt_ref[...]) * ff.reshape(G, R, D)
    if final:
        y = _rms(y) * fg_ref[...]
    o_ref[...] = y


def _ffn(x, mods, blk_off, kbase, g, wg, wu, wd, G, R, final_g=None):
    nseq, T, D = x.shape
    F = wg.shape[1]
    final = final_g is not None
    in_specs = [pl.BlockSpec((G, R, D), lambda i, j: (i, j, 0)),
                _mod_spec(G, blk_off, kbase, D), _mod_spec(G, blk_off, kbase + 1, D),
                _mod_spec(G, blk_off, kbase + 2, D),
                _const_spec((1, D)), _const_spec((D, F)), _const_spec((D, F)), _const_spec((F, D))]
    args = [x, mods, mods, mods, g.reshape(1, D), wg, wu, wd]
    if final:
        in_specs.append(_const_spec((1, D)))
        args.append(final_g.reshape(1, D))
    return pl.pallas_call(
        functools.partial(_ffn_body, final=final),
        grid=(nseq // G, T // R),
        in_specs=in_specs,
        out_specs=pl.BlockSpec((G, R, D), lambda i, j: (i, j, 0)),
        out_shape=jax.ShapeDtypeStruct((nseq, T, D), F32),
        scratch_shapes=[pltpu.VMEM((G * R, F), BF)],
        compiler_params=_cparams(2),
        name="ffn",
    )(*args)


def _dup_halves(k):
    rolled = pltpu.roll(k, HEAD_DIM, 1)
    lo = lax.broadcasted_iota(jnp.int32, k.shape, 1) < HEAD_DIM
    return jnp.concatenate([jnp.where(lo, k, rolled), jnp.where(lo, rolled, k)], axis=1)


def _proj_ab_body(x_ref, sh_ref, sc_ref, g_ref, w_ref,
                  qa_ref, ka_ref, va_ref, qb_ref, kb_ref, vb_ref,
                  ka32_ref, va32_ref, kb32_ref, vb32_ref, *, ta, tb):
    x = x_ref[...]
    G, R, D = x.shape
    hb = _norm_mod(x, g_ref[...], sh_ref[...], sc_ref[...]).reshape(G * R, D).astype(BF)
    res = jnp.dot(hb, w_ref[...], preferred_element_type=F32)
    o_ka, o_va, o_qb, o_kb, o_vb = A_W, 2 * A_W, 3 * A_W, 3 * A_W + BQ_W, 3 * A_W + BQ_W + BKV_W
    qa_ref[...] = res[:, :o_ka].astype(BF).reshape(G, R, A_W)
    ka_ref[...] = res[:, o_ka:o_va].astype(BF).reshape(G, R, A_W)
    va_ref[...] = res[:, o_va:o_qb].astype(BF).reshape(G, R, A_W)
    qb_ref[...] = res[:, o_qb:o_kb].astype(BF).reshape(G, R, BQ_W)
    kb = res[:, o_kb:o_vb]
    vb = res[:, o_vb:o_vb + BKV_W]
    kb_ref[...] = _dup_halves(kb).astype(BF).reshape(G, R, 2 * BKV_W)
    vb_ref[...] = _dup_halves(vb).astype(BF).reshape(G, R, 2 * BKV_W)

    @pl.when(pl.program_id(1) == pl.num_programs(1) - 1)
    def _():
        ka32_ref[...] = res[:, o_ka:o_va].reshape(G, R, A_W)[:, R - ta:, :]
        va32_ref[...] = res[:, o_va:o_qb].reshape(G, R, A_W)[:, R - ta:, :]
        kb32_ref[...] = kb.reshape(G, R, BKV_W)[:, R - tb:, :]
        vb32_ref[...] = vb.reshape(G, R, BKV_W)[:, R - tb:, :]


def _proj_ab(x, mods, blk_off, g, w, G, R, ta, tb):
    nseq, T, D = x.shape
    row = lambda W: pl.BlockSpec((G, R, W), lambda i, j: (i, j, 0))
    tail = lambda t, W: pl.BlockSpec((G, t, W), lambda i, j: (i, 0, 0))
    bshape = lambda W: jax.ShapeDtypeStruct((nseq, T, W), BF)
    return pl.pallas_call(
        functools.partial(_proj_ab_body, ta=ta, tb=tb),
        grid=(nseq // G, T // R),
        in_specs=[row(D), _mod_spec(G, blk_off, 3, D), _mod_spec(G, blk_off, 4, D),
                  _const_spec((1, D)), _const_spec(w.shape)],
        out_specs=[row(A_W), row(A_W), row(A_W), row(BQ_W), row(2 * BKV_W), row(2 * BKV_W),
                   tail(ta, A_W), tail(ta, A_W), tail(tb, BKV_W), tail(tb, BKV_W)],
        out_shape=[bshape(A_W), bshape(A_W), bshape(A_W), bshape(BQ_W), bshape(2 * BKV_W), bshape(2 * BKV_W),
                   jax.ShapeDtypeStruct((nseq, ta, A_W), F32), jax.ShapeDtypeStruct((nseq, ta, A_W), F32),
                   jax.ShapeDtypeStruct((nseq, tb, BKV_W), F32), jax.ShapeDtypeStruct((nseq, tb, BKV_W), F32)],
        compiler_params=_cparams(2),
        name="proj_ab",
    )(x, mods, mods, g.reshape(1, D), w)


def _t5_bucket_np(rel):
    nb = T5_BUCKETS // 2
    ret = np.where(rel > 0, nb, 0)
    n = np.abs(rel)
    max_exact = nb // 2
    ratio = np.maximum(n, 1).astype(np.float32) / np.float32(max_exact)
    large = max_exact + (np.log(ratio).astype(np.float32) / np.float32(math.log(T5_MAX_DIST / max_exact))
                         * np.float32(nb - max_exact)).astype(np.int32)
    large = np.minimum(large, nb - 1)
    return ret + np.where(n < max_exact, n, large)


def _bias_body(thi_ref, tlo_ref, idx_ref, o_ref, *, Tq, span, prev_chunks, band_chunks):
    E = thi_ref.shape[1]
    L = idx_ref.shape[1]
    onehot = (lax.broadcasted_iota(jnp.int32, (E, L), 0) == idx_ref[...]).astype(BF)
    t = (jnp.dot(thi_ref[...], onehot, preferred_element_type=F32)
         + jnp.dot(tlo_ref[...], onehot, preferred_element_type=F32))
    if band_chunks is not None:
        qc = lax.broadcasted_iota(jnp.int32, (Tq, span), 0) // CHUNK
        kc = lax.broadcasted_iota(jnp.int32, (Tq, span), 1) // CHUNK - prev_chunks
        valid = (kc <= qc) & (kc >= qc - band_chunks)
    for h in range(o_ref.shape[0]):
        x = jnp.broadcast_to(t[h:h + 1, :], (Tq, L))
        y = pltpu.roll(x, L - Tq + 1, 1, stride=1, stride_axis=0)[:, :span]
        if band_chunks is not None:
            y = jnp.where(valid, y, NEG)
        o_ref[h] = y


def _bias_tile(table, kind, Tq, span, n_prev_rows, band_chunks):
    H = table.shape[1]
    L = -(-(Tq + span - 1) // LANES) * LANES
    rel = np.arange(L) - (Tq - 1) - n_prev_rows
    if kind == "clip":
        idx = np.clip(rel, -A_REL_CLIP, A_REL_CLIP) + A_REL_CLIP
    else:
        idx = _t5_bucket_np(rel)
    E = -(-table.shape[0] // LANES) * LANES
    tt = jnp.zeros((H, E), F32).at[:, :table.shape[0]].set(table.T.astype(F32))
    thi = tt.astype(BF)
    tlo = (tt - thi.astype(F32)).astype(BF)
    return pl.pallas_call(
        functools.partial(_bias_body, Tq=Tq, span=span, prev_chunks=n_prev_rows // CHUNK,
                          band_chunks=band_chunks),
        out_shape=jax.ShapeDtypeStruct((H, Tq, span), F32),
        compiler_params=pltpu.CompilerParams(vmem_limit_bytes=VMEM_LIMIT),
        name="bias_tile",
    )(thi, tlo, jnp.asarray(idx.reshape(1, L), jnp.int32))


def _band_body(*refs, n_parts, n_prev_rows, pair_map, has_sink, Tq):
    q_ref = refs[0]
    k_refs = refs[1:1 + n_parts]
    v_refs = refs[1 + n_parts:1 + 2 * n_parts]
    bias_ref = refs[1 + 2 * n_parts]
    sink_ref = refs[2 + 2 * n_parts] if has_sink else None
    o_ref = refs[-1]
    i = pl.program_id(1)
    k = jnp.concatenate([r[0].astype(BF) for r in k_refs], axis=0)
    v = jnp.concatenate([r[0].astype(BF) for r in v_refs], axis=0)
    q = q_ref[0]
    span = k.shape[0]
    lane_lo = lax.broadcasted_iota(jnp.int32, (1, LANES), 1) < HEAD_DIM
    if n_prev_rows:
        col_ok = lax.broadcasted_iota(jnp.int32, (1, span), 1) >= (n_prev_rows - i * Tq)
    for p in range(q.shape[1] // LANES):
        qp = q[:, p * LANES:(p + 1) * LANES]
        kp = k[:, pair_map[p] * LANES:(pair_map[p] + 1) * LANES]
        vp = v[:, pair_map[p] * LANES:(pair_map[p] + 1) * LANES]
        halves = []
        for hh in range(2):
            h = 2 * p + hh
            qm = jnp.where(lane_lo if hh == 0 else jnp.logical_not(lane_lo), qp, jnp.zeros_like(qp))
            s = lax.dot_general(qm, kp, (((1,), (1,)), ((), ())), preferred_element_type=F32)
            s = s + bias_ref[h]
            if n_prev_rows:
                s = jnp.where(col_ok, s, NEG)
            m = jnp.max(s, axis=-1, keepdims=True)
            if has_sink:
                m = jnp.maximum(m, sink_ref[h])
            e = jnp.exp(s - m)
            den = jnp.sum(e, axis=-1, keepdims=True)
            if has_sink:
                den = den + jnp.exp(sink_ref[h] - m)
            o = jnp.dot(e.astype(BF), vp, preferred_element_type=F32)
            halves.append(o * (1.0 / den))
        o_ref[0, :, p * LANES:(p + 1) * LANES] = jnp.where(lane_lo, halves[0], halves[1]).astype(BF)


def _band_attn(q, kparts, vparts, bias, sinks, *, Tq, n_prev_rows, pair_map):
    nseq, T, W = q.shape

    def part_spec(arr, rows, off):
        if off is None:
            return pl.BlockSpec((1, rows, arr.shape[2]), lambda b, i: (b, 0, 0))
        return pl.BlockSpec((1, rows, arr.shape[2]), lambda b, i: (b, jnp.maximum(i + off, 0), 0))

    in_specs = [pl.BlockSpec((1, Tq, W), lambda b, i: (b, i, 0))]
    in_specs += [part_spec(*p) for p in kparts] + [part_spec(*p) for p in vparts]
    in_specs.append(_const_spec(bias.shape))
    args = [q] + [p[0] for p in kparts] + [p[0] for p in vparts] + [bias]
    if sinks is not None:
        in_specs.append(pl.BlockSpec(memory_space=pltpu.SMEM))
        args.append(sinks.astype(F32))
    return pl.pallas_call(
        functools.partial(_band_body, n_parts=len(kparts), n_prev_rows=n_prev_rows,
                          pair_map=pair_map, has_sink=sinks is not None, Tq=Tq),
        grid=(nseq, T // Tq),
        in_specs=in_specs,
        out_specs=pl.BlockSpec((1, Tq, W), lambda b, i: (b, i, 0)),
        out_shape=jax.ShapeDtypeStruct((nseq, T, W), BF),
        compiler_params=_cparams(2),
        name="band_attn",
    )(*args)


def _out_proj_body(*refs, n_in):
    o_refs = refs[:n_in]
    w_refs = refs[n_in:2 * n_in]
    x_ref, gt_ref, out_ref = refs[2 * n_in:]
    x = x_ref[...]
    G, R, D = x.shape
    acc = None
    for o_ref, w_ref in zip(o_refs, w_refs):
        y = jnp.dot(o_ref[...].reshape(G * R, o_ref.shape[2]), w_ref[...], preferred_element_type=F32)
        acc = y if acc is None else acc + y
    out_ref[...] = x + gt_ref[...] * acc.reshape(G, R, D)


def _out_proj(os_, ws, x, mods, blk_off, G, R):
    nseq, T, D = x.shape
    row = lambda W: pl.BlockSpec((G, R, W), lambda i, j: (i, j, 0))
    return pl.pallas_call(
        functools.partial(_out_proj_body, n_in=len(os_)),
        grid=(nseq // G, T // R),
        in_specs=[row(o.shape[2]) for o in os_] + [_const_spec(w.shape) for w in ws]
                 + [row(D), _mod_spec(G, blk_off, 5, D)],
        out_specs=row(D),
        out_shape=jax.ShapeDtypeStruct((nseq, T, D), F32),
        compiler_params=_cparams(2),
        name="out_proj",
    )(*os_, *ws, x, mods)


def _proj_c_body(x_ref, sh_ref, sc_ref, g_ref, win_ref, qn_ref, kvn_ref, wqb_ref, cos_ref, sin_ref,
                 cost_ref, sint_ref, q_ref, kv32_ref, kr32_ref, *, q_transposed):
    x = x_ref[...]
    G, R, D = x.shape
    hb = _norm_mod(x, g_ref[...], sh_ref[...], sc_ref[...]).reshape(G * R, D).astype(BF)
    res = jnp.dot(hb, win_ref[...], preferred_element_type=F32)
    q_lat = res[:, :C_Q_LORA]
    kv_lat = res[:, C_Q_LORA:C_Q_LORA + C_KV_LORA]
    krg = res[:, C_Q_LORA + C_KV_LORA:]
    cos = cos_ref[...]
    sin = sin_ref[...]
    qn = (_rms(q_lat) * qn_ref[...]).astype(BF)
    scale = (C_NOPE + C_ROPE) ** -0.5 * math.log2(math.e)
    nh = q_ref.shape[1]
    if q_transposed:
        qt = lax.dot_general(wqb_ref[...], qn, (((1,), (1,)), ((), ())), preferred_element_type=F32)
        cost, sint = cost_ref[...], sint_ref[...]
        for h in range(nh):
            blk = qt[h * C_HEAD_W:(h + 1) * C_HEAD_W]
            roped = blk[C_NOPE:C_NOPE + C_ROPE] * cost + blk[C_NOPE + C_ROPE:] * sint
            q_ref[0, h] = (jnp.concatenate([blk[:C_NOPE], roped, jnp.zeros_like(roped)], axis=0) * scale).astype(BF)
    else:
        qr = jnp.dot(qn, wqb_ref[...], preferred_element_type=F32)
        qrot = pltpu.roll(qr, qr.shape[1] - C_ROPE, 1)
        for h in range(nh):
            a = qr[:, h * C_HEAD_W:(h + 1) * C_HEAD_W].reshape(G, R, C_HEAD_W)
            b = qrot[:, h * C_HEAD_W:(h + 1) * C_HEAD_W].reshape(G, R, C_HEAD_W)
            q_ref[:, h] = ((a * cos + b * sin) * scale).astype(BF)
    kv32_ref[...] = (_rms(kv_lat) * kvn_ref[...]).reshape(G, R, C_KV_LORA)
    krot = pltpu.roll(krg, C_HEAD_W - C_ROPE, 1)
    krf = krg.reshape(G, R, C_HEAD_W) * cos + krot.reshape(G, R, C_HEAD_W) * sin
    kr32_ref[...] = krf[:, :, C_NOPE:C_NOPE + C_ROPE]


def _proj_c(x, mods, blk_off, g, win, qn_g, kvn_g, wqb, tables, G, R, q_transposed):
    nseq, T, D = x.shape
    nh = C_HEADS
    cos_t, sin_t, cost_t, sint_t = tables
    tab = pl.BlockSpec((1, R, C_HEAD_W), lambda i, j: (0, j, 0))
    tabt = pl.BlockSpec((C_ROPE, R), lambda i, j: (0, j))
    if q_transposed:
        assert G == 1
        wqb = wqb.T
        q_spec = pl.BlockSpec((1, nh, C_HEAD_W, R), lambda i, j: (i, 0, 0, j))
        q_shape = jax.ShapeDtypeStruct((nseq, nh, C_HEAD_W, T), BF)
    else:
        q_spec = pl.BlockSpec((G, nh, R, C_HEAD_W), lambda i, j: (i, 0, j, 0))
        q_shape = jax.ShapeDtypeStruct((nseq, nh, T, C_HEAD_W), BF)
    return pl.pallas_call(
        functools.partial(_proj_c_body, q_transposed=q_transposed),
        grid=(nseq // G, T // R),
        in_specs=[pl.BlockSpec((G, R, D), lambda i, j: (i, j, 0)),
                  _mod_spec(G, blk_off, 3, D), _mod_spec(G, blk_off, 4, D), _const_spec((1, D)),
                  _const_spec(win.shape), _const_spec((1, C_Q_LORA)), _const_spec((1, C_KV_LORA)),
                  _const_spec(wqb.shape), tab, tab, tabt, tabt],
        out_specs=[q_spec,
                   pl.BlockSpec((G, R, C_KV_LORA), lambda i, j: (i, j, 0)),
                   pl.BlockSpec((G, R, C_ROPE), lambda i, j: (i, j, 0))],
        out_shape=[q_shape,
                   jax.ShapeDtypeStruct((nseq, T, C_KV_LORA), F32),
                   jax.ShapeDtypeStruct((nseq, T, C_ROPE), F32)],
        compiler_params=_cparams(2),
        name="proj_c",
    )(x, mods, mods, g.reshape(1, D), win, qn_g.reshape(1, -1), kvn_g.reshape(1, -1), wqb,
      cos_t, sin_t, cost_t, sint_t)


def _kvexp_body(kv_ref, kr_ref, wk_ref, wvt_ref, place_ref, ones_ref, k_ref, vt_ref):
    kv = kv_ref[0].astype(BF)
    krf = jnp.dot(kr_ref[0].astype(BF), place_ref[...], preferred_element_type=F32)
    kx = jnp.dot(kv, wk_ref[...], preferred_element_type=F32)
    for h in range(k_ref.shape[1]):
        k_ref[0, h] = (kx[:, h * C_HEAD_W:(h + 1) * C_HEAD_W] + krf).astype(BF)
    vt = lax.dot_general(wvt_ref[...], kv, (((1,), (1,)), ((), ())), preferred_element_type=F32)
    vt_ref[0] = (vt + ones_ref[...]).astype(BF)


def _kv_expand(kv, kr, wk, wvt, R):
    nseq, T, _ = kv.shape
    nh = wk.shape[1] // C_HEAD_W
    place = np.zeros((C_ROPE, C_HEAD_W), np.float32)
    place[np.arange(C_ROPE), C_NOPE + np.arange(C_ROPE)] = 1.0
    ones = np.zeros((nh, C_VT_ROWS, 1), np.float32)
    ones[:, C_V] = 1.0
    ones = ones.reshape(nh * C_VT_ROWS, 1)
    return pl.pallas_call(
        _kvexp_body,
        grid=(nseq, T // R),
        in_specs=[pl.BlockSpec((1, R, C_KV_LORA), lambda b, j: (b, j, 0)),
                  pl.BlockSpec((1, R, C_ROPE), lambda b, j: (b, j, 0)),
                  _const_spec(wk.shape), _const_spec(wvt.shape), _const_spec(place.shape),
                  _const_spec(ones.shape)],
        out_specs=[pl.BlockSpec((1, nh, R, C_HEAD_W), lambda b, j: (b, 0, j, 0)),
                   pl.BlockSpec((1, nh * C_VT_ROWS, R), lambda b, j: (b, 0, j))],
        out_shape=[jax.ShapeDtypeStruct((nseq, nh, T, C_HEAD_W), BF),
                   jax.ShapeDtypeStruct((nseq, nh * C_VT_ROWS, T), BF)],
        compiler_params=_cparams(2),
        name="kv_expand",
    )(kv, kr, wk, wvt, jnp.asarray(place, BF), jnp.asarray(ones))


def _flash_body(qi_ref, kj_ref, q_ref, k_ref, vt_ref, o_ref, acc_ref, m_ref,
                *, Tq, Tk, SK, QW, causal, valid_len):
    t = pl.program_id(1)
    i = qi_ref[t]
    j = kj_ref[t]
    nh = q_ref.shape[1]
    n_stripes = Tq // QW

    @pl.when(j == 0)
    def _():
        m_ref[...] = jnp.full(m_ref.shape, NEG, F32)
        acc_ref[...] = jnp.zeros(acc_ref.shape, F32)

    def sweep(diag):
        tiles = []
        n_sub = Tk // SK if valid_len is None else -(-valid_len // SK)
        for c in range(n_sub):
            for r in range(n_stripes):
                if diag:
                    if c * SK >= (r + 1) * QW:
                        continue
                    mask = "chunk" if (c + 1) * SK > r * QW + CHUNK else None
                else:
                    mask = "len" if valid_len is not None and (c + 1) * SK > valid_len else None
                tiles.append((c, r, mask))

        def head(h, carry):
            state = [(m_ref[h, :, r * QW:(r + 1) * QW], acc_ref[h, :, r * QW:(r + 1) * QW])
                     for r in range(n_stripes)]

            def scores(c, r, mask):
                s = jnp.dot(k_ref[0, h, c * SK:(c + 1) * SK, :], q_ref[0, h, :, r * QW:(r + 1) * QW],
                            preferred_element_type=F32)
                kpos = lax.broadcasted_iota(jnp.int32, (SK, 1), 0) + c * SK
                if mask == "chunk":
                    qpos = lax.broadcasted_iota(jnp.int32, (1, QW), 1) + r * QW
                    s = jnp.where(kpos // CHUNK <= qpos // CHUNK, s, NEG)
                elif mask == "len":
                    s = jnp.where(kpos < valid_len, s, NEG)
                return s

            pending = [scores(*tl) for tl in tiles[:MLA_AHEAD]]
            for n, (c, r, _) in enumerate(tiles):
                s = pending.pop(0)
                if n + MLA_AHEAD < len(tiles):
                    pending.append(scores(*tiles[n + MLA_AHEAD]))
                m, acc = state[r]
                m_new = jnp.maximum(m, jnp.max(s, axis=0, keepdims=True))
                p = jnp.exp2(s - m_new).astype(BF)
                acc = jnp.exp2(m - m_new) * acc + jnp.dot(vt_ref[0, h, :, c * SK:(c + 1) * SK], p,
                                                          preferred_element_type=F32)
                state[r] = (m_new, acc)
            for r in range(n_stripes):
                m_ref[h, :, r * QW:(r + 1) * QW], acc_ref[h, :, r * QW:(r + 1) * QW] = state[r]
            return carry
        lax.fori_loop(0, nh, head, 0)

    if causal:
        pl.when(j == i)(lambda: sweep(True))
        pl.when(j != i)(lambda: sweep(False))
        last = i
    else:
        sweep(False)
        last = 0

    @pl.when(j == last)
    def _():
        for p in range(nh // 2):
            halves = [acc_ref[h, :C_V, :] * (1.0 / acc_ref[h, C_V:C_V + 1, :]) for h in (2 * p, 2 * p + 1)]
            o_ref[0, :, p * LANES:(p + 1) * LANES] = jnp.concatenate(halves, axis=0).T.astype(BF)


def _mla_attn(q, k, vt, *, Tq, Tk, SK, QW, causal, valid_len=None):
    nseq, nh, _, T = q.shape
    S = k.shape[2]
    nq, nk = T // Tq, S // Tk
    assert Tq % QW == 0 and Tk % SK == 0 and SK % CHUNK == 0
    if causal:
        assert Tq == Tk and QW % CHUNK == 0 and valid_len is None
        pairs = [(i, j) for i in range(nq) for j in range(i + 1)]
    else:
        assert nk == 1
        pairs = [(i, 0) for i in range(nq)]
    qi = jnp.asarray([p[0] for p in pairs], jnp.int32)
    kj = jnp.asarray([p[1] for p in pairs], jnp.int32)
    grid_spec = pltpu.PrefetchScalarGridSpec(
        num_scalar_prefetch=2,
        grid=(nseq, len(pairs)),
        in_specs=[pl.BlockSpec((1, nh, C_HEAD_W, Tq), lambda b, t, qi, kj: (b, 0, 0, qi[t])),
                  pl.BlockSpec((1, nh, Tk, C_HEAD_W), lambda b, t, qi, kj: (b, 0, kj[t], 0)),
                  pl.BlockSpec((1, nh, C_VT_ROWS, Tk), lambda b, t, qi, kj: (b, 0, 0, kj[t]))],
        out_specs=pl.BlockSpec((1, Tq, nh * C_V), lambda b, t, qi, kj: (b, qi[t], 0)),
        scratch_shapes=[pltpu.VMEM((nh, C_VT_ROWS, Tq), F32), pltpu.VMEM((nh, 1, Tq), F32)],
    )
    return pl.pallas_call(
        functools.partial(_flash_body, Tq=Tq, Tk=Tk, SK=SK, QW=QW, causal=causal, valid_len=valid_len),
        grid_spec=grid_spec,
        out_shape=jax.ShapeDtypeStruct((nseq, T, nh * C_V), BF),
        compiler_params=_cparams(2),
        name="mla_attn",
    )(qi, kj, q, k, vt)


def _rope_tables(pos):
    half = C_ROPE // 2
    inv = (np.float32(ROPE_BASE) ** (-np.arange(half, dtype=np.float32) / np.float32(half))).astype(np.float32)
    ang = (pos.astype(np.float32)[:, None] * inv[None, :]).astype(np.float32).astype(np.float64)
    n = pos.shape[0]
    cos = np.zeros((1, n, C_HEAD_W), np.float32)
    sin = np.zeros((1, n, C_HEAD_W), np.float32)
    cos[0, :, :C_NOPE] = 1.0
    cos[0, :, C_NOPE:C_NOPE + C_ROPE] = np.concatenate([np.cos(ang), np.cos(ang)], axis=1)
    sin[0, :, C_NOPE:C_NOPE + C_ROPE] = np.concatenate([np.sin(ang), np.sin(ang)], axis=1)
    cos32, sin32 = cos[0, :, C_NOPE:C_NOPE + C_ROPE].T, sin[0, :, C_NOPE:C_NOPE + C_ROPE].T
    return jnp.asarray(cos), jnp.asarray(sin), jnp.asarray(cos32), jnp.asarray(sin32)


def _rot_half_cols(w):
    half = w.shape[-1] // 2
    return jnp.concatenate([-w[..., half:], w[..., :half]], axis=-1)


def kernel(x_prompt, x_sample, c_prompt, c_sample, cache_a_k, cache_a_v, cache_b_k, cache_b_v, cache_c_kv, cache_c_kr, w_ada, b_ada, norm_g, final_norm_g, ffn_w_gate, ffn_w_up, ffn_w_down, w_in_ab, w_out_ab, rel_bias_a, t5_bias, sinks_b, w_in_c, c_q_norm_g, c_kv_norm_g, w_qb, w_kvb, w_out_c):
    nb, S, D = x_prompt.shape
    ns, TS, _ = x_sample.shape
    la_c, lb_c = cache_a_k.shape[2], cache_b_k.shape[2]
    past = cache_c_kv.shape[2]
    la_p, lb_p = min(A_PREV_CHUNKS * CHUNK, S), min(B_PREV_CHUNKS * CHUNK, S)
    assert la_p <= ROW_TILE and S % MLA_T == 0 and S % ROW_TILE == 0 and TS % 8 == 0

    n_cond = ns + nb
    n_cond_pad = -(-n_cond // 8) * 8
    c_all = jnp.zeros((n_cond_pad, D), F32).at[:ns].set(c_sample).at[ns:n_cond].set(c_prompt)
    mods_all = _adaln(c_all, w_ada, b_ada)
    groups = {
        "p": dict(x=x_prompt, blk_off=ns, G=1, R=ROW_TILE),
        "s": dict(x=x_sample, blk_off=0, G=ns, R=TS),
    }

    wg, wu, wd = (w.astype(BF) for w in (ffn_w_gate, ffn_w_up, ffn_w_down))
    qscale = HEAD_DIM ** -0.5
    col_scale = jnp.concatenate([jnp.full((A_W,), qscale, F32), jnp.ones((2 * A_W,), F32),
                                 jnp.full((BQ_W,), qscale, F32), jnp.ones((2 * BKV_W,), F32)])
    w_ab = (w_in_ab[0] * col_scale[None, :]).astype(BF)
    w_oa, w_ob = w_out_ab[0, :A_W].astype(BF), w_out_ab[0, A_W:].astype(BF)
    pair_a = tuple(range(A_HEADS // 2))
    pair_b = tuple(p // (B_Q_HEADS // B_KV_HEADS // 2) for p in range(B_Q_HEADS // 2))
    a_prev_rows = 2 * BAND_TQ
    b_prev_rows = BAND_TQ
    assert a_prev_rows == A_PREV_CHUNKS * CHUNK and b_prev_rows >= B_PREV_CHUNKS * CHUNK
    bias_ap = _bias_tile(rel_bias_a[0], "clip", BAND_TQ, a_prev_rows + BAND_TQ, a_prev_rows, A_PREV_CHUNKS)
    bias_bp = _bias_tile(t5_bias, "t5", BAND_TQ, b_prev_rows + BAND_TQ, b_prev_rows, B_PREV_CHUNKS)
    bias_as = _bias_tile(rel_bias_a[0], "clip", TS, la_c + TS, la_c, None)
    bias_bs = _bias_tile(t5_bias, "t5", TS, lb_c + TS, lb_c, None)
    dup = lambda c: jnp.repeat(c, 2, axis=2).reshape(c.shape[0], c.shape[1], 2 * BKV_W)
    cak = cache_a_k[0].reshape(ns, la_c, A_W)
    cav = cache_a_v[0].reshape(ns, la_c, A_W)
    cbk, cbv = dup(cache_b_k[0]), dup(cache_b_v[0])

    mods = mods_all[0].reshape(n_cond_pad, 1, -1)
    state0 = {}
    for name, gr in groups.items():
        x, off, G, R = gr["x"], gr["blk_off"] // gr["G"], gr["G"], gr["R"]
        x = _ffn(x, mods, off, 0, norm_g[0, 0], wg[0, 0], wu[0, 0], wd[0, 0], G, R)
        ta, tb = (la_p, lb_p) if name == "p" else (TS, TS)
        qa, ka, va, qb, kb, vb, ka32, va32, kb32, vb32 = _proj_ab(x, mods, off, norm_g[0, 1], w_ab, G, R, ta, tb)
        if name == "p":
            prev = lambda arr, n: [(arr, BAND_TQ, d - n) for d in range(n + 1)]
            oa = _band_attn(qa, prev(ka, 2), prev(va, 2), bias_ap, None,
                            Tq=BAND_TQ, n_prev_rows=a_prev_rows, pair_map=pair_a)
            ob = _band_attn(qb, prev(kb, 1), prev(vb, 1), bias_bp, sinks_b[0],
                            Tq=BAND_TQ, n_prev_rows=b_prev_rows, pair_map=pair_b)
        else:
            oa = _band_attn(qa, [(cak, la_c, None), (ka, TS, None)], [(cav, la_c, None), (va, TS, None)],
                            bias_as, None, Tq=TS, n_prev_rows=0, pair_map=pair_a)
            ob = _band_attn(qb, [(cbk, lb_c, None), (kb, TS, None)], [(cbv, lb_c, None), (vb, TS, None)],
                            bias_bs, sinks_b[0], Tq=TS, n_prev_rows=0, pair_map=pair_b)
        x = _out_proj([oa, ob], [w_oa, w_ob], x, mods, off, G, R)
        x = _ffn(x, mods, off, 6, norm_g[0, 2], wg[0, 1], wu[0, 1], wd[0, 1], G, R)
        gr["x"] = x
        n_out = x.shape[0]
        state0[name] = (ka32.reshape(1, n_out, ta, A_HEADS, HEAD_DIM), va32.reshape(1, n_out, ta, A_HEADS, HEAD_DIM),
                        kb32.reshape(1, n_out, tb, B_KV_HEADS, HEAD_DIM), vb32.reshape(1, n_out, tb, B_KV_HEADS, HEAD_DIM))

    hw = C_NOPE + C_ROPE
    w_in = w_in_c[0]
    w_kr = w_in[:, C_Q_LORA + C_KV_LORA:]
    win_ext = jnp.concatenate([w_in[:, :C_Q_LORA + C_KV_LORA], jnp.zeros((D, C_NOPE), F32),
                               w_kr, _rot_half_cols(w_kr)], axis=1).astype(BF)
    wq3 = w_qb[0].reshape(C_Q_LORA, C_HEADS, hw)
    wqb_ext = jnp.concatenate([wq3, _rot_half_cols(wq3[..., C_NOPE:])], axis=-1
                              ).reshape(C_Q_LORA, C_HEADS * C_HEAD_W).astype(BF)
    wkv3 = w_kvb[0].reshape(C_KV_LORA, C_HEADS, C_NOPE + C_V)
    wk_ext = jnp.concatenate([wkv3[..., :C_NOPE], jnp.zeros((C_KV_LORA, C_HEADS, C_HEAD_W - C_NOPE), F32)],
                             axis=-1).reshape(C_KV_LORA, C_HEADS * C_HEAD_W).astype(BF)
    wvt = jnp.concatenate([wkv3[..., C_NOPE:], jnp.zeros((C_KV_LORA, C_HEADS, C_VT_ROWS - C_V), F32)], axis=-1
                          ).reshape(C_KV_LORA, C_HEADS * C_VT_ROWS).T.astype(BF)
    w_oc = w_out_c[0].astype(BF)
    tables = {"p": _rope_tables(np.arange(S)), "s": _rope_tables(past + np.arange(TS))}

    mods = mods_all[1].reshape(n_cond_pad, 1, -1)
    state1 = {}
    for name, gr in groups.items():
        x, off, G, R = gr["x"], gr["blk_off"] // gr["G"], gr["G"], gr["R"]
        x = _ffn(x, mods, off, 0, norm_g[1, 0], wg[1, 0], wu[1, 0], wd[1, 0], G, R)
        q, kv32, kr32 = _proj_c(x, mods, off, norm_g[1, 1], win_ext, c_q_norm_g[0], c_kv_norm_g[0],
                                wqb_ext, tables[name], G, R, q_transposed=(name == "p"))
        if name == "p":
            k, vt = _kv_expand(kv32, kr32, wk_ext, wvt, ROW_TILE)
            o = _mla_attn(q, k, vt.reshape(nb, C_HEADS, C_VT_ROWS, S), Tq=MLA_T, Tk=MLA_T, SK=MLA_SK, QW=MLA_QW,
                          causal=True)
        else:
            n_keys = past + TS
            n_pad = -(-n_keys // LANES) * LANES
            kv_all = jnp.zeros((ns, n_pad, C_KV_LORA), F32).at[:, :past].set(cache_c_kv[0]).at[:, past:n_keys].set(kv32)
            kr_all = jnp.zeros((ns, n_pad, C_ROPE), F32).at[:, :past].set(cache_c_kr[0]).at[:, past:n_keys].set(kr32)
            k, vt = _kv_expand(kv_all, kr_all, wk_ext, wvt, n_pad)
            o = _mla_attn(jnp.swapaxes(q, 2, 3), k, vt.reshape(ns, C_HEADS, C_VT_ROWS, n_pad), Tq=TS, Tk=n_pad,
                          SK=LANES, QW=TS,
                          causal=False, valid_len=n_keys)
        x = _out_proj([o], [w_oc], x, mods, off, G, R)
        x = _ffn(x, mods, off, 6, norm_g[1, 2], wg[1, 1], wu[1, 1], wd[1, 1], G, R, final_g=final_norm_g)
        gr["x"] = x
        state1[name] = (kv32[None], kr32[None])

    return (groups["p"]["x"], groups["s"]["x"],
            *state0["p"], *state1["p"], *state0["s"], *state1["s"])
```

```python
import functools
import math

import numpy as np
import jax
import jax.numpy as jnp
from jax import lax
from jax.experimental import pallas as pl
from jax.experimental.pallas import tpu as pltpu

F32 = jnp.float32
BF = jnp.bfloat16

CHUNK = 64
HEAD_DIM = 64
A_HEADS = 8
A_PREV_CHUNKS = 8
A_REL_CLIP = 128
B_Q_HEADS = 8
B_KV_HEADS = 2
B_PREV_CHUNKS = 2
T5_BUCKETS = 32
T5_MAX_DIST = 128
C_HEADS = 16
C_Q_LORA = 384
C_KV_LORA = 256
C_NOPE = 64
C_ROPE = 32
C_V = 64
ROPE_BASE = 10000.0
EPS = 1e-6
NEG = -1e30
A_W = A_HEADS * HEAD_DIM
BQ_W = B_Q_HEADS * HEAD_DIM
BKV_W = B_KV_HEADS * HEAD_DIM

LANES = 128
VMEM_LIMIT = 56 * 1024 * 1024

ROW_TILE = 512
FF_CHUNK = 256
BAND_TQ = 256
BAND_AHEAD = 2
LOG2E = math.log2(math.e)
MLA_T = 1024
MLA_SK = 256
MLA_QW = 256
MLA_HU = 8
MLA_AHEAD = 5
C_VT_ROWS = 80
C_IN_EXT = 768
C_HEAD_W = 128


def _cparams(n_axes, vmem=VMEM_LIMIT):
    return pltpu.CompilerParams(dimension_semantics=("arbitrary",) * n_axes,
                                vmem_limit_bytes=vmem)


def _const_spec(shape):
    n = len(shape)
    return pl.BlockSpec(shape, lambda *_: (0,) * n)


def _rms(x):
    return x * lax.rsqrt(jnp.mean(x * x, axis=-1, keepdims=True) + EPS)


def _norm_mod(x, g, shift, scale):
    return (_rms(x) * g) * (1.0 + scale) + shift


def _mod_spec(G, blk_off, k, D):
    return pl.BlockSpec((G, 1, D), lambda i, j: (blk_off + i, 0, k))


def _adaln_body(c_ref, w_ref, b_ref, o_ref):
    c = c_ref[...]
    s = (c * jax.nn.sigmoid(c)).astype(BF)
    o_ref[0] = jnp.dot(s, w_ref[0].astype(BF), preferred_element_type=F32) + b_ref[0]


def _adaln(c_all, w_ada, b_ada):
    L, D, N = w_ada.shape
    R = c_all.shape[0]
    tn = 1024
    return pl.pallas_call(
        _adaln_body,
        grid=(L, N // tn),
        in_specs=[pl.BlockSpec((R, D), lambda l, j: (0, 0)),
                  pl.BlockSpec((1, D, tn), lambda l, j: (l, 0, j)),
                  pl.BlockSpec((1, 1, tn), lambda l, j: (l, 0, j))],
        out_specs=pl.BlockSpec((1, R, tn), lambda l, j: (l, 0, j)),
        out_shape=jax.ShapeDtypeStruct((L, R, N), F32),
        compiler_params=_cparams(2),
        name="adaln",
    )(c_all, w_ada, b_ada.reshape(L, 1, N))


def _ffn_body(x_ref, sh_ref, sc_ref, gt_ref, g_ref, wg_ref, wu_ref, wd_ref, *rest, final):
    if final:
        fg_ref, o_ref, a_ref = rest
    else:
        o_ref, a_ref = rest
    x = x_ref[...]
    G, R, D = x.shape
    F = wg_ref.shape[1]
    hb = _norm_mod(x, g_ref[...], sh_ref[...], sc_ref[...]).reshape(G * R, D).astype(BF)
    for c in range(F // FF_CHUNK):
        lo, hi = c * FF_CHUNK, (c + 1) * FF_CHUNK
        g = jnp.dot(hb, wg_ref[:, lo:hi], preferred_element_type=F32)
        u = jnp.dot(hb, wu_ref[:, lo:hi], preferred_element_type=F32)
        a_ref[:, lo:hi] = (g * jax.nn.sigmoid(g) * u).astype(BF)
    ff = jnp.dot(a_ref[...], wd_ref[...], preferred_element_type=F32)
    y = x + (0.5 * gt_ref[...]) * ff.reshape(G, R, D)
    if final:
        y = _rms(y) * fg_ref[...]
    o_ref[...] = y


def _ffn(x, mods, blk_off, kbase, g, wg, wu, wd, G, R, final_g=None):
    nseq, T, D = x.shape
    F = wg.shape[1]
    final = final_g is not None
    in_specs = [pl.BlockSpec((G, R, D), lambda i, j: (i, j, 0)),
                _mod_spec(G, blk_off, kbase, D), _mod_spec(G, blk_off, kbase + 1, D),
                _mod_spec(G, blk_off, kbase + 2, D),
                _const_spec((1, D)), _const_spec((D, F)), _const_spec((D, F)), _const_spec((F, D))]
    args = [x, mods, mods, mods, g.reshape(1, D), wg, wu, wd]
    if final:
        in_specs.append(_const_spec((1, D)))
        args.append(final_g.reshape(1, D))
    return pl.pallas_call(
        functools.partial(_ffn_body, final=final),
        grid=(nseq // G, T // R),
        in_specs=in_specs,
        out_specs=pl.BlockSpec((G, R, D), lambda i, j: (i, j, 0)),
        out_shape=jax.ShapeDtypeStruct((nseq, T, D), F32),
        scratch_shapes=[pltpu.VMEM((G * R, F), BF)],
        compiler_params=_cparams(2),
        name="ffn",
    )(*args)


def _dup_halves(k):
    rolled = pltpu.roll(k, HEAD_DIM, 1)
    lo = lax.broadcasted_iota(jnp.int32, k.shape, 1) < HEAD_DIM
    return jnp.concatenate([jnp.where(lo, k, rolled), jnp.where(lo, rolled, k)], axis=1)


def _proj_ab_body(x_ref, sh_ref, sc_ref, g_ref, w_ref,
                  qa_ref, ka_ref, va_ref, qb_ref, kb_ref, vb_ref,
                  ka32_ref, va32_ref, kb32_ref, vb32_ref, *, ta, tb):
    x = x_ref[...]
    G, R, D = x.shape
    hb = _norm_mod(x, g_ref[...], sh_ref[...], sc_ref[...]).reshape(G * R, D).astype(BF)
    res = jnp.dot(hb, w_ref[...], preferred_element_type=F32)
    o_ka, o_va, o_qb, o_kb, o_vb = A_W, 2 * A_W, 3 * A_W, 3 * A_W + BQ_W, 3 * A_W + BQ_W + BKV_W
    qscale = HEAD_DIM ** -0.5 * LOG2E
    qa_ref[...] = (res[:, :o_ka] * qscale).astype(BF).reshape(G, R, A_W)
    ka_ref[...] = res[:, o_ka:o_va].astype(BF).reshape(G, R, A_W)
    va_ref[...] = res[:, o_va:o_qb].astype(BF).reshape(G, R, A_W)
    qb_ref[...] = (res[:, o_qb:o_kb] * qscale).astype(BF).reshape(G, R, BQ_W)
    kb = res[:, o_kb:o_vb]
    vb = res[:, o_vb:o_vb + BKV_W]
    kb_ref[...] = _dup_halves(kb).astype(BF).reshape(G, R, 2 * BKV_W)
    vb_ref[...] = _dup_halves(vb).astype(BF).reshape(G, R, 2 * BKV_W)

    @pl.when(pl.program_id(1) == pl.num_programs(1) - 1)
    def _():
        ka32_ref[...] = res[:, o_ka:o_va].reshape(G, R, A_W)[:, R - ta:, :]
        va32_ref[...] = res[:, o_va:o_qb].reshape(G, R, A_W)[:, R - ta:, :]
        kb32_ref[...] = kb.reshape(G, R, BKV_W)[:, R - tb:, :]
        vb32_ref[...] = vb.reshape(G, R, BKV_W)[:, R - tb:, :]


def _proj_ab(x, mods, blk_off, g, w, G, R, ta, tb):
    nseq, T, D = x.shape
    row = lambda W: pl.BlockSpec((G, R, W), lambda i, j: (i, j, 0))
    tail = lambda t, W: pl.BlockSpec((G, t, W), lambda i, j: (i, 0, 0))
    bshape = lambda W: jax.ShapeDtypeStruct((nseq, T, W), BF)
    return pl.pallas_call(
        functools.partial(_proj_ab_body, ta=ta, tb=tb),
        grid=(nseq // G, T // R),
        in_specs=[row(D), _mod_spec(G, blk_off, 3, D), _mod_spec(G, blk_off, 4, D),
                  _const_spec((1, D)), _const_spec(w.shape)],
        out_specs=[row(A_W), row(A_W), row(A_W), row(BQ_W), row(2 * BKV_W), row(2 * BKV_W),
                   tail(ta, A_W), tail(ta, A_W), tail(tb, BKV_W), tail(tb, BKV_W)],
        out_shape=[bshape(A_W), bshape(A_W), bshape(A_W), bshape(BQ_W), bshape(2 * BKV_W), bshape(2 * BKV_W),
                   jax.ShapeDtypeStruct((nseq, ta, A_W), F32), jax.ShapeDtypeStruct((nseq, ta, A_W), F32),
                   jax.ShapeDtypeStruct((nseq, tb, BKV_W), F32), jax.ShapeDtypeStruct((nseq, tb, BKV_W), F32)],
        compiler_params=_cparams(2),
        name="proj_ab",
    )(x, mods, mods, g.reshape(1, D), w)


def _t5_bucket_np(rel):
    nb = T5_BUCKETS // 2
    ret = np.where(rel > 0, nb, 0)
    n = np.abs(rel)
    max_exact = nb // 2
    ratio = np.maximum(n, 1).astype(np.float32) / np.float32(max_exact)
    large = max_exact + (np.log(ratio).astype(np.float32) / np.float32(math.log(T5_MAX_DIST / max_exact))
                         * np.float32(nb - max_exact)).astype(np.int32)
    large = np.minimum(large, nb - 1)
    return ret + np.where(n < max_exact, n, large)


def _bias_body(thi_ref, tlo_ref, idx_ref, o_ref, *, Tq, span, prev_chunks, band_chunks):
    E = thi_ref.shape[1]
    L = idx_ref.shape[1]
    onehot = (lax.broadcasted_iota(jnp.int32, (E, L), 0) == idx_ref[...]).astype(BF)
    t = (jnp.dot(thi_ref[...], onehot, preferred_element_type=F32)
         + jnp.dot(tlo_ref[...], onehot, preferred_element_type=F32)) * LOG2E
    if band_chunks is not None:
        qc = lax.broadcasted_iota(jnp.int32, (Tq, span), 0) // CHUNK
        kc = lax.broadcasted_iota(jnp.int32, (Tq, span), 1) // CHUNK - prev_chunks
        valid = (kc <= qc) & (kc >= qc - band_chunks)
    for h in range(o_ref.shape[0]):
        x = jnp.broadcast_to(t[h:h + 1, :], (Tq, L))
        y = pltpu.roll(x, L - Tq + 1, 1, stride=1, stride_axis=0)[:, :span]
        if band_chunks is not None:
            y = jnp.where(valid, y, NEG)
        o_ref[h] = y


def _bias_tile(table, kind, Tq, span, n_prev_rows, band_chunks):
    H = table.shape[1]
    L = -(-(Tq + span - 1) // LANES) * LANES
    rel = np.arange(L) - (Tq - 1) - n_prev_rows
    if kind == "clip":
        idx = np.clip(rel, -A_REL_CLIP, A_REL_CLIP) + A_REL_CLIP
    else:
        idx = _t5_bucket_np(rel)
    E = -(-table.shape[0] // LANES) * LANES
    tt = jnp.zeros((H, E), F32).at[:, :table.shape[0]].set(table.T.astype(F32))
    thi = tt.astype(BF)
    tlo = (tt - thi.astype(F32)).astype(BF)
    return pl.pallas_call(
        functools.partial(_bias_body, Tq=Tq, span=span, prev_chunks=n_prev_rows // CHUNK,
                          band_chunks=band_chunks),
        out_shape=jax.ShapeDtypeStruct((H, Tq, span), F32),
        compiler_params=pltpu.CompilerParams(vmem_limit_bytes=VMEM_LIMIT),
        name="bias_tile",
    )(thi, tlo, jnp.asarray(idx.reshape(1, L), jnp.int32))


def _band_body(*refs, n_parts, n_prev_rows, pair_map, has_sink, Tq):
    q_ref = refs[0]
    k_refs = refs[1:1 + n_parts]
    v_refs = refs[1 + n_parts:1 + 2 * n_parts]
    bias_ref = refs[1 + 2 * n_parts]
    sink_ref = refs[2 + 2 * n_parts] if has_sink else None
    o_ref = refs[-1]
    i = pl.program_id(1)
    k = jnp.concatenate([r[0].astype(BF) for r in k_refs], axis=0)
    v = jnp.concatenate([r[0].astype(BF) for r in v_refs], axis=0)
    q = q_ref[0]
    span = k.shape[0]
    n_heads = 2 * (q.shape[1] // LANES)
    lane_lo = lax.broadcasted_iota(jnp.int32, (1, LANES), 1) < HEAD_DIM

    def sweep(mask_start):
        if mask_start:
            col_ok = lax.broadcasted_iota(jnp.int32, (1, span), 1) >= (n_prev_rows - i * Tq)

        def scores(h):
            p = h // 2
            qp = q[:, p * LANES:(p + 1) * LANES]
            qm = jnp.where(lane_lo if h % 2 == 0 else jnp.logical_not(lane_lo), qp, jnp.zeros_like(qp))
            kp = k[:, pair_map[p] * LANES:(pair_map[p] + 1) * LANES]
            s = lax.dot_general(qm, kp, (((1,), (1,)), ((), ())), preferred_element_type=F32) + bias_ref[h]
            if mask_start:
                s = jnp.where(col_ok, s, NEG)
            return s

        pending = [scores(h) for h in range(min(BAND_AHEAD, n_heads))]
        halves = []
        for h in range(n_heads):
            s = pending.pop(0)
            if h + BAND_AHEAD < n_heads:
                pending.append(scores(h + BAND_AHEAD))
            p = h // 2
            vp = v[:, pair_map[p] * LANES:(pair_map[p] + 1) * LANES]
            m = jnp.max(s, axis=-1, keepdims=True)
            if has_sink:
                sink = sink_ref[h] * LOG2E
                m = jnp.maximum(m, sink)
            e = jnp.exp2(s - m)
            den = jnp.sum(e, axis=-1, keepdims=True)
            if has_sink:
                den = den + jnp.exp2(sink - m)
            o = jnp.dot(e.astype(BF), vp, preferred_element_type=F32)
            halves.append(o * (1.0 / den))
            if h % 2 == 1:
                o_ref[0, :, p * LANES:(p + 1) * LANES] = jnp.where(lane_lo, halves[0], halves[1]).astype(BF)
                halves = []

    if n_prev_rows:
        pl.when(i * Tq < n_prev_rows)(lambda: sweep(True))
        pl.when(i * Tq >= n_prev_rows)(lambda: sweep(False))
    else:
        sweep(False)


def _band_attn(q, kparts, vparts, bias, sinks, *, Tq, n_prev_rows, pair_map):
    nseq, T, W = q.shape

    def part_spec(arr, rows, off):
        if off is None:
            return pl.BlockSpec((1, rows, arr.shape[2]), lambda b, i: (b, 0, 0))
        return pl.BlockSpec((1, rows, arr.shape[2]), lambda b, i: (b, jnp.maximum(i + off, 0), 0))

    in_specs = [pl.BlockSpec((1, Tq, W), lambda b, i: (b, i, 0))]
    in_specs += [part_spec(*p) for p in kparts] + [part_spec(*p) for p in vparts]
    in_specs.append(_const_spec(bias.shape))
    args = [q] + [p[0] for p in kparts] + [p[0] for p in vparts] + [bias]
    if sinks is not None:
        in_specs.append(pl.BlockSpec(memory_space=pltpu.SMEM))
        args.append(sinks.astype(F32))
    return pl.pallas_call(
        functools.partial(_band_body, n_parts=len(kparts), n_prev_rows=n_prev_rows,
                          pair_map=pair_map, has_sink=sinks is not None, Tq=Tq),
        grid=(nseq, T // Tq),
        in_specs=in_specs,
        out_specs=pl.BlockSpec((1, Tq, W), lambda b, i: (b, i, 0)),
        out_shape=jax.ShapeDtypeStruct((nseq, T, W), BF),
        compiler_params=_cparams(2),
        name="band_attn",
    )(*args)


def _out_proj_body(*refs, n_in):
    o_refs = refs[:n_in]
    w_refs = refs[n_in:2 * n_in]
    x_ref, gt_ref, out_ref = refs[2 * n_in:]
    x = x_ref[...]
    G, R, D = x.shape
    acc = None
    for o_ref, w_ref in zip(o_refs, w_refs):
        y = jnp.dot(o_ref[...].reshape(G * R, o_ref.shape[2]), w_ref[...], preferred_element_type=F32)
        acc = y if acc is None else acc + y
    out_ref[...] = x + gt_ref[...] * acc.reshape(G, R, D)


def _out_proj(os_, ws, x, mods, blk_off, G, R):
    nseq, T, D = x.shape
    row = lambda W: pl.BlockSpec((G, R, W), lambda i, j: (i, j, 0))
    return pl.pallas_call(
        functools.partial(_out_proj_body, n_in=len(os_)),
        grid=(nseq // G, T // R),
        in_specs=[row(o.shape[2]) for o in os_] + [_const_spec(w.shape) for w in ws]
                 + [row(D), _mod_spec(G, blk_off, 5, D)],
        out_specs=row(D),
        out_shape=jax.ShapeDtypeStruct((nseq, T, D), F32),
        compiler_params=_cparams(2),
        name="out_proj",
    )(*os_, *ws, x, mods)


def _proj_c_body(x_ref, sh_ref, sc_ref, g_ref, win_ref, qn_ref, kvn_ref, wqb_ref, cos_ref, sin_ref,
                 cost_ref, sint_ref, q_ref, kv32_ref, kr32_ref, *, q_transposed):
    x = x_ref[...]
    G, R, D = x.shape
    hb = _norm_mod(x, g_ref[...], sh_ref[...], sc_ref[...]).reshape(G * R, D).astype(BF)
    res = jnp.dot(hb, win_ref[...], preferred_element_type=F32)
    q_lat = res[:, :C_Q_LORA]
    kv_lat = res[:, C_Q_LORA:C_Q_LORA + C_KV_LORA]
    krg = res[:, C_Q_LORA + C_KV_LORA:]
    cos = cos_ref[...]
    sin = sin_ref[...]
    qn = (_rms(q_lat) * qn_ref[...]).astype(BF)
    scale = (C_NOPE + C_ROPE) ** -0.5 * math.log2(math.e)
    nh = q_ref.shape[1]
    if q_transposed:
        qt = lax.dot_general(wqb_ref[...], qn, (((1,), (1,)), ((), ())), preferred_element_type=F32)
        cost, sint = cost_ref[...], sint_ref[...]
        for h in range(nh):
            blk = qt[h * C_HEAD_W:(h + 1) * C_HEAD_W]
            roped = blk[C_NOPE:C_NOPE + C_ROPE] * cost + blk[C_NOPE + C_ROPE:] * sint
            q_ref[0, h] = (jnp.concatenate([blk[:C_NOPE], roped, jnp.zeros_like(roped)], axis=0) * scale).astype(BF)
    else:
        qr = jnp.dot(qn, wqb_ref[...], preferred_element_type=F32)
        qrot = pltpu.roll(qr, qr.shape[1] - C_ROPE, 1)
        for h in range(nh):
            a = qr[:, h * C_HEAD_W:(h + 1) * C_HEAD_W].reshape(G, R, C_HEAD_W)
            b = qrot[:, h * C_HEAD_W:(h + 1) * C_HEAD_W].reshape(G, R, C_HEAD_W)
            q_ref[:, h] = ((a * cos + b * sin) * scale).astype(BF)
    kv32_ref[...] = (_rms(kv_lat) * kvn_ref[...]).reshape(G, R, C_KV_LORA)
    krot = pltpu.roll(krg, C_HEAD_W - C_ROPE, 1)
    krf = krg.reshape(G, R, C_HEAD_W) * cos + krot.reshape(G, R, C_HEAD_W) * sin
    kr32_ref[...] = krf[:, :, C_NOPE:C_NOPE + C_ROPE]


def _proj_c(x, mods, blk_off, g, win, qn_g, kvn_g, wqb, tables, G, R, q_transposed):
    nseq, T, D = x.shape
    nh = C_HEADS
    cos_t, sin_t, cost_t, sint_t = tables
    tab = pl.BlockSpec((1, R, C_HEAD_W), lambda i, j: (0, j, 0))
    tabt = pl.BlockSpec((C_ROPE, R), lambda i, j: (0, j))
    if q_transposed:
        assert G == 1
        wqb = wqb.T
        q_spec = pl.BlockSpec((1, nh, C_HEAD_W, R), lambda i, j: (i, 0, 0, j))
        q_shape = jax.ShapeDtypeStruct((nseq, nh, C_HEAD_W, T), BF)
    else:
        q_spec = pl.BlockSpec((G, nh, R, C_HEAD_W), lambda i, j: (i, 0, j, 0))
        q_shape = jax.ShapeDtypeStruct((nseq, nh, T, C_HEAD_W), BF)
    return pl.pallas_call(
        functools.partial(_proj_c_body, q_transposed=q_transposed),
        grid=(nseq // G, T // R),
        in_specs=[pl.BlockSpec((G, R, D), lambda i, j: (i, j, 0)),
                  _mod_spec(G, blk_off, 3, D), _mod_spec(G, blk_off, 4, D), _const_spec((1, D)),
                  _const_spec(win.shape), _const_spec((1, C_Q_LORA)), _const_spec((1, C_KV_LORA)),
                  _const_spec(wqb.shape), tab, tab, tabt, tabt],
        out_specs=[q_spec,
                   pl.BlockSpec((G, R, C_KV_LORA), lambda i, j: (i, j, 0)),
                   pl.BlockSpec((G, R, C_ROPE), lambda i, j: (i, j, 0))],
        out_shape=[q_shape,
                   jax.ShapeDtypeStruct((nseq, T, C_KV_LORA), F32),
                   jax.ShapeDtypeStruct((nseq, T, C_ROPE), F32)],
        compiler_params=_cparams(2),
        name="proj_c",
    )(x, mods, mods, g.reshape(1, D), win, qn_g.reshape(1, -1), kvn_g.reshape(1, -1), wqb,
      cos_t, sin_t, cost_t, sint_t)


def _kvexp_body(kv_ref, kr_ref, wk_ref, wvt_ref, place_ref, ones_ref, k_ref, vt_ref):
    kv = kv_ref[0].astype(BF)
    krf = jnp.dot(kr_ref[0].astype(BF), place_ref[...], preferred_element_type=F32)
    kx = jnp.dot(kv, wk_ref[...], preferred_element_type=F32)
    for h in range(k_ref.shape[1]):
        k_ref[0, h] = (kx[:, h * C_HEAD_W:(h + 1) * C_HEAD_W] + krf).astype(BF)
    vt = lax.dot_general(wvt_ref[...], kv, (((1,), (1,)), ((), ())), preferred_element_type=F32)
    vt_ref[0] = (vt + ones_ref[...]).astype(BF)


def _kv_expand(kv, kr, wk, wvt, R):
    nseq, T, _ = kv.shape
    nh = wk.shape[1] // C_HEAD_W
    place = np.zeros((C_ROPE, C_HEAD_W), np.float32)
    place[np.arange(C_ROPE), C_NOPE + np.arange(C_ROPE)] = 1.0
    ones = np.zeros((nh, C_VT_ROWS, 1), np.float32)
    ones[:, C_V] = 1.0
    ones = ones.reshape(nh * C_VT_ROWS, 1)
    return pl.pallas_call(
        _kvexp_body,
        grid=(nseq, T // R),
        in_specs=[pl.BlockSpec((1, R, C_KV_LORA), lambda b, j: (b, j, 0)),
                  pl.BlockSpec((1, R, C_ROPE), lambda b, j: (b, j, 0)),
                  _const_spec(wk.shape), _const_spec(wvt.shape), _const_spec(place.shape),
                  _const_spec(ones.shape)],
        out_specs=[pl.BlockSpec((1, nh, R, C_HEAD_W), lambda b, j: (b, 0, j, 0)),
                   pl.BlockSpec((1, nh * C_VT_ROWS, R), lambda b, j: (b, 0, j))],
        out_shape=[jax.ShapeDtypeStruct((nseq, nh, T, C_HEAD_W), BF),
                   jax.ShapeDtypeStruct((nseq, nh * C_VT_ROWS, T), BF)],
        compiler_params=_cparams(2),
        name="kv_expand",
    )(kv, kr, wk, wvt, jnp.asarray(place, BF), jnp.asarray(ones))


def _flash_body(qi_ref, kj_ref, q_ref, k_ref, vt_ref, o_ref, acc_ref, m_ref,
                *, Tq, Tk, SK, QW, causal, valid_len):
    t = pl.program_id(1)
    i = qi_ref[t]
    j = kj_ref[t]
    nh = q_ref.shape[1]
    n_stripes = Tq // QW

    @pl.when(j == 0)
    def _():
        m_ref[...] = jnp.full(m_ref.shape, NEG, F32)
        acc_ref[...] = jnp.zeros(acc_ref.shape, F32)

    def sweep(diag):
        tiles = []
        n_sub = Tk // SK if valid_len is None else -(-valid_len // SK)
        for c in range(n_sub):
            for r in range(n_stripes):
                if diag:
                    if c * SK >= (r + 1) * QW:
                        continue
                    mask = "chunk" if (c + 1) * SK > r * QW + CHUNK else None
                else:
                    mask = "len" if valid_len is not None and (c + 1) * SK > valid_len else None
                tiles.append((c, r, mask))

        def head_group(g, carry):
            heads = [g * MLA_HU + u for u in range(MLA_HU)]
            stream = [(u, c, r, mask) for u in range(MLA_HU) for (c, r, mask) in tiles]
            state = {(u, r): (m_ref[heads[u], :, r * QW:(r + 1) * QW], acc_ref[heads[u], :, r * QW:(r + 1) * QW])
                     for u in range(MLA_HU) for r in range(n_stripes)}

            def scores(u, c, r, mask):
                s = jnp.dot(k_ref[0, heads[u], c * SK:(c + 1) * SK, :], q_ref[0, heads[u], :, r * QW:(r + 1) * QW],
                            preferred_element_type=F32)
                kpos = lax.broadcasted_iota(jnp.int32, (SK, 1), 0) + c * SK
                if mask == "chunk":
                    qpos = lax.broadcasted_iota(jnp.int32, (1, QW), 1) + r * QW
                    s = jnp.where(kpos // CHUNK <= qpos // CHUNK, s, NEG)
                elif mask == "len":
                    s = jnp.where(kpos < valid_len, s, NEG)
                return s

            pending = [scores(*tl) for tl in stream[:MLA_AHEAD]]
            for n, (u, c, r, _) in enumerate(stream):
                s = pending.pop(0)
                if n + MLA_AHEAD < len(stream):
                    pending.append(scores(*stream[n + MLA_AHEAD]))
                m, acc = state[u, r]
                m_new = jnp.maximum(m, jnp.max(s, axis=0, keepdims=True))
                p = jnp.exp2(s - m_new).astype(BF)
                acc = jnp.exp2(m - m_new) * acc + jnp.dot(vt_ref[0, heads[u], :, c * SK:(c + 1) * SK], p,
                                                          preferred_element_type=F32)
                state[u, r] = (m_new, acc)
            for (u, r), (m, acc) in state.items():
                m_ref[heads[u], :, r * QW:(r + 1) * QW] = m
                acc_ref[heads[u], :, r * QW:(r + 1) * QW] = acc
            return carry
        lax.fori_loop(0, nh // MLA_HU, head_group, 0)

    if causal:
        pl.when(j == i)(lambda: sweep(True))
        pl.when(j != i)(lambda: sweep(False))
        last = i
    else:
        sweep(False)
        last = 0

    @pl.when(j == last)
    def _():
        for p in range(nh // 2):
            halves = [acc_ref[h, :C_V, :] * (1.0 / acc_ref[h, C_V:C_V + 1, :]) for h in (2 * p, 2 * p + 1)]
            o_ref[0, :, p * LANES:(p + 1) * LANES] = jnp.concatenate(halves, axis=0).T.astype(BF)


def _mla_attn(q, k, vt, *, Tq, Tk, SK, QW, causal, valid_len=None):
    nseq, nh, _, T = q.shape
    S = k.shape[2]
    nq, nk = T // Tq, S // Tk
    assert Tq % QW == 0 and Tk % SK == 0 and SK % CHUNK == 0
    if causal:
        assert Tq == Tk and QW % CHUNK == 0 and valid_len is None
        pairs = [(i, j) for i in range(nq) for j in range(i + 1)]
    else:
        assert nk == 1
        pairs = [(i, 0) for i in range(nq)]
    qi = jnp.asarray([p[0] for p in pairs], jnp.int32)
    kj = jnp.asarray([p[1] for p in pairs], jnp.int32)
    grid_spec = pltpu.PrefetchScalarGridSpec(
        num_scalar_prefetch=2,
        grid=(nseq, len(pairs)),
        in_specs=[pl.BlockSpec((1, nh, C_HEAD_W, Tq), lambda b, t, qi, kj: (b, 0, 0, qi[t])),
                  pl.BlockSpec((1, nh, Tk, C_HEAD_W), lambda b, t, qi, kj: (b, 0, kj[t], 0)),
                  pl.BlockSpec((1, nh, C_VT_ROWS, Tk), lambda b, t, qi, kj: (b, 0, 0, kj[t]))],
        out_specs=pl.BlockSpec((1, Tq, nh * C_V), lambda b, t, qi, kj: (b, qi[t], 0)),
        scratch_shapes=[pltpu.VMEM((nh, C_VT_ROWS, Tq), F32), pltpu.VMEM((nh, 1, Tq), F32)],
    )
    return pl.pallas_call(
        functools.partial(_flash_body, Tq=Tq, Tk=Tk, SK=SK, QW=QW, causal=causal, valid_len=valid_len),
        grid_spec=grid_spec,
        out_shape=jax.ShapeDtypeStruct((nseq, T, nh * C_V), BF),
        compiler_params=_cparams(2),
        name="mla_attn",
    )(qi, kj, q, k, vt)


def _rope_tables(pos):
    half = C_ROPE // 2
    inv = (np.float32(ROPE_BASE) ** (-np.arange(half, dtype=np.float32) / np.float32(half))).astype(np.float32)
    ang = (pos.astype(np.float32)[:, None] * inv[None, :]).astype(np.float32).astype(np.float64)
    n = pos.shape[0]
    cos = np.zeros((1, n, C_HEAD_W), np.float32)
    sin = np.zeros((1, n, C_HEAD_W), np.float32)
    cos[0, :, :C_NOPE] = 1.0
    cos[0, :, C_NOPE:C_NOPE + C_ROPE] = np.concatenate([np.cos(ang), np.cos(ang)], axis=1)
    sin[0, :, C_NOPE:C_NOPE + C_ROPE] = np.concatenate([np.sin(ang), np.sin(ang)], axis=1)
    cos32, sin32 = cos[0, :, C_NOPE:C_NOPE + C_ROPE].T, sin[0, :, C_NOPE:C_NOPE + C_ROPE].T
    return jnp.asarray(cos), jnp.asarray(sin), jnp.asarray(cos32), jnp.asarray(sin32)


def _rot_half_cols(w):
    half = w.shape[-1] // 2
    return jnp.concatenate([-w[..., half:], w[..., :half]], axis=-1)


def kernel(x_prompt, x_sample, c_prompt, c_sample, cache_a_k, cache_a_v, cache_b_k, cache_b_v, cache_c_kv, cache_c_kr, w_ada, b_ada, norm_g, final_norm_g, ffn_w_gate, ffn_w_up, ffn_w_down, w_in_ab, w_out_ab, rel_bias_a, t5_bias, sinks_b, w_in_c, c_q_norm_g, c_kv_norm_g, w_qb, w_kvb, w_out_c):
    nb, S, D = x_prompt.shape
    ns, TS, _ = x_sample.shape
    la_c, lb_c = cache_a_k.shape[2], cache_b_k.shape[2]
    past = cache_c_kv.shape[2]
    la_p, lb_p = min(A_PREV_CHUNKS * CHUNK, S), min(B_PREV_CHUNKS * CHUNK, S)
    assert la_p <= ROW_TILE and S % MLA_T == 0 and S % ROW_TILE == 0 and TS % 8 == 0

    n_cond = ns + nb
    n_cond_pad = -(-n_cond // 8) * 8
    c_all = jnp.zeros((n_cond_pad, D), F32).at[:ns].set(c_sample).at[ns:n_cond].set(c_prompt)
    mods_all = _adaln(c_all, w_ada, b_ada)
    groups = {
        "p": dict(x=x_prompt, blk_off=ns, G=1, R=ROW_TILE),
        "s": dict(x=x_sample, blk_off=0, G=ns, R=TS),
    }

    wg, wu, wd = (w.astype(BF) for w in (ffn_w_gate, ffn_w_up, ffn_w_down))
    w_ab = w_in_ab[0].astype(BF)
    w_oa, w_ob = w_out_ab[0, :A_W].astype(BF), w_out_ab[0, A_W:].astype(BF)
    pair_a = tuple(range(A_HEADS // 2))
    pair_b = tuple(p // (B_Q_HEADS // B_KV_HEADS // 2) for p in range(B_Q_HEADS // 2))
    a_prev_rows = 2 * BAND_TQ
    b_prev_rows = BAND_TQ
    assert a_prev_rows == A_PREV_CHUNKS * CHUNK and b_prev_rows >= B_PREV_CHUNKS * CHUNK
    bias_ap = _bias_tile(rel_bias_a[0], "clip", BAND_TQ, a_prev_rows + BAND_TQ, a_prev_rows, A_PREV_CHUNKS)
    bias_bp = _bias_tile(t5_bias, "t5", BAND_TQ, b_prev_rows + BAND_TQ, b_prev_rows, B_PREV_CHUNKS)
    bias_as = _bias_tile(rel_bias_a[0], "clip", TS, la_c + TS, la_c, None)
    bias_bs = _bias_tile(t5_bias, "t5", TS, lb_c + TS, lb_c, None)
    dup = lambda c: jnp.repeat(c, 2, axis=2).reshape(c.shape[0], c.shape[1], 2 * BKV_W)
    cak = cache_a_k[0].reshape(ns, la_c, A_W)
    cav = cache_a_v[0].reshape(ns, la_c, A_W)
    cbk, cbv = dup(cache_b_k[0]), dup(cache_b_v[0])

    mods = mods_all[0].reshape(n_cond_pad, 1, -1)
    state0 = {}
    for name, gr in groups.items():
        x, off, G, R = gr["x"], gr["blk_off"] // gr["G"], gr["G"], gr["R"]
        x = _ffn(x, mods, off, 0, norm_g[0, 0], wg[0, 0], wu[0, 0], wd[0, 0], G, R)
        ta, tb = (la_p, lb_p) if name == "p" else (TS, TS)
        qa, ka, va, qb, kb, vb, ka32, va32, kb32, vb32 = _proj_ab(x, mods, off, norm_g[0, 1], w_ab, G, R, ta, tb)
        if name == "p":
            prev = lambda arr, n: [(arr, BAND_TQ, d - n) for d in range(n + 1)]
            oa = _band_attn(qa, prev(ka, 2), prev(va, 2), bias_ap, None,
                            Tq=BAND_TQ, n_prev_rows=a_prev_rows, pair_map=pair_a)
            ob = _band_attn(qb, prev(kb, 1), prev(vb, 1), bias_bp, sinks_b[0],
                            Tq=BAND_TQ, n_prev_rows=b_prev_rows, pair_map=pair_b)
        else:
            oa = _band_attn(qa, [(cak, la_c, None), (ka, TS, None)], [(cav, la_c, None), (va, TS, None)],
                            bias_as, None, Tq=TS, n_prev_rows=0, pair_map=pair_a)
            ob = _band_attn(qb, [(cbk, lb_c, None), (kb, TS, None)], [(cbv, lb_c, None), (vb, TS, None)],
                            bias_bs, sinks_b[0], Tq=TS, n_prev_rows=0, pair_map=pair_b)
        x = _out_proj([oa, ob], [w_oa, w_ob], x, mods, off, G, R)
        x = _ffn(x, mods, off, 6, norm_g[0, 2], wg[0, 1], wu[0, 1], wd[0, 1], G, R)
        gr["x"] = x
        n_out = x.shape[0]
        state0[name] = (ka32.reshape(1, n_out, ta, A_HEADS, HEAD_DIM), va32.reshape(1, n_out, ta, A_HEADS, HEAD_DIM),
                        kb32.reshape(1, n_out, tb, B_KV_HEADS, HEAD_DIM), vb32.reshape(1, n_out, tb, B_KV_HEADS, HEAD_DIM))

    hw = C_NOPE + C_ROPE
    w_in = w_in_c[0]
    w_kr = w_in[:, C_Q_LORA + C_KV_LORA:]
    win_ext = jnp.concatenate([w_in[:, :C_Q_LORA + C_KV_LORA], jnp.zeros((D, C_NOPE), F32),
                               w_kr, _rot_half_cols(w_kr)], axis=1).astype(BF)
    wq3 = w_qb[0].reshape(C_Q_LORA, C_HEADS, hw)
    wqb_ext = jnp.concatenate([wq3, _rot_half_cols(wq3[..., C_NOPE:])], axis=-1
                              ).reshape(C_Q_LORA, C_HEADS * C_HEAD_W).astype(BF)
    wkv3 = w_kvb[0].reshape(C_KV_LORA, C_HEADS, C_NOPE + C_V)
    wk_ext = jnp.concatenate([wkv3[..., :C_NOPE], jnp.zeros((C_KV_LORA, C_HEADS, C_HEAD_W - C_NOPE), F32)],
                             axis=-1).reshape(C_KV_LORA, C_HEADS * C_HEAD_W).astype(BF)
    wvt = jnp.concatenate([wkv3[..., C_NOPE:], jnp.zeros((C_KV_LORA, C_HEADS, C_VT_ROWS - C_V), F32)], axis=-1
                          ).reshape(C_KV_LORA, C_HEADS * C_VT_ROWS).T.astype(BF)
    w_oc = w_out_c[0].astype(BF)
    tables = {"p": _rope_tables(np.arange(S)), "s": _rope_tables(past + np.arange(TS))}

    mods = mods_all[1].reshape(n_cond_pad, 1, -1)
    state1 = {}
    for name, gr in groups.items():
        x, off, G, R = gr["x"], gr["blk_off"] // gr["G"], gr["G"], gr["R"]
        x = _ffn(x, mods, off, 0, norm_g[1, 0], wg[1, 0], wu[1, 0], wd[1, 0], G, R)
        q, kv32, kr32 = _proj_c(x, mods, off, norm_g[1, 1], win_ext, c_q_norm_g[0], c_kv_norm_g[0],
                                wqb_ext, tables[name], G, R, q_transposed=(name == "p"))
        if name == "p":
            k, vt = _kv_expand(kv32, kr32, wk_ext, wvt, ROW_TILE)
            o = _mla_attn(q, k, vt.reshape(nb, C_HEADS, C_VT_ROWS, S), Tq=MLA_T, Tk=MLA_T, SK=MLA_SK, QW=MLA_QW,
                          causal=True)
        else:
            n_keys = past + TS
            n_pad = -(-n_keys // LANES) * LANES
            kv_all = jnp.zeros((ns, n_pad, C_KV_LORA), F32).at[:, :past].set(cache_c_kv[0]).at[:, past:n_keys].set(kv32)
            kr_all = jnp.zeros((ns, n_pad, C_ROPE), F32).at[:, :past].set(cache_c_kr[0]).at[:, past:n_keys].set(kr32)
            k, vt = _kv_expand(kv_all, kr_all, wk_ext, wvt, n_pad)
            o = _mla_attn(jnp.swapaxes(q, 2, 3), k, vt.reshape(ns, C_HEADS, C_VT_ROWS, n_pad), Tq=TS, Tk=n_pad,
                          SK=LANES, QW=TS,
                          causal=False, valid_len=n_keys)
        x = _out_proj([o], [w_oc], x, mods, off, G, R)
        x = _ffn(x, mods, off, 6, norm_g[1, 2], wg[1, 1], wu[1, 1], wd[1, 1], G, R, final_g=final_norm_g)
        gr["x"] = x
        state1[name] = (kv32[None], kr32[None])

    return (groups["p"]["x"], groups["s"]["x"],
            *state0["p"], *state1["p"], *state0["s"], *state1["s"])
```

```python
import functools
import math

import numpy as np
import jax
import jax.numpy as jnp
from jax import lax
from jax.experimental import pallas as pl
from jax.experimental.pallas import tpu as pltpu

F32 = jnp.float32
BF = jnp.bfloat16

CHUNK = 64
HEAD_DIM = 64
A_HEADS = 8
A_PREV_CHUNKS = 8
A_REL_CLIP = 128
B_Q_HEADS = 8
B_KV_HEADS = 2
B_PREV_CHUNKS = 2
T5_BUCKETS = 32
T5_MAX_DIST = 128
C_HEADS = 16
C_Q_LORA = 384
C_KV_LORA = 256
C_NOPE = 64
C_ROPE = 32
C_V = 64
ROPE_BASE = 10000.0
EPS = 1e-6
NEG = -1e30
A_W = A_HEADS * HEAD_DIM
BQ_W = B_Q_HEADS * HEAD_DIM
BKV_W = B_KV_HEADS * HEAD_DIM

LANES = 128
VMEM_LIMIT = 56 * 1024 * 1024

ROW_TILE = 512
FF_CHUNK = 256
BAND_TQ = 256
BAND_AHEAD = 2
LOG2E = math.log2(math.e)
MLA_T = 1024
MLA_SK = 256
MLA_QW = 256
MLA_HU = 8
MLA_AHEAD = 5
C_VT_ROWS = 80
C_IN_EXT = 768
C_HEAD_W = 128


def _cparams(n_axes, vmem=VMEM_LIMIT):
    return pltpu.CompilerParams(dimension_semantics=("arbitrary",) * n_axes,
                                vmem_limit_bytes=vmem)


def _const_spec(shape):
    n = len(shape)
    return pl.BlockSpec(shape, lambda *_: (0,) * n)


def _rms(x):
    return x * lax.rsqrt(jnp.mean(x * x, axis=-1, keepdims=True) + EPS)


def _norm_mod(x, g, shift, scale):
    return (_rms(x) * g) * (1.0 + scale) + shift


def _mod_spec(G, blk_off, k, D):
    return pl.BlockSpec((G, 1, D), lambda i, j: (blk_off + i, 0, k))


def _adaln_body(c_ref, w_ref, b_ref, o_ref):
    c = c_ref[...]
    s = (c * jax.nn.sigmoid(c)).astype(BF)
    o_ref[0] = jnp.dot(s, w_ref[0].astype(BF), preferred_element_type=F32) + b_ref[0]


def _adaln(c_all, w_ada, b_ada):
    L, D, N = w_ada.shape
    R = c_all.shape[0]
    tn = 1024
    return pl.pallas_call(
        _adaln_body,
        grid=(L, N // tn),
        in_specs=[pl.BlockSpec((R, D), lambda l, j: (0, 0)),
                  pl.BlockSpec((1, D, tn), lambda l, j: (l, 0, j)),
                  pl.BlockSpec((1, 1, tn), lambda l, j: (l, 0, j))],
        out_specs=pl.BlockSpec((1, R, tn), lambda l, j: (l, 0, j)),
        out_shape=jax.ShapeDtypeStruct((L, R, N), F32),
        compiler_params=_cparams(2),
        name="adaln",
    )(c_all, w_ada, b_ada.reshape(L, 1, N))


def _ffn_body(*refs, n_attn, final):
    attn_refs, wo_refs = refs[:n_attn], refs[n_attn:2 * n_attn]
    refs = refs[2 * n_attn:]
    if n_attn:
        ga_ref, refs = refs[0], refs[1:]
    x_ref, sh_ref, sc_ref, gt_ref, g_ref, wg_ref, wu_ref, wd_ref = refs[:8]
    if final:
        fg_ref, o_ref, a_ref = refs[8:]
    else:
        o_ref, a_ref = refs[8:]
    x = x_ref[...]
    G, R, D = x.shape
    F = wg_ref.shape[1]
    if n_attn:
        mix = None
        for at_ref, wo_ref in zip(attn_refs, wo_refs):
            y = jnp.dot(at_ref[...].reshape(G * R, at_ref.shape[2]), wo_ref[...], preferred_element_type=F32)
            mix = y if mix is None else mix + y
        x = x + ga_ref[...] * mix.reshape(G, R, D)
    hb = _norm_mod(x, g_ref[...], sh_ref[...], sc_ref[...]).reshape(G * R, D).astype(BF)
    for c in range(F // FF_CHUNK):
        lo, hi = c * FF_CHUNK, (c + 1) * FF_CHUNK
        g = jnp.dot(hb, wg_ref[:, lo:hi], preferred_element_type=F32)
        u = jnp.dot(hb, wu_ref[:, lo:hi], preferred_element_type=F32)
        a_ref[:, lo:hi] = (g * jax.nn.sigmoid(g) * u).astype(BF)
    ff = jnp.dot(a_ref[...], wd_ref[...], preferred_element_type=F32)
    y = x + (0.5 * gt_ref[...]) * ff.reshape(G, R, D)
    if final:
        y = _rms(y) * fg_ref[...]
    o_ref[...] = y


def _ffn(x, mods, blk_off, kbase, g, weights, layer, which, G, R, final_g=None, attn=None):
    nseq, T, D = x.shape
    wg, wu, wd = weights
    F = wg.shape[-1]
    final = final_g is not None
    row = lambda W: pl.BlockSpec((G, R, W), lambda i, j: (i, j, 0))
    stacked = lambda w: pl.BlockSpec((None, None) + w.shape[2:], lambda i, j: (layer, which, 0, 0),
                                     pipeline_mode=pl.Buffered(1))
    in_specs, args = [], []
    n_attn = 0
    if attn is not None:
        os_, ws = attn
        n_attn = len(os_)
        in_specs += [row(o.shape[2]) for o in os_] + [_const_spec(w.shape) for w in ws]
        in_specs.append(_mod_spec(G, blk_off, 5, D))
        args += [*os_, *ws, mods]
    in_specs += [row(D), _mod_spec(G, blk_off, kbase, D), _mod_spec(G, blk_off, kbase + 1, D),
                 _mod_spec(G, blk_off, kbase + 2, D), _const_spec((1, D)), stacked(wg), stacked(wu), stacked(wd)]
    args += [x, mods, mods, mods, g.reshape(1, D), wg, wu, wd]
    if final:
        in_specs.append(_const_spec((1, D)))
        args.append(final_g.reshape(1, D))
    return pl.pallas_call(
        functools.partial(_ffn_body, n_attn=n_attn, final=final),
        grid=(nseq // G, T // R),
        in_specs=in_specs,
        out_specs=pl.BlockSpec((G, R, D), lambda i, j: (i, j, 0)),
        out_shape=jax.ShapeDtypeStruct((nseq, T, D), F32),
        scratch_shapes=[pltpu.VMEM((G * R, F), BF)],
        compiler_params=_cparams(2),
        name="ffn",
    )(*args)


def _dup_halves(k):
    rolled = pltpu.roll(k, HEAD_DIM, 1)
    lo = lax.broadcasted_iota(jnp.int32, k.shape, 1) < HEAD_DIM
    return jnp.concatenate([jnp.where(lo, k, rolled), jnp.where(lo, rolled, k)], axis=1)


def _proj_ab_body(x_ref, sh_ref, sc_ref, g_ref, w_ref,
                  qa_ref, ka_ref, va_ref, qb_ref, kb_ref, vb_ref,
                  ka32_ref, va32_ref, kb32_ref, vb32_ref, *, ta, tb):
    x = x_ref[...]
    G, R, D = x.shape
    hb = _norm_mod(x, g_ref[...], sh_ref[...], sc_ref[...]).reshape(G * R, D).astype(BF)
    res = jnp.dot(hb, w_ref[...], preferred_element_type=F32)
    o_ka, o_va, o_qb, o_kb, o_vb = A_W, 2 * A_W, 3 * A_W, 3 * A_W + BQ_W, 3 * A_W + BQ_W + BKV_W
    qscale = HEAD_DIM ** -0.5 * LOG2E
    qa_ref[...] = (res[:, :o_ka] * qscale).astype(BF).reshape(G, R, A_W)
    ka_ref[...] = res[:, o_ka:o_va].astype(BF).reshape(G, R, A_W)
    va_ref[...] = res[:, o_va:o_qb].astype(BF).reshape(G, R, A_W)
    qb_ref[...] = (res[:, o_qb:o_kb] * qscale).astype(BF).reshape(G, R, BQ_W)
    kb = res[:, o_kb:o_vb]
    vb = res[:, o_vb:o_vb + BKV_W]
    kb_ref[...] = _dup_halves(kb).astype(BF).reshape(G, R, 2 * BKV_W)
    vb_ref[...] = _dup_halves(vb).astype(BF).reshape(G, R, 2 * BKV_W)

    @pl.when(pl.program_id(1) == pl.num_programs(1) - 1)
    def _():
        ka32_ref[...] = res[:, o_ka:o_va].reshape(G, R, A_W)[:, R - ta:, :]
        va32_ref[...] = res[:, o_va:o_qb].reshape(G, R, A_W)[:, R - ta:, :]
        kb32_ref[...] = kb.reshape(G, R, BKV_W)[:, R - tb:, :]
        vb32_ref[...] = vb.reshape(G, R, BKV_W)[:, R - tb:, :]


def _proj_ab(x, mods, blk_off, g, w, G, R, ta, tb):
    nseq, T, D = x.shape
    row = lambda W: pl.BlockSpec((G, R, W), lambda i, j: (i, j, 0))
    tail = lambda t, W: pl.BlockSpec((G, t, W), lambda i, j: (i, 0, 0))
    bshape = lambda W: jax.ShapeDtypeStruct((nseq, T, W), BF)
    return pl.pallas_call(
        functools.partial(_proj_ab_body, ta=ta, tb=tb),
        grid=(nseq // G, T // R),
        in_specs=[row(D), _mod_spec(G, blk_off, 3, D), _mod_spec(G, blk_off, 4, D),
                  _const_spec((1, D)), _const_spec(w.shape)],
        out_specs=[row(A_W), row(A_W), row(A_W), row(BQ_W), row(2 * BKV_W), row(2 * BKV_W),
                   tail(ta, A_W), tail(ta, A_W), tail(tb, BKV_W), tail(tb, BKV_W)],
        out_shape=[bshape(A_W), bshape(A_W), bshape(A_W), bshape(BQ_W), bshape(2 * BKV_W), bshape(2 * BKV_W),
                   jax.ShapeDtypeStruct((nseq, ta, A_W), F32), jax.ShapeDtypeStruct((nseq, ta, A_W), F32),
                   jax.ShapeDtypeStruct((nseq, tb, BKV_W), F32), jax.ShapeDtypeStruct((nseq, tb, BKV_W), F32)],
        compiler_params=_cparams(2),
        name="proj_ab",
    )(x, mods, mods, g.reshape(1, D), w)


def _t5_bucket_np(rel):
    nb = T5_BUCKETS // 2
    ret = np.where(rel > 0, nb, 0)
    n = np.abs(rel)
    max_exact = nb // 2
    ratio = np.maximum(n, 1).astype(np.float32) / np.float32(max_exact)
    large = max_exact + (np.log(ratio).astype(np.float32) / np.float32(math.log(T5_MAX_DIST / max_exact))
                         * np.float32(nb - max_exact)).astype(np.int32)
    large = np.minimum(large, nb - 1)
    return ret + np.where(n < max_exact, n, large)


def _bias_body(thi_ref, tlo_ref, idx_ref, o_ref, *, Tq, span, prev_chunks, band_chunks):
    E = thi_ref.shape[1]
    L = idx_ref.shape[1]
    onehot = (lax.broadcasted_iota(jnp.int32, (E, L), 0) == idx_ref[...]).astype(BF)
    t = (jnp.dot(thi_ref[...], onehot, preferred_element_type=F32)
         + jnp.dot(tlo_ref[...], onehot, preferred_element_type=F32)) * LOG2E
    if band_chunks is not None:
        qc = lax.broadcasted_iota(jnp.int32, (Tq, span), 0) // CHUNK
        kc = lax.broadcasted_iota(jnp.int32, (Tq, span), 1) // CHUNK - prev_chunks
        valid = (kc <= qc) & (kc >= qc - band_chunks)
    for h in range(o_ref.shape[0]):
        x = jnp.broadcast_to(t[h:h + 1, :], (Tq, L))
        y = pltpu.roll(x, L - Tq + 1, 1, stride=1, stride_axis=0)[:, :span]
        if band_chunks is not None:
            y = jnp.where(valid, y, NEG)
        o_ref[h] = y


def _bias_tile(table, kind, Tq, span, n_prev_rows, band_chunks):
    H = table.shape[1]
    L = -(-(Tq + span - 1) // LANES) * LANES
    rel = np.arange(L) - (Tq - 1) - n_prev_rows
    if kind == "clip":
        idx = np.clip(rel, -A_REL_CLIP, A_REL_CLIP) + A_REL_CLIP
    else:
        idx = _t5_bucket_np(rel)
    E = -(-table.shape[0] // LANES) * LANES
    tt = jnp.zeros((H, E), F32).at[:, :table.shape[0]].set(table.T.astype(F32))
    thi = tt.astype(BF)
    tlo = (tt - thi.astype(F32)).astype(BF)
    return pl.pallas_call(
        functools.partial(_bias_body, Tq=Tq, span=span, prev_chunks=n_prev_rows // CHUNK,
                          band_chunks=band_chunks),
        out_shape=jax.ShapeDtypeStruct((H, Tq, span), F32),
        compiler_params=pltpu.CompilerParams(vmem_limit_bytes=VMEM_LIMIT),
        name="bias_tile",
    )(thi, tlo, jnp.asarray(idx.reshape(1, L), jnp.int32))


def _band_body(*refs, n_parts, n_prev_rows, pair_map, has_sink, Tq):
    q_ref = refs[0]
    k_refs = refs[1:1 + n_parts]
    v_refs = refs[1 + n_parts:1 + 2 * n_parts]
    bias_ref = refs[1 + 2 * n_parts]
    sink_ref = refs[2 + 2 * n_parts] if has_sink else None
    o_ref = refs[-1]
    i = pl.program_id(1)
    k = jnp.concatenate([r[0].astype(BF) for r in k_refs], axis=0)
    v = jnp.concatenate([r[0].astype(BF) for r in v_refs], axis=0)
    q = q_ref[0]
    span = k.shape[0]
    n_heads = 2 * (q.shape[1] // LANES)
    lane_lo = lax.broadcasted_iota(jnp.int32, (1, LANES), 1) < HEAD_DIM

    def sweep(mask_start):
        if mask_start:
            col_ok = lax.broadcasted_iota(jnp.int32, (1, span), 1) >= (n_prev_rows - i * Tq)

        def scores(h):
            p = h // 2
            qp = q[:, p * LANES:(p + 1) * LANES]
            qm = jnp.where(lane_lo if h % 2 == 0 else jnp.logical_not(lane_lo), qp, jnp.zeros_like(qp))
            kp = k[:, pair_map[p] * LANES:(pair_map[p] + 1) * LANES]
            s = lax.dot_general(qm, kp, (((1,), (1,)), ((), ())), preferred_element_type=F32) + bias_ref[h]
            if mask_start:
                s = jnp.where(col_ok, s, NEG)
            return s

        pending = [scores(h) for h in range(min(BAND_AHEAD, n_heads))]
        halves = []
        for h in range(n_heads):
            s = pending.pop(0)
            if h + BAND_AHEAD < n_heads:
                pending.append(scores(h + BAND_AHEAD))
            p = h // 2
            vp = v[:, pair_map[p] * LANES:(pair_map[p] + 1) * LANES]
            m = jnp.max(s, axis=-1, keepdims=True)
            if has_sink:
                sink = sink_ref[h] * LOG2E
                m = jnp.maximum(m, sink)
            e = jnp.exp2(s - m)
            den = jnp.sum(e, axis=-1, keepdims=True)
            if has_sink:
                den = den + jnp.exp2(sink - m)
            o = jnp.dot(e.astype(BF), vp, preferred_element_type=F32)
            halves.append(o * (1.0 / den))
            if h % 2 == 1:
                o_ref[0, :, p * LANES:(p + 1) * LANES] = jnp.where(lane_lo, halves[0], halves[1]).astype(BF)
                halves = []

    if n_prev_rows:
        pl.when(i * Tq < n_prev_rows)(lambda: sweep(True))
        pl.when(i * Tq >= n_prev_rows)(lambda: sweep(False))
    else:
        sweep(False)


def _band_attn(q, kparts, vparts, bias, sinks, *, Tq, n_prev_rows, pair_map):
    nseq, T, W = q.shape

    def part_spec(arr, rows, off):
        if off is None:
            return pl.BlockSpec((1, rows, arr.shape[2]), lambda b, i: (b, 0, 0))
        return pl.BlockSpec((1, rows, arr.shape[2]), lambda b, i: (b, jnp.maximum(i + off, 0), 0))

    in_specs = [pl.BlockSpec((1, Tq, W), lambda b, i: (b, i, 0))]
    in_specs += [part_spec(*p) for p in kparts] + [part_spec(*p) for p in vparts]
    in_specs.append(_const_spec(bias.shape))
    args = [q] + [p[0] for p in kparts] + [p[0] for p in vparts] + [bias]
    if sinks is not None:
        in_specs.append(pl.BlockSpec(memory_space=pltpu.SMEM))
        args.append(sinks.astype(F32))
    return pl.pallas_call(
        functools.partial(_band_body, n_parts=len(kparts), n_prev_rows=n_prev_rows,
                          pair_map=pair_map, has_sink=sinks is not None, Tq=Tq),
        grid=(nseq, T // Tq),
        in_specs=in_specs,
        out_specs=pl.BlockSpec((1, Tq, W), lambda b, i: (b, i, 0)),
        out_shape=jax.ShapeDtypeStruct((nseq, T, W), BF),
        compiler_params=_cparams(2),
        name="band_attn",
    )(*args)


def _out_proj_body(*refs, n_in):
    o_refs = refs[:n_in]
    w_refs = refs[n_in:2 * n_in]
    x_ref, gt_ref, out_ref = refs[2 * n_in:]
    x = x_ref[...]
    G, R, D = x.shape
    acc = None
    for o_ref, w_ref in zip(o_refs, w_refs):
        y = jnp.dot(o_ref[...].reshape(G * R, o_ref.shape[2]), w_ref[...], preferred_element_type=F32)
        acc = y if acc is None else acc + y
    out_ref[...] = x + gt_ref[...] * acc.reshape(G, R, D)


def _out_proj(os_, ws, x, mods, blk_off, G, R):
    nseq, T, D = x.shape
    row = lambda W: pl.BlockSpec((G, R, W), lambda i, j: (i, j, 0))
    return pl.pallas_call(
        functools.partial(_out_proj_body, n_in=len(os_)),
        grid=(nseq // G, T // R),
        in_specs=[row(o.shape[2]) for o in os_] + [_const_spec(w.shape) for w in ws]
                 + [row(D), _mod_spec(G, blk_off, 5, D)],
        out_specs=row(D),
        out_shape=jax.ShapeDtypeStruct((nseq, T, D), F32),
        compiler_params=_cparams(2),
        name="out_proj",
    )(*os_, *ws, x, mods)


def _proj_c_body(x_ref, sh_ref, sc_ref, g_ref, win_ref, qn_ref, kvn_ref, wqb_ref, cos_ref, sin_ref,
                 cost_ref, sint_ref, q_ref, kv32_ref, kr32_ref, *, q_transposed):
    x = x_ref[...]
    G, R, D = x.shape
    hb = _norm_mod(x, g_ref[...], sh_ref[...], sc_ref[...]).reshape(G * R, D).astype(BF)
    res = jnp.dot(hb, win_ref[...], preferred_element_type=F32)
    q_lat = res[:, :C_Q_LORA]
    kv_lat = res[:, C_Q_LORA:C_Q_LORA + C_KV_LORA]
    krg = res[:, C_Q_LORA + C_KV_LORA:]
    cos = cos_ref[...]
    sin = sin_ref[...]
    qn = (_rms(q_lat) * qn_ref[...]).astype(BF)
    scale = (C_NOPE + C_ROPE) ** -0.5 * math.log2(math.e)
    nh = q_ref.shape[1]
    if q_transposed:
        qt = lax.dot_general(wqb_ref[...], qn, (((1,), (1,)), ((), ())), preferred_element_type=F32)
        cost, sint = cost_ref[...], sint_ref[...]
        for h in range(nh):
            blk = qt[h * C_HEAD_W:(h + 1) * C_HEAD_W]
            roped = blk[C_NOPE:C_NOPE + C_ROPE] * cost + blk[C_NOPE + C_ROPE:] * sint
            q_ref[0, h] = (jnp.concatenate([blk[:C_NOPE], roped, jnp.zeros_like(roped)], axis=0) * scale).astype(BF)
    else:
        qr = jnp.dot(qn, wqb_ref[...], preferred_element_type=F32)
        qrot = pltpu.roll(qr, qr.shape[1] - C_ROPE, 1)
        for h in range(nh):
            a = qr[:, h * C_HEAD_W:(h + 1) * C_HEAD_W].reshape(G, R, C_HEAD_W)
            b = qrot[:, h * C_HEAD_W:(h + 1) * C_HEAD_W].reshape(G, R, C_HEAD_W)
            q_ref[:, h] = ((a * cos + b * sin) * scale).astype(BF)
    kv32_ref[...] = (_rms(kv_lat) * kvn_ref[...]).reshape(G, R, C_KV_LORA)
    krot = pltpu.roll(krg, C_HEAD_W - C_ROPE, 1)
    krf = krg.reshape(G, R, C_HEAD_W) * cos + krot.reshape(G, R, C_HEAD_W) * sin
    kr32_ref[...] = krf[:, :, C_NOPE:C_NOPE + C_ROPE]


def _proj_c(x, mods, blk_off, g, win, qn_g, kvn_g, wqb, tables, G, R, q_transposed):
    nseq, T, D = x.shape
    nh = C_HEADS
    cos_t, sin_t, cost_t, sint_t = tables
    tab = pl.BlockSpec((1, R, C_HEAD_W), lambda i, j: (0, j, 0))
    tabt = pl.BlockSpec((C_ROPE, R), lambda i, j: (0, j))
    if q_transposed:
        assert G == 1
        wqb = wqb.T
        q_spec = pl.BlockSpec((1, nh, C_HEAD_W, R), lambda i, j: (i, 0, 0, j))
        q_shape = jax.ShapeDtypeStruct((nseq, nh, C_HEAD_W, T), BF)
    else:
        q_spec = pl.BlockSpec((G, nh, R, C_HEAD_W), lambda i, j: (i, 0, j, 0))
        q_shape = jax.ShapeDtypeStruct((nseq, nh, T, C_HEAD_W), BF)
    return pl.pallas_call(
        functools.partial(_proj_c_body, q_transposed=q_transposed),
        grid=(nseq // G, T // R),
        in_specs=[pl.BlockSpec((G, R, D), lambda i, j: (i, j, 0)),
                  _mod_spec(G, blk_off, 3, D), _mod_spec(G, blk_off, 4, D), _const_spec((1, D)),
                  _const_spec(win.shape), _const_spec((1, C_Q_LORA)), _const_spec((1, C_KV_LORA)),
                  _const_spec(wqb.shape), tab, tab, tabt, tabt],
        out_specs=[q_spec,
                   pl.BlockSpec((G, R, C_KV_LORA), lambda i, j: (i, j, 0)),
                   pl.BlockSpec((G, R, C_ROPE), lambda i, j: (i, j, 0))],
        out_shape=[q_shape,
                   jax.ShapeDtypeStruct((nseq, T, C_KV_LORA), F32),
                   jax.ShapeDtypeStruct((nseq, T, C_ROPE), F32)],
        compiler_params=_cparams(2),
        name="proj_c",
    )(x, mods, mods, g.reshape(1, D), win, qn_g.reshape(1, -1), kvn_g.reshape(1, -1), wqb,
      cos_t, sin_t, cost_t, sint_t)


def _kvexp_body(kv_ref, kr_ref, wk_ref, wvt_ref, place_ref, ones_ref, k_ref, vt_ref):
    kv = kv_ref[0].astype(BF)
    krf = jnp.dot(kr_ref[0].astype(BF), place_ref[...], preferred_element_type=F32)
    kx = jnp.dot(kv, wk_ref[...], preferred_element_type=F32)
    for h in range(k_ref.shape[1]):
        k_ref[0, h] = (kx[:, h * C_HEAD_W:(h + 1) * C_HEAD_W] + krf).astype(BF)
    vt = lax.dot_general(wvt_ref[...], kv, (((1,), (1,)), ((), ())), preferred_element_type=F32)
    vt_ref[0] = (vt + ones_ref[...]).astype(BF)


def _kv_expand(kv, kr, wk, wvt, R):
    nseq, T, _ = kv.shape
    nh = wk.shape[1] // C_HEAD_W
    place = np.zeros((C_ROPE, C_HEAD_W), np.float32)
    place[np.arange(C_ROPE), C_NOPE + np.arange(C_ROPE)] = 1.0
    ones = np.zeros((nh, C_VT_ROWS, 1), np.float32)
    ones[:, C_V] = 1.0
    ones = ones.reshape(nh * C_VT_ROWS, 1)
    return pl.pallas_call(
        _kvexp_body,
        grid=(nseq, T // R),
        in_specs=[pl.BlockSpec((1, R, C_KV_LORA), lambda b, j: (b, j, 0)),
                  pl.BlockSpec((1, R, C_ROPE), lambda b, j: (b, j, 0)),
                  _const_spec(wk.shape), _const_spec(wvt.shape), _const_spec(place.shape),
                  _const_spec(ones.shape)],
        out_specs=[pl.BlockSpec((1, nh, R, C_HEAD_W), lambda b, j: (b, 0, j, 0)),
                   pl.BlockSpec((1, nh * C_VT_ROWS, R), lambda b, j: (b, 0, j))],
        out_shape=[jax.ShapeDtypeStruct((nseq, nh, T, C_HEAD_W), BF),
                   jax.ShapeDtypeStruct((nseq, nh * C_VT_ROWS, T), BF)],
        compiler_params=_cparams(2),
        name="kv_expand",
    )(kv, kr, wk, wvt, jnp.asarray(place, BF), jnp.asarray(ones))


def _flash_body(qi_ref, kj_ref, q_ref, k_ref, vt_ref, o_ref, acc_ref, m_ref,
                *, Tq, Tk, SK, QW, causal, valid_len):
    t = pl.program_id(1)
    i = qi_ref[t]
    j = kj_ref[t]
    nh = q_ref.shape[1]
    n_stripes = Tq // QW

    @pl.when(j == 0)
    def _():
        m_ref[...] = jnp.full(m_ref.shape, NEG, F32)
        acc_ref[...] = jnp.zeros(acc_ref.shape, F32)

    def sweep(diag):
        tiles = []
        n_sub = Tk // SK if valid_len is None else -(-valid_len // SK)
        for c in range(n_sub):
            for r in range(n_stripes):
                if diag:
                    if c * SK >= (r + 1) * QW:
                        continue
                    mask = "chunk" if (c + 1) * SK > r * QW + CHUNK else None
                else:
                    mask = "len" if valid_len is not None and (c + 1) * SK > valid_len else None
                tiles.append((c, r, mask))

        def head_group(g, carry):
            heads = [g * MLA_HU + u for u in range(MLA_HU)]
            stream = [(u, c, r, mask) for u in range(MLA_HU) for (c, r, mask) in tiles]
            state = {(u, r): (m_ref[heads[u], :, r * QW:(r + 1) * QW], acc_ref[heads[u], :, r * QW:(r + 1) * QW])
                     for u in range(MLA_HU) for r in range(n_stripes)}

            def scores(u, c, r, mask):
                s = jnp.dot(k_ref[0, heads[u], c * SK:(c + 1) * SK, :], q_ref[0, heads[u], :, r * QW:(r + 1) * QW],
                            preferred_element_type=F32)
                kpos = lax.broadcasted_iota(jnp.int32, (SK, 1), 0) + c * SK
                if mask == "chunk":
                    qpos = lax.broadcasted_iota(jnp.int32, (1, QW), 1) + r * QW
                    s = jnp.where(kpos // CHUNK <= qpos // CHUNK, s, NEG)
                elif mask == "len":
                    s = jnp.where(kpos < valid_len, s, NEG)
                return s

            pending = [scores(*tl) for tl in stream[:MLA_AHEAD]]
            for n, (u, c, r, _) in enumerate(stream):
                s = pending.pop(0)
                if n + MLA_AHEAD < len(stream):
                    pending.append(scores(*stream[n + MLA_AHEAD]))
                m, acc = state[u, r]
                m_new = jnp.maximum(m, jnp.max(s, axis=0, keepdims=True))
                p = jnp.exp2(s - m_new).astype(BF)
                acc = jnp.exp2(m - m_new) * acc + jnp.dot(vt_ref[0, heads[u], :, c * SK:(c + 1) * SK], p,
                                                          preferred_element_type=F32)
                state[u, r] = (m_new, acc)
            for (u, r), (m, acc) in state.items():
                m_ref[heads[u], :, r * QW:(r + 1) * QW] = m
                acc_ref[heads[u], :, r * QW:(r + 1) * QW] = acc
            return carry
        lax.fori_loop(0, nh // MLA_HU, head_group, 0)

    if causal:
        pl.when(j == i)(lambda: sweep(True))
        pl.when(j != i)(lambda: sweep(False))
        last = i
    else:
        sweep(False)
        last = 0

    @pl.when(j == last)
    def _():
        for p in range(nh // 2):
            halves = [acc_ref[h, :C_V, :] * (1.0 / acc_ref[h, C_V:C_V + 1, :]) for h in (2 * p, 2 * p + 1)]
            o_ref[0, :, p * LANES:(p + 1) * LANES] = jnp.concatenate(halves, axis=0).T.astype(BF)


def _mla_attn(q, k, vt, *, Tq, Tk, SK, QW, causal, valid_len=None):
    nseq, nh, _, T = q.shape
    S = k.shape[2]
    nq, nk = T // Tq, S // Tk
    assert Tq % QW == 0 and Tk % SK == 0 and SK % CHUNK == 0
    if causal:
        assert Tq == Tk and QW % CHUNK == 0 and valid_len is None
        pairs = [(i, j) for i in range(nq) for j in range(i + 1)]
    else:
        assert nk == 1
        pairs = [(i, 0) for i in range(nq)]
    qi = jnp.asarray([p[0] for p in pairs], jnp.int32)
    kj = jnp.asarray([p[1] for p in pairs], jnp.int32)
    grid_spec = pltpu.PrefetchScalarGridSpec(
        num_scalar_prefetch=2,
        grid=(nseq, len(pairs)),
        in_specs=[pl.BlockSpec((1, nh, C_HEAD_W, Tq), lambda b, t, qi, kj: (b, 0, 0, qi[t])),
                  pl.BlockSpec((1, nh, Tk, C_HEAD_W), lambda b, t, qi, kj: (b, 0, kj[t], 0)),
                  pl.BlockSpec((1, nh, C_VT_ROWS, Tk), lambda b, t, qi, kj: (b, 0, 0, kj[t]))],
        out_specs=pl.BlockSpec((1, Tq, nh * C_V), lambda b, t, qi, kj: (b, qi[t], 0)),
        scratch_shapes=[pltpu.VMEM((nh, C_VT_ROWS, Tq), F32), pltpu.VMEM((nh, 1, Tq), F32)],
    )
    return pl.pallas_call(
        functools.partial(_flash_body, Tq=Tq, Tk=Tk, SK=SK, QW=QW, causal=causal, valid_len=valid_len),
        grid_spec=grid_spec,
        out_shape=jax.ShapeDtypeStruct((nseq, T, nh * C_V), BF),
        compiler_params=_cparams(2),
        name="mla_attn",
    )(qi, kj, q, k, vt)


def _rope_tables(pos):
    half = C_ROPE // 2
    inv = (np.float32(ROPE_BASE) ** (-np.arange(half, dtype=np.float32) / np.float32(half))).astype(np.float32)
    ang = (pos.astype(np.float32)[:, None] * inv[None, :]).astype(np.float32).astype(np.float64)
    n = pos.shape[0]
    cos = np.zeros((1, n, C_HEAD_W), np.float32)
    sin = np.zeros((1, n, C_HEAD_W), np.float32)
    cos[0, :, :C_NOPE] = 1.0
    cos[0, :, C_NOPE:C_NOPE + C_ROPE] = np.concatenate([np.cos(ang), np.cos(ang)], axis=1)
    sin[0, :, C_NOPE:C_NOPE + C_ROPE] = np.concatenate([np.sin(ang), np.sin(ang)], axis=1)
    cos32, sin32 = cos[0, :, C_NOPE:C_NOPE + C_ROPE].T, sin[0, :, C_NOPE:C_NOPE + C_ROPE].T
    return jnp.asarray(cos), jnp.asarray(sin), jnp.asarray(cos32), jnp.asarray(sin32)


def _rot_half_cols(w):
    half = w.shape[-1] // 2
    return jnp.concatenate([-w[..., half:], w[..., :half]], axis=-1)


def kernel(x_prompt, x_sample, c_prompt, c_sample, cache_a_k, cache_a_v, cache_b_k, cache_b_v, cache_c_kv, cache_c_kr, w_ada, b_ada, norm_g, final_norm_g, ffn_w_gate, ffn_w_up, ffn_w_down, w_in_ab, w_out_ab, rel_bias_a, t5_bias, sinks_b, w_in_c, c_q_norm_g, c_kv_norm_g, w_qb, w_kvb, w_out_c):
    nb, S, D = x_prompt.shape
    ns, TS, _ = x_sample.shape
    la_c, lb_c = cache_a_k.shape[2], cache_b_k.shape[2]
    past = cache_c_kv.shape[2]
    la_p, lb_p = min(A_PREV_CHUNKS * CHUNK, S), min(B_PREV_CHUNKS * CHUNK, S)
    assert la_p <= ROW_TILE and S % MLA_T == 0 and S % ROW_TILE == 0 and TS % 8 == 0

    n_cond = ns + nb
    n_cond_pad = -(-n_cond // 8) * 8
    c_all = jnp.zeros((n_cond_pad, D), F32).at[:ns].set(c_sample).at[ns:n_cond].set(c_prompt)
    mods_all = _adaln(c_all, w_ada, b_ada)
    groups = {
        "p": dict(x=x_prompt, blk_off=ns, G=1, R=ROW_TILE),
        "s": dict(x=x_sample, blk_off=0, G=ns, R=TS),
    }

    ffn_w = tuple(w.astype(BF) for w in (ffn_w_gate, ffn_w_up, ffn_w_down))
    w_ab = w_in_ab[0].astype(BF)
    w_oa, w_ob = w_out_ab[0, :A_W].astype(BF), w_out_ab[0, A_W:].astype(BF)
    pair_a = tuple(range(A_HEADS // 2))
    pair_b = tuple(p // (B_Q_HEADS // B_KV_HEADS // 2) for p in range(B_Q_HEADS // 2))
    a_prev_rows = 2 * BAND_TQ
    b_prev_rows = BAND_TQ
    assert a_prev_rows == A_PREV_CHUNKS * CHUNK and b_prev_rows >= B_PREV_CHUNKS * CHUNK
    bias_ap = _bias_tile(rel_bias_a[0], "clip", BAND_TQ, a_prev_rows + BAND_TQ, a_prev_rows, A_PREV_CHUNKS)
    bias_bp = _bias_tile(t5_bias, "t5", BAND_TQ, b_prev_rows + BAND_TQ, b_prev_rows, B_PREV_CHUNKS)
    bias_as = _bias_tile(rel_bias_a[0], "clip", TS, la_c + TS, la_c, None)
    bias_bs = _bias_tile(t5_bias, "t5", TS, lb_c + TS, lb_c, None)
    dup = lambda c: jnp.repeat(c, 2, axis=2).reshape(c.shape[0], c.shape[1], 2 * BKV_W)
    cak = cache_a_k[0].reshape(ns, la_c, A_W)
    cav = cache_a_v[0].reshape(ns, la_c, A_W)
    cbk, cbv = dup(cache_b_k[0]), dup(cache_b_v[0])

    mods = mods_all[0].reshape(n_cond_pad, 1, -1)
    state0 = {}
    for name, gr in groups.items():
        x, off, G, R = gr["x"], gr["blk_off"] // gr["G"], gr["G"], gr["R"]
        x = _ffn(x, mods, off, 0, norm_g[0, 0], ffn_w, 0, 0, G, R)
        ta, tb = (la_p, lb_p) if name == "p" else (TS, TS)
        qa, ka, va, qb, kb, vb, ka32, va32, kb32, vb32 = _proj_ab(x, mods, off, norm_g[0, 1], w_ab, G, R, ta, tb)
        if name == "p":
            prev = lambda arr, n: [(arr, BAND_TQ, d - n) for d in range(n + 1)]
            oa = _band_attn(qa, prev(ka, 2), prev(va, 2), bias_ap, None,
                            Tq=BAND_TQ, n_prev_rows=a_prev_rows, pair_map=pair_a)
            ob = _band_attn(qb, prev(kb, 1), prev(vb, 1), bias_bp, sinks_b[0],
                            Tq=BAND_TQ, n_prev_rows=b_prev_rows, pair_map=pair_b)
        else:
            oa = _band_attn(qa, [(cak, la_c, None), (ka, TS, None)], [(cav, la_c, None), (va, TS, None)],
                            bias_as, None, Tq=TS, n_prev_rows=0, pair_map=pair_a)
            ob = _band_attn(qb, [(cbk, lb_c, None), (kb, TS, None)], [(cbv, lb_c, None), (vb, TS, None)],
                            bias_bs, sinks_b[0], Tq=TS, n_prev_rows=0, pair_map=pair_b)
        x = _ffn(x, mods, off, 6, norm_g[0, 2], ffn_w, 0, 1, G, R, attn=([oa, ob], [w_oa, w_ob]))
        gr["x"] = x
        n_out = x.shape[0]
        state0[name] = (ka32.reshape(1, n_out, ta, A_HEADS, HEAD_DIM), va32.reshape(1, n_out, ta, A_HEADS, HEAD_DIM),
                        kb32.reshape(1, n_out, tb, B_KV_HEADS, HEAD_DIM), vb32.reshape(1, n_out, tb, B_KV_HEADS, HEAD_DIM))

    hw = C_NOPE + C_ROPE
    w_in = w_in_c[0]
    w_kr = w_in[:, C_Q_LORA + C_KV_LORA:]
    win_ext = jnp.concatenate([w_in[:, :C_Q_LORA + C_KV_LORA], jnp.zeros((D, C_NOPE), F32),
                               w_kr, _rot_half_cols(w_kr)], axis=1).astype(BF)
    wq3 = w_qb[0].reshape(C_Q_LORA, C_HEADS, hw)
    wqb_ext = jnp.concatenate([wq3, _rot_half_cols(wq3[..., C_NOPE:])], axis=-1
                              ).reshape(C_Q_LORA, C_HEADS * C_HEAD_W).astype(BF)
    wkv3 = w_kvb[0].reshape(C_KV_LORA, C_HEADS, C_NOPE + C_V)
    wk_ext = jnp.concatenate([wkv3[..., :C_NOPE], jnp.zeros((C_KV_LORA, C_HEADS, C_HEAD_W - C_NOPE), F32)],
                             axis=-1).reshape(C_KV_LORA, C_HEADS * C_HEAD_W).astype(BF)
    wvt = jnp.concatenate([wkv3[..., C_NOPE:], jnp.zeros((C_KV_LORA, C_HEADS, C_VT_ROWS - C_V), F32)], axis=-1
                          ).reshape(C_KV_LORA, C_HEADS * C_VT_ROWS).T.astype(BF)
    w_oc = w_out_c[0].astype(BF)
    tables = {"p": _rope_tables(np.arange(S)), "s": _rope_tables(past + np.arange(TS))}

    mods = mods_all[1].reshape(n_cond_pad, 1, -1)
    state1 = {}
    for name, gr in groups.items():
        x, off, G, R = gr["x"], gr["blk_off"] // gr["G"], gr["G"], gr["R"]
        x = _ffn(x, mods, off, 0, norm_g[1, 0], ffn_w, 1, 0, G, R)
        q, kv32, kr32 = _proj_c(x, mods, off, norm_g[1, 1], win_ext, c_q_norm_g[0], c_kv_norm_g[0],
                                wqb_ext, tables[name], G, R, q_transposed=(name == "p"))
        if name == "p":
            k, vt = _kv_expand(kv32, kr32, wk_ext, wvt, ROW_TILE)
            o = _mla_attn(q, k, vt.reshape(nb, C_HEADS, C_VT_ROWS, S), Tq=MLA_T, Tk=MLA_T, SK=MLA_SK, QW=MLA_QW,
                          causal=True)
        else:
            n_keys = past + TS
            n_pad = -(-n_keys // LANES) * LANES
            kv_all = jnp.zeros((ns, n_pad, C_KV_LORA), F32).at[:, :past].set(cache_c_kv[0]).at[:, past:n_keys].set(kv32)
            kr_all = jnp.zeros((ns, n_pad, C_ROPE), F32).at[:, :past].set(cache_c_kr[0]).at[:, past:n_keys].set(kr32)
            k, vt = _kv_expand(kv_all, kr_all, wk_ext, wvt, n_pad)
            o = _mla_attn(jnp.swapaxes(q, 2, 3), k, vt.reshape(ns, C_HEADS, C_VT_ROWS, n_pad), Tq=TS, Tk=n_pad,
                          SK=LANES, QW=TS,
                          causal=False, valid_len=n_keys)
        x = _ffn(x, mods, off, 6, norm_g[1, 2], ffn_w, 1, 1, G, R, final_g=final_norm_g, attn=([o], [w_oc]))
        gr["x"] = x
        state1[name] = (kv32[None], kr32[None])

    return (groups["p"]["x"], groups["s"]["x"],
            *state0["p"], *state1["p"], *state0["s"], *state1["s"])
```

```python
import functools
import math

import numpy as np
import jax
import jax.numpy as jnp
from jax import lax
from jax.experimental import pallas as pl
from jax.experimental.pallas import tpu as pltpu

F32 = jnp.float32
BF = jnp.bfloat16

CHUNK = 64
HEAD_DIM = 64
A_HEADS = 8
A_PREV_CHUNKS = 8
A_REL_CLIP = 128
B_Q_HEADS = 8
B_KV_HEADS = 2
B_PREV_CHUNKS = 2
T5_BUCKETS = 32
T5_MAX_DIST = 128
C_HEADS = 16
C_Q_LORA = 384
C_KV_LORA = 256
C_NOPE = 64
C_ROPE = 32
C_V = 64
ROPE_BASE = 10000.0
EPS = 1e-6
NEG = -1e30
A_W = A_HEADS * HEAD_DIM
BQ_W = B_Q_HEADS * HEAD_DIM
BKV_W = B_KV_HEADS * HEAD_DIM

LANES = 128
VMEM_LIMIT = 56 * 1024 * 1024

ROW_TILE = 512
FF_CHUNK = 256
BAND_TQ = 256
BAND_AHEAD = 2
LOG2E = math.log2(math.e)
MLA_T = 1024
MLA_SK = 256
MLA_QW = 256
MLA_HU = 8
MLA_AHEAD = 5
C_VT_ROWS = 80
C_IN_EXT = 768
C_HEAD_W = 128


def _cparams(n_axes, vmem=VMEM_LIMIT):
    return pltpu.CompilerParams(dimension_semantics=("arbitrary",) * n_axes,
                                vmem_limit_bytes=vmem)


def _const_spec(shape):
    n = len(shape)
    return pl.BlockSpec(shape, lambda *_: (0,) * n)


def _rms(x):
    return x * lax.rsqrt(jnp.mean(x * x, axis=-1, keepdims=True) + EPS)


def _norm_mod(x, g, shift, scale):
    return (_rms(x) * g) * (1.0 + scale) + shift


def _mod_spec(G, blk_off, k, D):
    return pl.BlockSpec((G, 1, D), lambda i, j: (blk_off + i, 0, k))


def _adaln_body(c_ref, w_ref, b_ref, o_ref):
    c = c_ref[...]
    s = (c * jax.nn.sigmoid(c)).astype(BF)
    o_ref[0] = jnp.dot(s, w_ref[0].astype(BF), preferred_element_type=F32) + b_ref[0]


def _adaln(c_all, w_ada, b_ada):
    L, D, N = w_ada.shape
    R = c_all.shape[0]
    tn = 1024
    return pl.pallas_call(
        _adaln_body,
        grid=(L, N // tn),
        in_specs=[pl.BlockSpec((R, D), lambda l, j: (0, 0)),
                  pl.BlockSpec((1, D, tn), lambda l, j: (l, 0, j)),
                  pl.BlockSpec((1, 1, tn), lambda l, j: (l, 0, j))],
        out_specs=pl.BlockSpec((1, R, tn), lambda l, j: (l, 0, j)),
        out_shape=jax.ShapeDtypeStruct((L, R, N), F32),
        compiler_params=_cparams(2),
        name="adaln",
    )(c_all, w_ada, b_ada.reshape(L, 1, N))


def _ffn_body(*refs, n_attn, final):
    attn_refs, wo_refs = refs[:n_attn], refs[n_attn:2 * n_attn]
    refs = refs[2 * n_attn:]
    if n_attn:
        ga_ref, refs = refs[0], refs[1:]
    x_ref, sh_ref, sc_ref, gt_ref, g_ref, wg_ref, wu_ref, wd_ref = refs[:8]
    if final:
        fg_ref, o_ref, a_ref = refs[8:]
    else:
        o_ref, a_ref = refs[8:]
    x = x_ref[...]
    G, R, D = x.shape
    F = wg_ref.shape[1]
    if n_attn:
        mix = None
        for at_ref, wo_ref in zip(attn_refs, wo_refs):
            y = jnp.dot(at_ref[...].reshape(G * R, at_ref.shape[2]), wo_ref[...], preferred_element_type=F32)
            mix = y if mix is None else mix + y
        x = x + ga_ref[...] * mix.reshape(G, R, D)
    hb = _norm_mod(x, g_ref[...], sh_ref[...], sc_ref[...]).reshape(G * R, D).astype(BF)
    for c in range(F // FF_CHUNK):
        lo, hi = c * FF_CHUNK, (c + 1) * FF_CHUNK
        g = jnp.dot(hb, wg_ref[:, lo:hi], preferred_element_type=F32)
        u = jnp.dot(hb, wu_ref[:, lo:hi], preferred_element_type=F32)
        a_ref[:, lo:hi] = (g * jax.nn.sigmoid(g) * u).astype(BF)
    ff = jnp.dot(a_ref[...], wd_ref[...], preferred_element_type=F32)
    y = x + (0.5 * gt_ref[...]) * ff.reshape(G, R, D)
    if final:
        y = _rms(y) * fg_ref[...]
    o_ref[...] = y


def _ffn(x, mods, blk_off, kbase, g, weights, layer, which, G, R, final_g=None, attn=None):
    nseq, T, D = x.shape
    wg, wu, wd = weights
    F = wg.shape[-1]
    final = final_g is not None
    row = lambda W: pl.BlockSpec((G, R, W), lambda i, j: (i, j, 0))
    stacked = lambda w: pl.BlockSpec((None, None) + w.shape[2:], lambda i, j: (layer, which, 0, 0),
                                     pipeline_mode=pl.Buffered(1))
    in_specs, args = [], []
    n_attn = 0
    if attn is not None:
        os_, ws = attn
        n_attn = len(os_)
        in_specs += [row(o.shape[2]) for o in os_] + [_const_spec(w.shape) for w in ws]
        in_specs.append(_mod_spec(G, blk_off, 5, D))
        args += [*os_, *ws, mods]
    in_specs += [row(D), _mod_spec(G, blk_off, kbase, D), _mod_spec(G, blk_off, kbase + 1, D),
                 _mod_spec(G, blk_off, kbase + 2, D), _const_spec((1, D)), stacked(wg), stacked(wu), stacked(wd)]
    args += [x, mods, mods, mods, g.reshape(1, D), wg, wu, wd]
    if final:
        in_specs.append(_const_spec((1, D)))
        args.append(final_g.reshape(1, D))
    return pl.pallas_call(
        functools.partial(_ffn_body, n_attn=n_attn, final=final),
        grid=(nseq // G, T // R),
        in_specs=in_specs,
        out_specs=pl.BlockSpec((G, R, D), lambda i, j: (i, j, 0)),
        out_shape=jax.ShapeDtypeStruct((nseq, T, D), F32),
        scratch_shapes=[pltpu.VMEM((G * R, F), BF)],
        compiler_params=_cparams(2),
        name="ffn",
    )(*args)


def _dup_halves(k):
    rolled = pltpu.roll(k, HEAD_DIM, 1)
    lo = lax.broadcasted_iota(jnp.int32, k.shape, 1) < HEAD_DIM
    return jnp.concatenate([jnp.where(lo, k, rolled), jnp.where(lo, rolled, k)], axis=1)


def _proj_ab_body(x_ref, sh_ref, sc_ref, g_ref, w_ref,
                  qa_ref, ka_ref, va_ref, qb_ref, kb_ref, vb_ref,
                  ka32_ref, va32_ref, kb32_ref, vb32_ref, *, ta, tb):
    x = x_ref[...]
    G, R, D = x.shape
    hb = _norm_mod(x, g_ref[...], sh_ref[...], sc_ref[...]).reshape(G * R, D).astype(BF)
    res = jnp.dot(hb, w_ref[...], preferred_element_type=F32)
    o_ka, o_va, o_qb, o_kb, o_vb = A_W, 2 * A_W, 3 * A_W, 3 * A_W + BQ_W, 3 * A_W + BQ_W + BKV_W
    qscale = HEAD_DIM ** -0.5 * LOG2E
    qa_ref[...] = (res[:, :o_ka] * qscale).astype(BF).reshape(G, R, A_W)
    ka_ref[...] = res[:, o_ka:o_va].astype(BF).reshape(G, R, A_W)
    va_ref[...] = res[:, o_va:o_qb].astype(BF).reshape(G, R, A_W)
    qb_ref[...] = (res[:, o_qb:o_kb] * qscale).astype(BF).reshape(G, R, BQ_W)
    kb = res[:, o_kb:o_vb]
    vb = res[:, o_vb:o_vb + BKV_W]
    kb_ref[...] = _dup_halves(kb).astype(BF).reshape(G, R, 2 * BKV_W)
    vb_ref[...] = _dup_halves(vb).astype(BF).reshape(G, R, 2 * BKV_W)

    @pl.when(pl.program_id(1) == pl.num_programs(1) - 1)
    def _():
        ka32_ref[...] = res[:, o_ka:o_va].reshape(G, R, A_W)[:, R - ta:, :]
        va32_ref[...] = res[:, o_va:o_qb].reshape(G, R, A_W)[:, R - ta:, :]
        kb32_ref[...] = kb.reshape(G, R, BKV_W)[:, R - tb:, :]
        vb32_ref[...] = vb.reshape(G, R, BKV_W)[:, R - tb:, :]


def _proj_ab(x, mods, blk_off, g, w, G, R, ta, tb):
    nseq, T, D = x.shape
    row = lambda W: pl.BlockSpec((G, R, W), lambda i, j: (i, j, 0))
    tail = lambda t, W: pl.BlockSpec((G, t, W), lambda i, j: (i, 0, 0))
    bshape = lambda W: jax.ShapeDtypeStruct((nseq, T, W), BF)
    return pl.pallas_call(
        functools.partial(_proj_ab_body, ta=ta, tb=tb),
        grid=(nseq // G, T // R),
        in_specs=[row(D), _mod_spec(G, blk_off, 3, D), _mod_spec(G, blk_off, 4, D),
                  _const_spec((1, D)), _const_spec(w.shape)],
        out_specs=[row(A_W), row(A_W), row(A_W), row(BQ_W), row(2 * BKV_W), row(2 * BKV_W),
                   tail(ta, A_W), tail(ta, A_W), tail(tb, BKV_W), tail(tb, BKV_W)],
        out_shape=[bshape(A_W), bshape(A_W), bshape(A_W), bshape(BQ_W), bshape(2 * BKV_W), bshape(2 * BKV_W),
                   jax.ShapeDtypeStruct((nseq, ta, A_W), F32), jax.ShapeDtypeStruct((nseq, ta, A_W), F32),
                   jax.ShapeDtypeStruct((nseq, tb, BKV_W), F32), jax.ShapeDtypeStruct((nseq, tb, BKV_W), F32)],
        compiler_params=_cparams(2),
        name="proj_ab",
    )(x, mods, mods, g.reshape(1, D), w)


def _t5_bucket_np(rel):
    nb = T5_BUCKETS // 2
    ret = np.where(rel > 0, nb, 0)
    n = np.abs(rel)
    max_exact = nb // 2
    ratio = np.maximum(n, 1).astype(np.float32) / np.float32(max_exact)
    large = max_exact + (np.log(ratio).astype(np.float32) / np.float32(math.log(T5_MAX_DIST / max_exact))
                         * np.float32(nb - max_exact)).astype(np.int32)
    large = np.minimum(large, nb - 1)
    return ret + np.where(n < max_exact, n, large)


def _bias_body(thi_ref, tlo_ref, idx_ref, o_ref, *, Tq, span, prev_chunks, band_chunks):
    E = thi_ref.shape[1]
    L = idx_ref.shape[1]
    onehot = (lax.broadcasted_iota(jnp.int32, (E, L), 0) == idx_ref[...]).astype(BF)
    t = (jnp.dot(thi_ref[...], onehot, preferred_element_type=F32)
         + jnp.dot(tlo_ref[...], onehot, preferred_element_type=F32)) * LOG2E
    if band_chunks is not None:
        qc = lax.broadcasted_iota(jnp.int32, (Tq, span), 0) // CHUNK
        kc = lax.broadcasted_iota(jnp.int32, (Tq, span), 1) // CHUNK - prev_chunks
        valid = (kc <= qc) & (kc >= qc - band_chunks)
    for h in range(o_ref.shape[0]):
        x = jnp.broadcast_to(t[h:h + 1, :], (Tq, L))
        y = pltpu.roll(x, L - Tq + 1, 1, stride=1, stride_axis=0)[:, :span]
        if band_chunks is not None:
            y = jnp.where(valid, y, NEG)
        o_ref[h] = y


def _bias_tile(table, kind, Tq, span, n_prev_rows, band_chunks):
    H = table.shape[1]
    L = -(-(Tq + span - 1) // LANES) * LANES
    rel = np.arange(L) - (Tq - 1) - n_prev_rows
    if kind == "clip":
        idx = np.clip(rel, -A_REL_CLIP, A_REL_CLIP) + A_REL_CLIP
    else:
        idx = _t5_bucket_np(rel)
    E = -(-table.shape[0] // LANES) * LANES
    tt = jnp.zeros((H, E), F32).at[:, :table.shape[0]].set(table.T.astype(F32))
    thi = tt.astype(BF)
    tlo = (tt - thi.astype(F32)).astype(BF)
    return pl.pallas_call(
        functools.partial(_bias_body, Tq=Tq, span=span, prev_chunks=n_prev_rows // CHUNK,
                          band_chunks=band_chunks),
        out_shape=jax.ShapeDtypeStruct((H, Tq, span), F32),
        compiler_params=pltpu.CompilerParams(vmem_limit_bytes=VMEM_LIMIT),
        name="bias_tile",
    )(thi, tlo, jnp.asarray(idx.reshape(1, L), jnp.int32))


def _band_body(*refs, n_parts, n_prev_rows, pair_map, has_sink, Tq, windows):
    q_ref = refs[0]
    k_refs = refs[1:1 + n_parts]
    v_refs = refs[1 + n_parts:1 + 2 * n_parts]
    bias_ref = refs[1 + 2 * n_parts]
    sink_ref = refs[2 + 2 * n_parts] if has_sink else None
    o_ref = refs[-1]
    i = pl.program_id(1)
    k = jnp.concatenate([r[0].astype(BF) for r in k_refs], axis=0)
    v = jnp.concatenate([r[0].astype(BF) for r in v_refs], axis=0)
    q = q_ref[0]
    span = k.shape[0]
    n_heads = 2 * (q.shape[1] // LANES)
    lane_lo = lax.broadcasted_iota(jnp.int32, (1, LANES), 1) < HEAD_DIM

    items = [(w, h) for w in range(len(windows)) for h in range(n_heads)]

    def sweep(mask_start):
        def scores(w, h):
            qs, qr, ks, kr = windows[w]
            p = h // 2
            qp = q[qs:qs + qr, p * LANES:(p + 1) * LANES]
            qm = jnp.where(lane_lo if h % 2 == 0 else jnp.logical_not(lane_lo), qp, jnp.zeros_like(qp))
            kp = k[ks:ks + kr, pair_map[p] * LANES:(pair_map[p] + 1) * LANES]
            s = lax.dot_general(qm, kp, (((1,), (1,)), ((), ())), preferred_element_type=F32) + bias_ref[h]
            if mask_start and ks < n_prev_rows:
                s = jnp.where(lax.broadcasted_iota(jnp.int32, (1, kr), 1) >= (n_prev_rows - ks - i * Tq), s, NEG)
            return s

        pending = [scores(*it) for it in items[:BAND_AHEAD]]
        halves = []
        for n, (w, h) in enumerate(items):
            qs, qr, ks, kr = windows[w]
            s = pending.pop(0)
            if n + BAND_AHEAD < len(items):
                pending.append(scores(*items[n + BAND_AHEAD]))
            p = h // 2
            vp = v[ks:ks + kr, pair_map[p] * LANES:(pair_map[p] + 1) * LANES]
            m = jnp.max(s, axis=-1, keepdims=True)
            if has_sink:
                sink = sink_ref[h] * LOG2E
                m = jnp.maximum(m, sink)
            e = jnp.exp2(s - m)
            den = jnp.sum(e, axis=-1, keepdims=True)
            if has_sink:
                den = den + jnp.exp2(sink - m)
            o = jnp.dot(e.astype(BF), vp, preferred_element_type=F32)
            halves.append(o * (1.0 / den))
            if h % 2 == 1:
                o_ref[0, qs:qs + qr, p * LANES:(p + 1) * LANES] = jnp.where(lane_lo, halves[0], halves[1]).astype(BF)
                halves = []

    if n_prev_rows:
        pl.when(i * Tq < n_prev_rows)(lambda: sweep(True))
        pl.when(i * Tq >= n_prev_rows)(lambda: sweep(False))
    else:
        sweep(False)


def _band_attn(q, kparts, vparts, bias, sinks, *, Tq, n_prev_rows, pair_map, windows=None):
    nseq, T, W = q.shape
    if windows is None:
        windows = ((0, Tq, 0, sum(p[1] for p in kparts)),)

    def part_spec(arr, rows, off):
        if off is None:
            return pl.BlockSpec((1, rows, arr.shape[2]), lambda b, i: (b, 0, 0))
        return pl.BlockSpec((1, rows, arr.shape[2]), lambda b, i: (b, jnp.maximum(i + off, 0), 0))

    in_specs = [pl.BlockSpec((1, Tq, W), lambda b, i: (b, i, 0))]
    in_specs += [part_spec(*p) for p in kparts] + [part_spec(*p) for p in vparts]
    in_specs.append(_const_spec(bias.shape))
    args = [q] + [p[0] for p in kparts] + [p[0] for p in vparts] + [bias]
    if sinks is not None:
        in_specs.append(pl.BlockSpec(memory_space=pltpu.SMEM))
        args.append(sinks.astype(F32))
    return pl.pallas_call(
        functools.partial(_band_body, n_parts=len(kparts), n_prev_rows=n_prev_rows,
                          pair_map=pair_map, has_sink=sinks is not None, Tq=Tq, windows=tuple(windows)),
        grid=(nseq, T // Tq),
        in_specs=in_specs,
        out_specs=pl.BlockSpec((1, Tq, W), lambda b, i: (b, i, 0)),
        out_shape=jax.ShapeDtypeStruct((nseq, T, W), BF),
        compiler_params=_cparams(2),
        name="band_attn",
    )(*args)


def _out_proj_body(*refs, n_in):
    o_refs = refs[:n_in]
    w_refs = refs[n_in:2 * n_in]
    x_ref, gt_ref, out_ref = refs[2 * n_in:]
    x = x_ref[...]
    G, R, D = x.shape
    acc = None
    for o_ref, w_ref in zip(o_refs, w_refs):
        y = jnp.dot(o_ref[...].reshape(G * R, o_ref.shape[2]), w_ref[...], preferred_element_type=F32)
        acc = y if acc is None else acc + y
    out_ref[...] = x + gt_ref[...] * acc.reshape(G, R, D)


def _out_proj(os_, ws, x, mods, blk_off, G, R):
    nseq, T, D = x.shape
    row = lambda W: pl.BlockSpec((G, R, W), lambda i, j: (i, j, 0))
    return pl.pallas_call(
        functools.partial(_out_proj_body, n_in=len(os_)),
        grid=(nseq // G, T // R),
        in_specs=[row(o.shape[2]) for o in os_] + [_const_spec(w.shape) for w in ws]
                 + [row(D), _mod_spec(G, blk_off, 5, D)],
        out_specs=row(D),
        out_shape=jax.ShapeDtypeStruct((nseq, T, D), F32),
        compiler_params=_cparams(2),
        name="out_proj",
    )(*os_, *ws, x, mods)


def _proj_c_body(x_ref, sh_ref, sc_ref, g_ref, win_ref, qn_ref, kvn_ref, wqb_ref, cos_ref, sin_ref,
                 cost_ref, sint_ref, q_ref, kv32_ref, kr32_ref, *, q_transposed):
    x = x_ref[...]
    G, R, D = x.shape
    hb = _norm_mod(x, g_ref[...], sh_ref[...], sc_ref[...]).reshape(G * R, D).astype(BF)
    res = jnp.dot(hb, win_ref[...], preferred_element_type=F32)
    q_lat = res[:, :C_Q_LORA]
    kv_lat = res[:, C_Q_LORA:C_Q_LORA + C_KV_LORA]
    krg = res[:, C_Q_LORA + C_KV_LORA:]
    cos = cos_ref[...]
    sin = sin_ref[...]
    qn = (_rms(q_lat) * qn_ref[...]).astype(BF)
    scale = (C_NOPE + C_ROPE) ** -0.5 * math.log2(math.e)
    nh = q_ref.shape[1]
    if q_transposed:
        qt = lax.dot_general(wqb_ref[...], qn, (((1,), (1,)), ((), ())), preferred_element_type=F32)
        cost, sint = cost_ref[...], sint_ref[...]
        for h in range(nh):
            blk = qt[h * C_HEAD_W:(h + 1) * C_HEAD_W]
            roped = blk[C_NOPE:C_NOPE + C_ROPE] * cost + blk[C_NOPE + C_ROPE:] * sint
            q_ref[0, h] = (jnp.concatenate([blk[:C_NOPE], roped, jnp.zeros_like(roped)], axis=0) * scale).astype(BF)
    else:
        qr = jnp.dot(qn, wqb_ref[...], preferred_element_type=F32)
        qrot = pltpu.roll(qr, qr.shape[1] - C_ROPE, 1)
        for h in range(nh):
            a = qr[:, h * C_HEAD_W:(h + 1) * C_HEAD_W].reshape(G, R, C_HEAD_W)
            b = qrot[:, h * C_HEAD_W:(h + 1) * C_HEAD_W].reshape(G, R, C_HEAD_W)
            q_ref[:, h] = ((a * cos + b * sin) * scale).astype(BF)
    kv32_ref[...] = (_rms(kv_lat) * kvn_ref[...]).reshape(G, R, C_KV_LORA)
    krot = pltpu.roll(krg, C_HEAD_W - C_ROPE, 1)
    krf = krg.reshape(G, R, C_HEAD_W) * cos + krot.reshape(G, R, C_HEAD_W) * sin
    kr32_ref[...] = krf[:, :, C_NOPE:C_NOPE + C_ROPE]


def _proj_c(x, mods, blk_off, g, win, qn_g, kvn_g, wqb, tables, G, R, q_transposed):
    nseq, T, D = x.shape
    nh = C_HEADS
    cos_t, sin_t, cost_t, sint_t = tables
    tab = pl.BlockSpec((1, R, C_HEAD_W), lambda i, j: (0, j, 0))
    tabt = pl.BlockSpec((C_ROPE, R), lambda i, j: (0, j))
    if q_transposed:
        assert G == 1
        wqb = wqb.T
        q_spec = pl.BlockSpec((1, nh, C_HEAD_W, R), lambda i, j: (i, 0, 0, j))
        q_shape = jax.ShapeDtypeStruct((nseq, nh, C_HEAD_W, T), BF)
    else:
        q_spec = pl.BlockSpec((G, nh, R, C_HEAD_W), lambda i, j: (i, 0, j, 0))
        q_shape = jax.ShapeDtypeStruct((nseq, nh, T, C_HEAD_W), BF)
    return pl.pallas_call(
        functools.partial(_proj_c_body, q_transposed=q_transposed),
        grid=(nseq // G, T // R),
        in_specs=[pl.BlockSpec((G, R, D), lambda i, j: (i, j, 0)),
                  _mod_spec(G, blk_off, 3, D), _mod_spec(G, blk_off, 4, D), _const_spec((1, D)),
                  _const_spec(win.shape), _const_spec((1, C_Q_LORA)), _const_spec((1, C_KV_LORA)),
                  _const_spec(wqb.shape), tab, tab, tabt, tabt],
        out_specs=[q_spec,
                   pl.BlockSpec((G, R, C_KV_LORA), lambda i, j: (i, j, 0)),
                   pl.BlockSpec((G, R, C_ROPE), lambda i, j: (i, j, 0))],
        out_shape=[q_shape,
                   jax.ShapeDtypeStruct((nseq, T, C_KV_LORA), F32),
                   jax.ShapeDtypeStruct((nseq, T, C_ROPE), F32)],
        compiler_params=_cparams(2),
        name="proj_c",
    )(x, mods, mods, g.reshape(1, D), win, qn_g.reshape(1, -1), kvn_g.reshape(1, -1), wqb,
      cos_t, sin_t, cost_t, sint_t)


def _kvexp_body(kv_ref, kr_ref, wk_ref, wvt_ref, place_ref, ones_ref, k_ref, vt_ref):
    kv = kv_ref[0].astype(BF)
    krf = jnp.dot(kr_ref[0].astype(BF), place_ref[...], preferred_element_type=F32)
    kx = jnp.dot(kv, wk_ref[...], preferred_element_type=F32)
    for h in range(k_ref.shape[1]):
        k_ref[0, h] = (kx[:, h * C_HEAD_W:(h + 1) * C_HEAD_W] + krf).astype(BF)
    vt = lax.dot_general(wvt_ref[...], kv, (((1,), (1,)), ((), ())), preferred_element_type=F32)
    vt_ref[0] = (vt + ones_ref[...]).astype(BF)


def _kv_expand(kv, kr, wk, wvt, R):
    nseq, T, _ = kv.shape
    nh = wk.shape[1] // C_HEAD_W
    place = np.zeros((C_ROPE, C_HEAD_W), np.float32)
    place[np.arange(C_ROPE), C_NOPE + np.arange(C_ROPE)] = 1.0
    ones = np.zeros((nh, C_VT_ROWS, 1), np.float32)
    ones[:, C_V] = 1.0
    ones = ones.reshape(nh * C_VT_ROWS, 1)
    return pl.pallas_call(
        _kvexp_body,
        grid=(nseq, T // R),
        in_specs=[pl.BlockSpec((1, R, C_KV_LORA), lambda b, j: (b, j, 0)),
                  pl.BlockSpec((1, R, C_ROPE), lambda b, j: (b, j, 0)),
                  _const_spec(wk.shape), _const_spec(wvt.shape), _const_spec(place.shape),
                  _const_spec(ones.shape)],
        out_specs=[pl.BlockSpec((1, nh, R, C_HEAD_W), lambda b, j: (b, 0, j, 0)),
                   pl.BlockSpec((1, nh * C_VT_ROWS, R), lambda b, j: (b, 0, j))],
        out_shape=[jax.ShapeDtypeStruct((nseq, nh, T, C_HEAD_W), BF),
                   jax.ShapeDtypeStruct((nseq, nh * C_VT_ROWS, T), BF)],
        compiler_params=_cparams(2),
        name="kv_expand",
    )(kv, kr, wk, wvt, jnp.asarray(place, BF), jnp.asarray(ones))


def _flash_body(qi_ref, kj_ref, q_ref, k_ref, vt_ref, o_ref, acc_ref, m_ref,
                *, Tq, Tk, SK, QW, causal, valid_len):
    t = pl.program_id(1)
    i = qi_ref[t]
    j = kj_ref[t]
    nh = q_ref.shape[1]
    n_stripes = Tq // QW

    @pl.when(j == 0)
    def _():
        m_ref[...] = jnp.full(m_ref.shape, NEG, F32)
        acc_ref[...] = jnp.zeros(acc_ref.shape, F32)

    def sweep(diag):
        tiles = []
        n_sub = Tk // SK if valid_len is None else -(-valid_len // SK)
        for c in range(n_sub):
            for r in range(n_stripes):
                if diag:
                    if c * SK >= (r + 1) * QW:
                        continue
                    mask = "chunk" if (c + 1) * SK > r * QW + CHUNK else None
                else:
                    mask = "len" if valid_len is not None and (c + 1) * SK > valid_len else None
                tiles.append((c, r, mask))

        def head_group(g, carry):
            heads = [g * MLA_HU + u for u in range(MLA_HU)]
            stream = [(u, c, r, mask) for u in range(MLA_HU) for (c, r, mask) in tiles]
            state = {(u, r): (m_ref[heads[u], :, r * QW:(r + 1) * QW], acc_ref[heads[u], :, r * QW:(r + 1) * QW])
                     for u in range(MLA_HU) for r in range(n_stripes)}

            def scores(u, c, r, mask):
                s = jnp.dot(k_ref[0, heads[u], c * SK:(c + 1) * SK, :], q_ref[0, heads[u], :, r * QW:(r + 1) * QW],
                            preferred_element_type=F32)
                kpos = lax.broadcasted_iota(jnp.int32, (SK, 1), 0) + c * SK
                if mask == "chunk":
                    qpos = lax.broadcasted_iota(jnp.int32, (1, QW), 1) + r * QW
                    s = jnp.where(kpos // CHUNK <= qpos // CHUNK, s, NEG)
                elif mask == "len":
                    s = jnp.where(kpos < valid_len, s, NEG)
                return s

            pending = [scores(*tl) for tl in stream[:MLA_AHEAD]]
            for n, (u, c, r, _) in enumerate(stream):
                s = pending.pop(0)
                if n + MLA_AHEAD < len(stream):
                    pending.append(scores(*stream[n + MLA_AHEAD]))
                m, acc = state[u, r]
                m_new = jnp.maximum(m, jnp.max(s, axis=0, keepdims=True))
                p = jnp.exp2(s - m_new).astype(BF)
                acc = jnp.exp2(m - m_new) * acc + jnp.dot(vt_ref[0, heads[u], :, c * SK:(c + 1) * SK], p,
                                                          preferred_element_type=F32)
                state[u, r] = (m_new, acc)
            for (u, r), (m, acc) in state.items():
                m_ref[heads[u], :, r * QW:(r + 1) * QW] = m
                acc_ref[heads[u], :, r * QW:(r + 1) * QW] = acc
            return carry
        lax.fori_loop(0, nh // MLA_HU, head_group, 0)

    if causal:
        pl.when(j == i)(lambda: sweep(True))
        pl.when(j != i)(lambda: sweep(False))
        last = i
    else:
        sweep(False)
        last = 0

    @pl.when(j == last)
    def _():
        for p in range(nh // 2):
            halves = [acc_ref[h, :C_V, :] * (1.0 / acc_ref[h, C_V:C_V + 1, :]) for h in (2 * p, 2 * p + 1)]
            o_ref[0, :, p * LANES:(p + 1) * LANES] = jnp.concatenate(halves, axis=0).T.astype(BF)


def _mla_attn(q, k, vt, *, Tq, Tk, SK, QW, causal, valid_len=None):
    nseq, nh, _, T = q.shape
    S = k.shape[2]
    nq, nk = T // Tq, S // Tk
    assert Tq % QW == 0 and Tk % SK == 0 and SK % CHUNK == 0
    if causal:
        assert Tq == Tk and QW % CHUNK == 0 and valid_len is None
        pairs = [(i, j) for i in range(nq) for j in range(i + 1)]
    else:
        assert nk == 1
        pairs = [(i, 0) for i in range(nq)]
    qi = jnp.asarray([p[0] for p in pairs], jnp.int32)
    kj = jnp.asarray([p[1] for p in pairs], jnp.int32)
    grid_spec = pltpu.PrefetchScalarGridSpec(
        num_scalar_prefetch=2,
        grid=(nseq, len(pairs)),
        in_specs=[pl.BlockSpec((1, nh, C_HEAD_W, Tq), lambda b, t, qi, kj: (b, 0, 0, qi[t])),
                  pl.BlockSpec((1, nh, Tk, C_HEAD_W), lambda b, t, qi, kj: (b, 0, kj[t], 0)),
                  pl.BlockSpec((1, nh, C_VT_ROWS, Tk), lambda b, t, qi, kj: (b, 0, 0, kj[t]))],
        out_specs=pl.BlockSpec((1, Tq, nh * C_V), lambda b, t, qi, kj: (b, qi[t], 0)),
        scratch_shapes=[pltpu.VMEM((nh, C_VT_ROWS, Tq), F32), pltpu.VMEM((nh, 1, Tq), F32)],
    )
    return pl.pallas_call(
        functools.partial(_flash_body, Tq=Tq, Tk=Tk, SK=SK, QW=QW, causal=causal, valid_len=valid_len),
        grid_spec=grid_spec,
        out_shape=jax.ShapeDtypeStruct((nseq, T, nh * C_V), BF),
        compiler_params=_cparams(2),
        name="mla_attn",
    )(qi, kj, q, k, vt)


def _rope_tables(pos):
    half = C_ROPE // 2
    inv = (np.float32(ROPE_BASE) ** (-np.arange(half, dtype=np.float32) / np.float32(half))).astype(np.float32)
    ang = (pos.astype(np.float32)[:, None] * inv[None, :]).astype(np.float32).astype(np.float64)
    n = pos.shape[0]
    cos = np.zeros((1, n, C_HEAD_W), np.float32)
    sin = np.zeros((1, n, C_HEAD_W), np.float32)
    cos[0, :, :C_NOPE] = 1.0
    cos[0, :, C_NOPE:C_NOPE + C_ROPE] = np.concatenate([np.cos(ang), np.cos(ang)], axis=1)
    sin[0, :, C_NOPE:C_NOPE + C_ROPE] = np.concatenate([np.sin(ang), np.sin(ang)], axis=1)
    cos32, sin32 = cos[0, :, C_NOPE:C_NOPE + C_ROPE].T, sin[0, :, C_NOPE:C_NOPE + C_ROPE].T
    return jnp.asarray(cos), jnp.asarray(sin), jnp.asarray(cos32), jnp.asarray(sin32)


def _rot_half_cols(w):
    half = w.shape[-1] // 2
    return jnp.concatenate([-w[..., half:], w[..., :half]], axis=-1)


def kernel(x_prompt, x_sample, c_prompt, c_sample, cache_a_k, cache_a_v, cache_b_k, cache_b_v, cache_c_kv, cache_c_kr, w_ada, b_ada, norm_g, final_norm_g, ffn_w_gate, ffn_w_up, ffn_w_down, w_in_ab, w_out_ab, rel_bias_a, t5_bias, sinks_b, w_in_c, c_q_norm_g, c_kv_norm_g, w_qb, w_kvb, w_out_c):
    nb, S, D = x_prompt.shape
    ns, TS, _ = x_sample.shape
    la_c, lb_c = cache_a_k.shape[2], cache_b_k.shape[2]
    past = cache_c_kv.shape[2]
    la_p, lb_p = min(A_PREV_CHUNKS * CHUNK, S), min(B_PREV_CHUNKS * CHUNK, S)
    assert la_p <= ROW_TILE and S % MLA_T == 0 and S % ROW_TILE == 0 and TS % 8 == 0

    n_cond = ns + nb
    n_cond_pad = -(-n_cond // 8) * 8
    c_all = jnp.zeros((n_cond_pad, D), F32).at[:ns].set(c_sample).at[ns:n_cond].set(c_prompt)
    mods_all = _adaln(c_all, w_ada, b_ada)
    groups = {
        "p": dict(x=x_prompt, blk_off=ns, G=1, R=ROW_TILE),
        "s": dict(x=x_sample, blk_off=0, G=ns, R=TS),
    }

    ffn_w = tuple(w.astype(BF) for w in (ffn_w_gate, ffn_w_up, ffn_w_down))
    w_ab = w_in_ab[0].astype(BF)
    w_oa, w_ob = w_out_ab[0, :A_W].astype(BF), w_out_ab[0, A_W:].astype(BF)
    pair_a = tuple(range(A_HEADS // 2))
    pair_b = tuple(p // (B_Q_HEADS // B_KV_HEADS // 2) for p in range(B_Q_HEADS // 2))
    a_prev_rows = 2 * BAND_TQ
    b_prev_rows = BAND_TQ
    assert a_prev_rows == A_PREV_CHUNKS * CHUNK and b_prev_rows >= B_PREV_CHUNKS * CHUNK
    bias_ap = _bias_tile(rel_bias_a[0], "clip", BAND_TQ, a_prev_rows + BAND_TQ, a_prev_rows, A_PREV_CHUNKS)
    b_half = BAND_TQ // 2
    b_back = B_PREV_CHUNKS * CHUNK
    assert b_half % CHUNK == 0 and b_back <= b_prev_rows
    windows_b = tuple((qs, b_half, b_prev_rows + qs - b_back, b_back + b_half) for qs in (0, b_half))
    bias_bp = _bias_tile(t5_bias, "t5", b_half, b_back + b_half, b_back, B_PREV_CHUNKS)
    bias_as = _bias_tile(rel_bias_a[0], "clip", TS, la_c + TS, la_c, None)
    bias_bs = _bias_tile(t5_bias, "t5", TS, lb_c + TS, lb_c, None)
    dup = lambda c: jnp.repeat(c, 2, axis=2).reshape(c.shape[0], c.shape[1], 2 * BKV_W)
    cak = cache_a_k[0].reshape(ns, la_c, A_W)
    cav = cache_a_v[0].reshape(ns, la_c, A_W)
    cbk, cbv = dup(cache_b_k[0]), dup(cache_b_v[0])

    mods = mods_all[0].reshape(n_cond_pad, 1, -1)
    state0 = {}
    for name, gr in groups.items():
        x, off, G, R = gr["x"], gr["blk_off"] // gr["G"], gr["G"], gr["R"]
        x = _ffn(x, mods, off, 0, norm_g[0, 0], ffn_w, 0, 0, G, R)
        ta, tb = (la_p, lb_p) if name == "p" else (TS, TS)
        qa, ka, va, qb, kb, vb, ka32, va32, kb32, vb32 = _proj_ab(x, mods, off, norm_g[0, 1], w_ab, G, R, ta, tb)
        if name == "p":
            prev = lambda arr, n: [(arr, BAND_TQ, d - n) for d in range(n + 1)]
            oa = _band_attn(qa, prev(ka, 2), prev(va, 2), bias_ap, None,
                            Tq=BAND_TQ, n_prev_rows=a_prev_rows, pair_map=pair_a)
            ob = _band_attn(qb, prev(kb, 1), prev(vb, 1), bias_bp, sinks_b[0],
                            Tq=BAND_TQ, n_prev_rows=b_prev_rows, pair_map=pair_b, windows=windows_b)
        else:
            oa = _band_attn(qa, [(cak, la_c, None), (ka, TS, None)], [(cav, la_c, None), (va, TS, None)],
                            bias_as, None, Tq=TS, n_prev_rows=0, pair_map=pair_a)
            ob = _band_attn(qb, [(cbk, lb_c, None), (kb, TS, None)], [(cbv, lb_c, None), (vb, TS, None)],
                            bias_bs, sinks_b[0], Tq=TS, n_prev_rows=0, pair_map=pair_b)
        x = _ffn(x, mods, off, 6, norm_g[0, 2], ffn_w, 0, 1, G, R, attn=([oa, ob], [w_oa, w_ob]))
        gr["x"] = x
        n_out = x.shape[0]
        state0[name] = (ka32.reshape(1, n_out, ta, A_HEADS, HEAD_DIM), va32.reshape(1, n_out, ta, A_HEADS, HEAD_DIM),
                        kb32.reshape(1, n_out, tb, B_KV_HEADS, HEAD_DIM), vb32.reshape(1, n_out, tb, B_KV_HEADS, HEAD_DIM))

    hw = C_NOPE + C_ROPE
    w_in = w_in_c[0]
    w_kr = w_in[:, C_Q_LORA + C_KV_LORA:]
    win_ext = jnp.concatenate([w_in[:, :C_Q_LORA + C_KV_LORA], jnp.zeros((D, C_NOPE), F32),
                               w_kr, _rot_half_cols(w_kr)], axis=1).astype(BF)
    wq3 = w_qb[0].reshape(C_Q_LORA, C_HEADS, hw)
    wqb_ext = jnp.concatenate([wq3, _rot_half_cols(wq3[..., C_NOPE:])], axis=-1
                              ).reshape(C_Q_LORA, C_HEADS * C_HEAD_W).astype(BF)
    wkv3 = w_kvb[0].reshape(C_KV_LORA, C_HEADS, C_NOPE + C_V)
    wk_ext = jnp.concatenate([wkv3[..., :C_NOPE], jnp.zeros((C_KV_LORA, C_HEADS, C_HEAD_W - C_NOPE), F32)],
                             axis=-1).reshape(C_KV_LORA, C_HEADS * C_HEAD_W).astype(BF)
    wvt = jnp.concatenate([wkv3[..., C_NOPE:], jnp.zeros((C_KV_LORA, C_HEADS, C_VT_ROWS - C_V), F32)], axis=-1
                          ).reshape(C_KV_LORA, C_HEADS * C_VT_ROWS).T.astype(BF)
    w_oc = w_out_c[0].astype(BF)
    tables = {"p": _rope_tables(np.arange(S)), "s": _rope_tables(past + np.arange(TS))}

    mods = mods_all[1].reshape(n_cond_pad, 1, -1)
    state1 = {}
    for name, gr in groups.items():
        x, off, G, R = gr["x"], gr["blk_off"] // gr["G"], gr["G"], gr["R"]
        x = _ffn(x, mods, off, 0, norm_g[1, 0], ffn_w, 1, 0, G, R)
        q, kv32, kr32 = _proj_c(x, mods, off, norm_g[1, 1], win_ext, c_q_norm_g[0], c_kv_norm_g[0],
                                wqb_ext, tables[name], G, R, q_transposed=(name == "p"))
        if name == "p":
            k, vt = _kv_expand(kv32, kr32, wk_ext, wvt, ROW_TILE)
            o = _mla_attn(q, k, vt.reshape(nb, C_HEADS, C_VT_ROWS, S), Tq=MLA_T, Tk=MLA_T, SK=MLA_SK, QW=MLA_QW,
                          causal=True)
        else:
            n_keys = past + TS
            n_pad = -(-n_keys // LANES) * LANES
            kv_all = jnp.zeros((ns, n_pad, C_KV_LORA), F32).at[:, :past].set(cache_c_kv[0]).at[:, past:n_keys].set(kv32)
            kr_all = jnp.zeros((ns, n_pad, C_ROPE), F32).at[:, :past].set(cache_c_kr[0]).at[:, past:n_keys].set(kr32)
            k, vt = _kv_expand(kv_all, kr_all, wk_ext, wvt, n_pad)
            o = _mla_attn(jnp.swapaxes(q, 2, 3), k, vt.reshape(ns, C_HEADS, C_VT_ROWS, n_pad), Tq=TS, Tk=n_pad,
                          SK=LANES, QW=TS,
                          causal=False, valid_len=n_keys)
        x = _ffn(x, mods, off, 6, norm_g[1, 2], ffn_w, 1, 1, G, R, final_g=final_norm_g, attn=([o], [w_oc]))
        gr["x"] = x
        state1[name] = (kv32[None], kr32[None])

    return (groups["p"]["x"], groups["s"]["x"],
            *state0["p"], *state1["p"], *state0["s"], *state1["s"])
```

```python
import functools
import math

import numpy as np
import jax
import jax.numpy as jnp
from jax import lax
from jax.experimental import pallas as pl
from jax.experimental.pallas import tpu as pltpu

F32 = jnp.float32
BF = jnp.bfloat16

CHUNK = 64
HEAD_DIM = 64
A_HEADS = 8
A_PREV_CHUNKS = 8
A_REL_CLIP = 128
B_Q_HEADS = 8
B_KV_HEADS = 2
B_PREV_CHUNKS = 2
T5_BUCKETS = 32
T5_MAX_DIST = 128
C_HEADS = 16
C_Q_LORA = 384
C_KV_LORA = 256
C_NOPE = 64
C_ROPE = 32
C_V = 64
ROPE_BASE = 10000.0
EPS = 1e-6
NEG = -1e30
A_W = A_HEADS * HEAD_DIM
BQ_W = B_Q_HEADS * HEAD_DIM
BKV_W = B_KV_HEADS * HEAD_DIM

LANES = 128
VMEM_LIMIT = 56 * 1024 * 1024

ROW_TILE = 1024
FF_CHUNK = 256
BAND_TQ = 256
BAND_AHEAD = 2
LOG2E = math.log2(math.e)
MLA_T = 1024
MLA_SK = 256
MLA_QW = 256
MLA_HU = 8
MLA_AHEAD = 5
C_VT_ROWS = 80
C_IN_EXT = 768
C_HEAD_W = 128


def _cparams(n_axes, vmem=VMEM_LIMIT):
    return pltpu.CompilerParams(dimension_semantics=("arbitrary",) * n_axes,
                                vmem_limit_bytes=vmem)


def _const_spec(shape):
    n = len(shape)
    return pl.BlockSpec(shape, lambda *_: (0,) * n)


def _rms(x):
    return x * lax.rsqrt(jnp.mean(x * x, axis=-1, keepdims=True) + EPS)


def _norm_mod(x, g, shift, scale):
    return (_rms(x) * g) * (1.0 + scale) + shift


def _mod_spec(G, blk_off, k, D):
    return pl.BlockSpec((G, 1, D), lambda i, j: (blk_off + i, 0, k))


def _adaln_body(c_ref, w_ref, b_ref, o_ref):
    c = c_ref[...]
    s = (c * jax.nn.sigmoid(c)).astype(BF)
    o_ref[0] = jnp.dot(s, w_ref[0].astype(BF), preferred_element_type=F32) + b_ref[0]


def _adaln(c_all, w_ada, b_ada):
    L, D, N = w_ada.shape
    R = c_all.shape[0]
    tn = 1024
    return pl.pallas_call(
        _adaln_body,
        grid=(L, N // tn),
        in_specs=[pl.BlockSpec((R, D), lambda l, j: (0, 0)),
                  pl.BlockSpec((1, D, tn), lambda l, j: (l, 0, j)),
                  pl.BlockSpec((1, 1, tn), lambda l, j: (l, 0, j))],
        out_specs=pl.BlockSpec((1, R, tn), lambda l, j: (l, 0, j)),
        out_shape=jax.ShapeDtypeStruct((L, R, N), F32),
        compiler_params=_cparams(2),
        name="adaln",
    )(c_all, w_ada, b_ada.reshape(L, 1, N))


def _ffn_body(*refs, n_attn, final):
    attn_refs, wo_refs = refs[:n_attn], refs[n_attn:2 * n_attn]
    refs = refs[2 * n_attn:]
    if n_attn:
        ga_ref, refs = refs[0], refs[1:]
    x_ref, sh_ref, sc_ref, gt_ref, g_ref, wg_ref, wu_ref, wd_ref = refs[:8]
    if final:
        fg_ref, o_ref, a_ref = refs[8:]
    else:
        o_ref, a_ref = refs[8:]
    x = x_ref[...]
    G, R, D = x.shape
    F = wg_ref.shape[1]
    if n_attn:
        mix = None
        for at_ref, wo_ref in zip(attn_refs, wo_refs):
            y = jnp.dot(at_ref[...].reshape(G * R, at_ref.shape[2]), wo_ref[...], preferred_element_type=F32)
            mix = y if mix is None else mix + y
        x = x + ga_ref[...] * mix.reshape(G, R, D)
    hb = _norm_mod(x, g_ref[...], sh_ref[...], sc_ref[...]).reshape(G * R, D).astype(BF)
    for c in range(F // FF_CHUNK):
        lo, hi = c * FF_CHUNK, (c + 1) * FF_CHUNK
        g = jnp.dot(hb, wg_ref[:, lo:hi], preferred_element_type=F32)
        u = jnp.dot(hb, wu_ref[:, lo:hi], preferred_element_type=F32)
        a_ref[:, lo:hi] = (g * jax.nn.sigmoid(g) * u).astype(BF)
    ff = jnp.dot(a_ref[...], wd_ref[...], preferred_element_type=F32)
    y = x + (0.5 * gt_ref[...]) * ff.reshape(G, R, D)
    if final:
        y = _rms(y) * fg_ref[...]
    o_ref[...] = y


def _ffn(x, mods, blk_off, kbase, g, weights, layer, which, G, R, final_g=None, attn=None):
    nseq, T, D = x.shape
    wg, wu, wd = weights
    F = wg.shape[-1]
    final = final_g is not None
    row = lambda W: pl.BlockSpec((G, R, W), lambda i, j: (i, j, 0))
    stacked = lambda w: pl.BlockSpec((None, None) + w.shape[2:], lambda i, j: (layer, which, 0, 0),
                                     pipeline_mode=pl.Buffered(1))
    in_specs, args = [], []
    n_attn = 0
    if attn is not None:
        os_, ws = attn
        n_attn = len(os_)
        in_specs += [row(o.shape[2]) for o in os_] + [_const_spec(w.shape) for w in ws]
        in_specs.append(_mod_spec(G, blk_off, 5, D))
        args += [*os_, *ws, mods]
    in_specs += [row(D), _mod_spec(G, blk_off, kbase, D), _mod_spec(G, blk_off, kbase + 1, D),
                 _mod_spec(G, blk_off, kbase + 2, D), _const_spec((1, D)), stacked(wg), stacked(wu), stacked(wd)]
    args += [x, mods, mods, mods, g.reshape(1, D), wg, wu, wd]
    if final:
        in_specs.append(_const_spec((1, D)))
        args.append(final_g.reshape(1, D))
    return pl.pallas_call(
        functools.partial(_ffn_body, n_attn=n_attn, final=final),
        grid=(nseq // G, T // R),
        in_specs=in_specs,
        out_specs=pl.BlockSpec((G, R, D), lambda i, j: (i, j, 0)),
        out_shape=jax.ShapeDtypeStruct((nseq, T, D), F32),
        scratch_shapes=[pltpu.VMEM((G * R, F), BF)],
        compiler_params=_cparams(2),
        name="ffn",
    )(*args)


def _dup_halves(k):
    rolled = pltpu.roll(k, HEAD_DIM, 1)
    lo = lax.broadcasted_iota(jnp.int32, k.shape, 1) < HEAD_DIM
    return jnp.concatenate([jnp.where(lo, k, rolled), jnp.where(lo, rolled, k)], axis=1)


def _proj_ab_body(x_ref, sh_ref, sc_ref, g_ref, w_ref,
                  qa_ref, ka_ref, va_ref, qb_ref, kb_ref, vb_ref,
                  ka32_ref, va32_ref, kb32_ref, vb32_ref, *, ta, tb):
    x = x_ref[...]
    G, R, D = x.shape
    hb = _norm_mod(x, g_ref[...], sh_ref[...], sc_ref[...]).reshape(G * R, D).astype(BF)
    res = jnp.dot(hb, w_ref[...], preferred_element_type=F32)
    o_ka, o_va, o_qb, o_kb, o_vb = A_W, 2 * A_W, 3 * A_W, 3 * A_W + BQ_W, 3 * A_W + BQ_W + BKV_W
    qscale = HEAD_DIM ** -0.5 * LOG2E
    qa_ref[...] = (res[:, :o_ka] * qscale).astype(BF).reshape(G, R, A_W)
    ka_ref[...] = res[:, o_ka:o_va].astype(BF).reshape(G, R, A_W)
    va_ref[...] = res[:, o_va:o_qb].astype(BF).reshape(G, R, A_W)
    qb_ref[...] = (res[:, o_qb:o_kb] * qscale).astype(BF).reshape(G, R, BQ_W)
    kb = res[:, o_kb:o_vb]
    vb = res[:, o_vb:o_vb + BKV_W]
    kb_ref[...] = _dup_halves(kb).astype(BF).reshape(G, R, 2 * BKV_W)
    vb_ref[...] = _dup_halves(vb).astype(BF).reshape(G, R, 2 * BKV_W)

    @pl.when(pl.program_id(1) == pl.num_programs(1) - 1)
    def _():
        ka32_ref[...] = res[:, o_ka:o_va].reshape(G, R, A_W)[:, R - ta:, :]
        va32_ref[...] = res[:, o_va:o_qb].reshape(G, R, A_W)[:, R - ta:, :]
        kb32_ref[...] = kb.reshape(G, R, BKV_W)[:, R - tb:, :]
        vb32_ref[...] = vb.reshape(G, R, BKV_W)[:, R - tb:, :]


def _proj_ab(x, mods, blk_off, g, w, G, R, ta, tb):
    nseq, T, D = x.shape
    row = lambda W: pl.BlockSpec((G, R, W), lambda i, j: (i, j, 0))
    tail = lambda t, W: pl.BlockSpec((G, t, W), lambda i, j: (i, 0, 0))
    bshape = lambda W: jax.ShapeDtypeStruct((nseq, T, W), BF)
    return pl.pallas_call(
        functools.partial(_proj_ab_body, ta=ta, tb=tb),
        grid=(nseq // G, T // R),
        in_specs=[row(D), _mod_spec(G, blk_off, 3, D), _mod_spec(G, blk_off, 4, D),
                  _const_spec((1, D)), _const_spec(w.shape)],
        out_specs=[row(A_W), row(A_W), row(A_W), row(BQ_W), row(2 * BKV_W), row(2 * BKV_W),
                   tail(ta, A_W), tail(ta, A_W), tail(tb, BKV_W), tail(tb, BKV_W)],
        out_shape=[bshape(A_W), bshape(A_W), bshape(A_W), bshape(BQ_W), bshape(2 * BKV_W), bshape(2 * BKV_W),
                   jax.ShapeDtypeStruct((nseq, ta, A_W), F32), jax.ShapeDtypeStruct((nseq, ta, A_W), F32),
                   jax.ShapeDtypeStruct((nseq, tb, BKV_W), F32), jax.ShapeDtypeStruct((nseq, tb, BKV_W), F32)],
        compiler_params=_cparams(2),
        name="proj_ab",
    )(x, mods, mods, g.reshape(1, D), w)


def _t5_bucket_np(rel):
    nb = T5_BUCKETS // 2
    ret = np.where(rel > 0, nb, 0)
    n = np.abs(rel)
    max_exact = nb // 2
    ratio = np.maximum(n, 1).astype(np.float32) / np.float32(max_exact)
    large = max_exact + (np.log(ratio).astype(np.float32) / np.float32(math.log(T5_MAX_DIST / max_exact))
                         * np.float32(nb - max_exact)).astype(np.int32)
    large = np.minimum(large, nb - 1)
    return ret + np.where(n < max_exact, n, large)


def _bias_body(thi_ref, tlo_ref, idx_ref, o_ref, *, Tq, span, prev_chunks, band_chunks):
    E = thi_ref.shape[1]
    L = idx_ref.shape[1]
    onehot = (lax.broadcasted_iota(jnp.int32, (E, L), 0) == idx_ref[...]).astype(BF)
    t = (jnp.dot(thi_ref[...], onehot, preferred_element_type=F32)
         + jnp.dot(tlo_ref[...], onehot, preferred_element_type=F32)) * LOG2E
    if band_chunks is not None:
        qc = lax.broadcasted_iota(jnp.int32, (Tq, span), 0) // CHUNK
        kc = lax.broadcasted_iota(jnp.int32, (Tq, span), 1) // CHUNK - prev_chunks
        valid = (kc <= qc) & (kc >= qc - band_chunks)
    for h in range(o_ref.shape[0]):
        x = jnp.broadcast_to(t[h:h + 1, :], (Tq, L))
        y = pltpu.roll(x, L - Tq + 1, 1, stride=1, stride_axis=0)[:, :span]
        if band_chunks is not None:
            y = jnp.where(valid, y, NEG)
        o_ref[h] = y


def _bias_tile(table, kind, Tq, span, n_prev_rows, band_chunks):
    H = table.shape[1]
    L = -(-(Tq + span - 1) // LANES) * LANES
    rel = np.arange(L) - (Tq - 1) - n_prev_rows
    if kind == "clip":
        idx = np.clip(rel, -A_REL_CLIP, A_REL_CLIP) + A_REL_CLIP
    else:
        idx = _t5_bucket_np(rel)
    E = -(-table.shape[0] // LANES) * LANES
    tt = jnp.zeros((H, E), F32).at[:, :table.shape[0]].set(table.T.astype(F32))
    thi = tt.astype(BF)
    tlo = (tt - thi.astype(F32)).astype(BF)
    return pl.pallas_call(
        functools.partial(_bias_body, Tq=Tq, span=span, prev_chunks=n_prev_rows // CHUNK,
                          band_chunks=band_chunks),
        out_shape=jax.ShapeDtypeStruct((H, Tq, span), F32),
        compiler_params=pltpu.CompilerParams(vmem_limit_bytes=VMEM_LIMIT),
        name="bias_tile",
    )(thi, tlo, jnp.asarray(idx.reshape(1, L), jnp.int32))


def _band_body(*refs, n_parts, n_prev_rows, pair_map, has_sink, Tq, windows):
    q_ref = refs[0]
    k_refs = refs[1:1 + n_parts]
    v_refs = refs[1 + n_parts:1 + 2 * n_parts]
    bias_ref = refs[1 + 2 * n_parts]
    sink_ref = refs[2 + 2 * n_parts] if has_sink else None
    o_ref = refs[-1]
    i = pl.program_id(1)
    k = jnp.concatenate([r[0].astype(BF) for r in k_refs], axis=0)
    v = jnp.concatenate([r[0].astype(BF) for r in v_refs], axis=0)
    q = q_ref[0]
    span = k.shape[0]
    n_heads = 2 * (q.shape[1] // LANES)
    lane_lo = lax.broadcasted_iota(jnp.int32, (1, LANES), 1) < HEAD_DIM

    items = [(w, h) for w in range(len(windows)) for h in range(n_heads)]

    def sweep(mask_start):
        def scores(w, h):
            qs, qr, ks, kr = windows[w]
            p = h // 2
            qp = q[qs:qs + qr, p * LANES:(p + 1) * LANES]
            qm = jnp.where(lane_lo if h % 2 == 0 else jnp.logical_not(lane_lo), qp, jnp.zeros_like(qp))
            kp = k[ks:ks + kr, pair_map[p] * LANES:(pair_map[p] + 1) * LANES]
            s = lax.dot_general(qm, kp, (((1,), (1,)), ((), ())), preferred_element_type=F32) + bias_ref[h]
            if mask_start and ks < n_prev_rows:
                s = jnp.where(lax.broadcasted_iota(jnp.int32, (1, kr), 1) >= (n_prev_rows - ks - i * Tq), s, NEG)
            return s

        pending = [scores(*it) for it in items[:BAND_AHEAD]]
        halves = []
        for n, (w, h) in enumerate(items):
            qs, qr, ks, kr = windows[w]
            s = pending.pop(0)
            if n + BAND_AHEAD < len(items):
                pending.append(scores(*items[n + BAND_AHEAD]))
            p = h // 2
            vp = v[ks:ks + kr, pair_map[p] * LANES:(pair_map[p] + 1) * LANES]
            m = jnp.max(s, axis=-1, keepdims=True)
            if has_sink:
                sink = sink_ref[h] * LOG2E
                m = jnp.maximum(m, sink)
            e = jnp.exp2(s - m)
            den = jnp.sum(e, axis=-1, keepdims=True)
            if has_sink:
                den = den + jnp.exp2(sink - m)
            o = jnp.dot(e.astype(BF), vp, preferred_element_type=F32)
            halves.append(o * (1.0 / den))
            if h % 2 == 1:
                o_ref[0, qs:qs + qr, p * LANES:(p + 1) * LANES] = jnp.where(lane_lo, halves[0], halves[1]).astype(BF)
                halves = []

    if n_prev_rows:
        pl.when(i * Tq < n_prev_rows)(lambda: sweep(True))
        pl.when(i * Tq >= n_prev_rows)(lambda: sweep(False))
    else:
        sweep(False)


def _band_attn(q, kparts, vparts, bias, sinks, *, Tq, n_prev_rows, pair_map, windows=None):
    nseq, T, W = q.shape
    if windows is None:
        windows = ((0, Tq, 0, sum(p[1] for p in kparts)),)

    def part_spec(arr, rows, off):
        if off is None:
            return pl.BlockSpec((1, rows, arr.shape[2]), lambda b, i: (b, 0, 0))
        return pl.BlockSpec((1, rows, arr.shape[2]), lambda b, i: (b, jnp.maximum(i + off, 0), 0))

    in_specs = [pl.BlockSpec((1, Tq, W), lambda b, i: (b, i, 0))]
    in_specs += [part_spec(*p) for p in kparts] + [part_spec(*p) for p in vparts]
    in_specs.append(_const_spec(bias.shape))
    args = [q] + [p[0] for p in kparts] + [p[0] for p in vparts] + [bias]
    if sinks is not None:
        in_specs.append(pl.BlockSpec(memory_space=pltpu.SMEM))
        args.append(sinks.astype(F32))
    return pl.pallas_call(
        functools.partial(_band_body, n_parts=len(kparts), n_prev_rows=n_prev_rows,
                          pair_map=pair_map, has_sink=sinks is not None, Tq=Tq, windows=tuple(windows)),
        grid=(nseq, T // Tq),
        in_specs=in_specs,
        out_specs=pl.BlockSpec((1, Tq, W), lambda b, i: (b, i, 0)),
        out_shape=jax.ShapeDtypeStruct((nseq, T, W), BF),
        compiler_params=_cparams(2),
        name="band_attn",
    )(*args)


def _proj_c_body(x_ref, sh_ref, sc_ref, g_ref, win_ref, qn_ref, kvn_ref, wqb_ref, cos_ref, sin_ref,
                 cost_ref, sint_ref, q_ref, kv32_ref, kr32_ref, *, q_transposed):
    x = x_ref[...]
    G, R, D = x.shape
    hb = _norm_mod(x, g_ref[...], sh_ref[...], sc_ref[...]).reshape(G * R, D).astype(BF)
    res = jnp.dot(hb, win_ref[...], preferred_element_type=F32)
    q_lat = res[:, :C_Q_LORA]
    kv_lat = res[:, C_Q_LORA:C_Q_LORA + C_KV_LORA]
    krg = res[:, C_Q_LORA + C_KV_LORA:]
    cos = cos_ref[...]
    sin = sin_ref[...]
    qn = (_rms(q_lat) * qn_ref[...]).astype(BF)
    scale = (C_NOPE + C_ROPE) ** -0.5 * math.log2(math.e)
    nh = q_ref.shape[1]
    if q_transposed:
        qt = lax.dot_general(wqb_ref[...], qn, (((1,), (1,)), ((), ())), preferred_element_type=F32)
        cost, sint = cost_ref[...], sint_ref[...]
        for h in range(nh):
            blk = qt[h * C_HEAD_W:(h + 1) * C_HEAD_W]
            roped = blk[C_NOPE:C_NOPE + C_ROPE] * cost + blk[C_NOPE + C_ROPE:] * sint
            q_ref[0, h] = (jnp.concatenate([blk[:C_NOPE], roped, jnp.zeros_like(roped)], axis=0) * scale).astype(BF)
    else:
        qr = jnp.dot(qn, wqb_ref[...], preferred_element_type=F32)
        qrot = pltpu.roll(qr, qr.shape[1] - C_ROPE, 1)
        for h in range(nh):
            a = qr[:, h * C_HEAD_W:(h + 1) * C_HEAD_W].reshape(G, R, C_HEAD_W)
            b = qrot[:, h * C_HEAD_W:(h + 1) * C_HEAD_W].reshape(G, R, C_HEAD_W)
            q_ref[:, h] = ((a * cos + b * sin) * scale).astype(BF)
    kv32_ref[...] = (_rms(kv_lat) * kvn_ref[...]).reshape(G, R, C_KV_LORA)
    krot = pltpu.roll(krg, C_HEAD_W - C_ROPE, 1)
    krf = krg.reshape(G, R, C_HEAD_W) * cos + krot.reshape(G, R, C_HEAD_W) * sin
    kr32_ref[...] = krf[:, :, C_NOPE:C_NOPE + C_ROPE]


def _proj_c(x, mods, blk_off, g, win, qn_g, kvn_g, wqb, tables, G, R, q_transposed):
    nseq, T, D = x.shape
    nh = C_HEADS
    cos_t, sin_t, cost_t, sint_t = tables
    tab = pl.BlockSpec((1, R, C_HEAD_W), lambda i, j: (0, j, 0))
    tabt = pl.BlockSpec((C_ROPE, R), lambda i, j: (0, j))
    if q_transposed:
        assert G == 1
        wqb = wqb.T
        q_spec = pl.BlockSpec((1, nh, C_HEAD_W, R), lambda i, j: (i, 0, 0, j))
        q_shape = jax.ShapeDtypeStruct((nseq, nh, C_HEAD_W, T), BF)
    else:
        q_spec = pl.BlockSpec((G, nh, R, C_HEAD_W), lambda i, j: (i, 0, j, 0))
        q_shape = jax.ShapeDtypeStruct((nseq, nh, T, C_HEAD_W), BF)
    return pl.pallas_call(
        functools.partial(_proj_c_body, q_transposed=q_transposed),
        grid=(nseq // G, T // R),
        in_specs=[pl.BlockSpec((G, R, D), lambda i, j: (i, j, 0)),
                  _mod_spec(G, blk_off, 3, D), _mod_spec(G, blk_off, 4, D), _const_spec((1, D)),
                  _const_spec(win.shape), _const_spec((1, C_Q_LORA)), _const_spec((1, C_KV_LORA)),
                  _const_spec(wqb.shape), tab, tab, tabt, tabt],
        out_specs=[q_spec,
                   pl.BlockSpec((G, R, C_KV_LORA), lambda i, j: (i, j, 0)),
                   pl.BlockSpec((G, R, C_ROPE), lambda i, j: (i, j, 0))],
        out_shape=[q_shape,
                   jax.ShapeDtypeStruct((nseq, T, C_KV_LORA), F32),
                   jax.ShapeDtypeStruct((nseq, T, C_ROPE), F32)],
        compiler_params=_cparams(2),
        name="proj_c",
    )(x, mods, mods, g.reshape(1, D), win, qn_g.reshape(1, -1), kvn_g.reshape(1, -1), wqb,
      cos_t, sin_t, cost_t, sint_t)


def _kvexp_body(kv_ref, kr_ref, wk_ref, wvt_ref, place_ref, ones_ref, k_ref, vt_ref):
    kv = kv_ref[0].astype(BF)
    krf = jnp.dot(kr_ref[0].astype(BF), place_ref[...], preferred_element_type=F32)
    kx = jnp.dot(kv, wk_ref[...], preferred_element_type=F32)
    for h in range(k_ref.shape[1]):
        k_ref[0, h] = (kx[:, h * C_HEAD_W:(h + 1) * C_HEAD_W] + krf).astype(BF)
    vt = lax.dot_general(wvt_ref[...], kv, (((1,), (1,)), ((), ())), preferred_element_type=F32)
    vt_ref[0] = (vt + ones_ref[...]).astype(BF)


def _kv_expand(kv, kr, wk, wvt, R):
    nseq, T, _ = kv.shape
    nh = wk.shape[1] // C_HEAD_W
    place = np.zeros((C_ROPE, C_HEAD_W), np.float32)
    place[np.arange(C_ROPE), C_NOPE + np.arange(C_ROPE)] = 1.0
    ones = np.zeros((nh, C_VT_ROWS, 1), np.float32)
    ones[:, C_V] = 1.0
    ones = ones.reshape(nh * C_VT_ROWS, 1)
    return pl.pallas_call(
        _kvexp_body,
        grid=(nseq, T // R),
        in_specs=[pl.BlockSpec((1, R, C_KV_LORA), lambda b, j: (b, j, 0)),
                  pl.BlockSpec((1, R, C_ROPE), lambda b, j: (b, j, 0)),
                  _const_spec(wk.shape), _const_spec(wvt.shape), _const_spec(place.shape),
                  _const_spec(ones.shape)],
        out_specs=[pl.BlockSpec((1, nh, R, C_HEAD_W), lambda b, j: (b, 0, j, 0)),
                   pl.BlockSpec((1, nh * C_VT_ROWS, R), lambda b, j: (b, 0, j))],
        out_shape=[jax.ShapeDtypeStruct((nseq, nh, T, C_HEAD_W), BF),
                   jax.ShapeDtypeStruct((nseq, nh * C_VT_ROWS, T), BF)],
        compiler_params=_cparams(2),
        name="kv_expand",
    )(kv, kr, wk, wvt, jnp.asarray(place, BF), jnp.asarray(ones))


def _flash_body(qi_ref, kj_ref, q_ref, k_ref, vt_ref, o_ref, acc_ref, m_ref,
                *, Tq, Tk, SK, QW, causal, valid_len):
    t = pl.program_id(1)
    i = qi_ref[t]
    j = kj_ref[t]
    nh = q_ref.shape[1]
    n_stripes = Tq // QW

    @pl.when(j == 0)
    def _():
        m_ref[...] = jnp.full(m_ref.shape, NEG, F32)
        acc_ref[...] = jnp.zeros(acc_ref.shape, F32)

    def sweep(diag):
        tiles = []
        n_sub = Tk // SK if valid_len is None else -(-valid_len // SK)
        for r in range(n_stripes):
            for c in range(n_sub):
                if diag:
                    if c * SK >= (r + 1) * QW:
                        continue
                    mask = "chunk" if (c + 1) * SK > r * QW + CHUNK else None
                else:
                    mask = "len" if valid_len is not None and (c + 1) * SK > valid_len else None
                tiles.append((c, r, mask))

        def head_group(g, carry):
            heads = [g * MLA_HU + u for u in range(MLA_HU)]
            stream = [(u, c, r, mask) for u in range(MLA_HU) for (c, r, mask) in tiles]
            state = {(u, r): (m_ref[heads[u], :, r * QW:(r + 1) * QW], acc_ref[heads[u], :, r * QW:(r + 1) * QW])
                     for u in range(MLA_HU) for r in range(n_stripes)}

            def scores(u, c, r, mask):
                s = jnp.dot(k_ref[0, heads[u], c * SK:(c + 1) * SK, :], q_ref[0, heads[u], :, r * QW:(r + 1) * QW],
                            preferred_element_type=F32)
                kpos = lax.broadcasted_iota(jnp.int32, (SK, 1), 0) + c * SK
                if mask == "chunk":
                    qpos = lax.broadcasted_iota(jnp.int32, (1, QW), 1) + r * QW
                    s = jnp.where(kpos // CHUNK <= qpos // CHUNK, s, NEG)
                elif mask == "len":
                    s = jnp.where(kpos < valid_len, s, NEG)
                return s

            pending = [scores(*tl) for tl in stream[:MLA_AHEAD]]
            for n, (u, c, r, _) in enumerate(stream):
                s = pending.pop(0)
                if n + MLA_AHEAD < len(stream):
                    pending.append(scores(*stream[n + MLA_AHEAD]))
                m, acc = state[u, r]
                m_new = jnp.maximum(m, jnp.max(s, axis=0, keepdims=True))
                p = jnp.exp2(s - m_new).astype(BF)
                acc = jnp.exp2(m - m_new) * acc + jnp.dot(vt_ref[0, heads[u], :, c * SK:(c + 1) * SK], p,
                                                          preferred_element_type=F32)
                state[u, r] = (m_new, acc)
            for (u, r), (m, acc) in state.items():
                m_ref[heads[u], :, r * QW:(r + 1) * QW] = m
                acc_ref[heads[u], :, r * QW:(r + 1) * QW] = acc
            return carry
        lax.fori_loop(0, nh // MLA_HU, head_group, 0)

    if causal:
        pl.when(j == i)(lambda: sweep(True))
        pl.when(j != i)(lambda: sweep(False))
        last = i
    else:
        sweep(False)
        last = 0

    @pl.when(j == last)
    def _():
        for p in range(nh // 2):
            halves = [acc_ref[h, :C_V, :] * (1.0 / acc_ref[h, C_V:C_V + 1, :]) for h in (2 * p, 2 * p + 1)]
            o_ref[0, :, p * LANES:(p + 1) * LANES] = jnp.concatenate(halves, axis=0).T.astype(BF)


def _mla_attn(q, k, vt, *, Tq, Tk, SK, QW, causal, valid_len=None):
    nseq, nh, _, T = q.shape
    S = k.shape[2]
    nq, nk = T // Tq, S // Tk
    assert Tq % QW == 0 and Tk % SK == 0 and SK % CHUNK == 0
    if causal:
        assert Tq == Tk and QW % CHUNK == 0 and valid_len is None
        pairs = [(i, j) for i in range(nq) for j in range(i + 1)]
    else:
        assert nk == 1
        pairs = [(i, 0) for i in range(nq)]
    qi = jnp.asarray([p[0] for p in pairs], jnp.int32)
    kj = jnp.asarray([p[1] for p in pairs], jnp.int32)
    grid_spec = pltpu.PrefetchScalarGridSpec(
        num_scalar_prefetch=2,
        grid=(nseq, len(pairs)),
        in_specs=[pl.BlockSpec((1, nh, C_HEAD_W, Tq), lambda b, t, qi, kj: (b, 0, 0, qi[t])),
                  pl.BlockSpec((1, nh, Tk, C_HEAD_W), lambda b, t, qi, kj: (b, 0, kj[t], 0)),
                  pl.BlockSpec((1, nh, C_VT_ROWS, Tk), lambda b, t, qi, kj: (b, 0, 0, kj[t]))],
        out_specs=pl.BlockSpec((1, Tq, nh * C_V), lambda b, t, qi, kj: (b, qi[t], 0)),
        scratch_shapes=[pltpu.VMEM((nh, C_VT_ROWS, Tq), F32), pltpu.VMEM((nh, 1, Tq), F32)],
    )
    return pl.pallas_call(
        functools.partial(_flash_body, Tq=Tq, Tk=Tk, SK=SK, QW=QW, causal=causal, valid_len=valid_len),
        grid_spec=grid_spec,
        out_shape=jax.ShapeDtypeStruct((nseq, T, nh * C_V), BF),
        compiler_params=_cparams(2),
        name="mla_attn",
    )(qi, kj, q, k, vt)


def _rope_tables(pos):
    half = C_ROPE // 2
    inv = (np.float32(ROPE_BASE) ** (-np.arange(half, dtype=np.float32) / np.float32(half))).astype(np.float32)
    ang = (pos.astype(np.float32)[:, None] * inv[None, :]).astype(np.float32).astype(np.float64)
    n = pos.shape[0]
    cos = np.zeros((1, n, C_HEAD_W), np.float32)
    sin = np.zeros((1, n, C_HEAD_W), np.float32)
    cos[0, :, :C_NOPE] = 1.0
    cos[0, :, C_NOPE:C_NOPE + C_ROPE] = np.concatenate([np.cos(ang), np.cos(ang)], axis=1)
    sin[0, :, C_NOPE:C_NOPE + C_ROPE] = np.concatenate([np.sin(ang), np.sin(ang)], axis=1)
    cos32, sin32 = cos[0, :, C_NOPE:C_NOPE + C_ROPE].T, sin[0, :, C_NOPE:C_NOPE + C_ROPE].T
    return jnp.asarray(cos), jnp.asarray(sin), jnp.asarray(cos32), jnp.asarray(sin32)


def _rot_half_cols(w):
    half = w.shape[-1] // 2
    return jnp.concatenate([-w[..., half:], w[..., :half]], axis=-1)


def kernel(x_prompt, x_sample, c_prompt, c_sample, cache_a_k, cache_a_v, cache_b_k, cache_b_v, cache_c_kv, cache_c_kr, w_ada, b_ada, norm_g, final_norm_g, ffn_w_gate, ffn_w_up, ffn_w_down, w_in_ab, w_out_ab, rel_bias_a, t5_bias, sinks_b, w_in_c, c_q_norm_g, c_kv_norm_g, w_qb, w_kvb, w_out_c):
    nb, S, D = x_prompt.shape
    ns, TS, _ = x_sample.shape
    la_c, lb_c = cache_a_k.shape[2], cache_b_k.shape[2]
    past = cache_c_kv.shape[2]
    la_p, lb_p = min(A_PREV_CHUNKS * CHUNK, S), min(B_PREV_CHUNKS * CHUNK, S)
    assert la_p <= ROW_TILE and S % MLA_T == 0 and S % ROW_TILE == 0 and TS % 8 == 0

    n_cond = ns + nb
    n_cond_pad = -(-n_cond // 8) * 8
    c_all = jnp.zeros((n_cond_pad, D), F32).at[:ns].set(c_sample).at[ns:n_cond].set(c_prompt)
    mods_all = _adaln(c_all, w_ada, b_ada)
    groups = {
        "p": dict(x=x_prompt, blk_off=ns, G=1, R=ROW_TILE),
        "s": dict(x=x_sample, blk_off=0, G=ns, R=TS),
    }

    ffn_w = tuple(w.astype(BF) for w in (ffn_w_gate, ffn_w_up, ffn_w_down))
    w_ab = w_in_ab[0].astype(BF)
    w_oa, w_ob = w_out_ab[0, :A_W].astype(BF), w_out_ab[0, A_W:].astype(BF)
    pair_a = tuple(range(A_HEADS // 2))
    pair_b = tuple(p // (B_Q_HEADS // B_KV_HEADS // 2) for p in range(B_Q_HEADS // 2))
    a_prev_rows = 2 * BAND_TQ
    b_prev_rows = BAND_TQ
    assert a_prev_rows == A_PREV_CHUNKS * CHUNK and b_prev_rows >= B_PREV_CHUNKS * CHUNK
    bias_ap = _bias_tile(rel_bias_a[0], "clip", BAND_TQ, a_prev_rows + BAND_TQ, a_prev_rows, A_PREV_CHUNKS)
    b_half = BAND_TQ // 2
    b_back = B_PREV_CHUNKS * CHUNK
    assert b_half % CHUNK == 0 and b_back <= b_prev_rows
    windows_b = tuple((qs, b_half, b_prev_rows + qs - b_back, b_back + b_half) for qs in (0, b_half))
    bias_bp = _bias_tile(t5_bias, "t5", b_half, b_back + b_half, b_back, B_PREV_CHUNKS)
    bias_as = _bias_tile(rel_bias_a[0], "clip", TS, la_c + TS, la_c, None)
    bias_bs = _bias_tile(t5_bias, "t5", TS, lb_c + TS, lb_c, None)
    dup = lambda c: jnp.repeat(c, 2, axis=2).reshape(c.shape[0], c.shape[1], 2 * BKV_W)
    cak = cache_a_k[0].reshape(ns, la_c, A_W)
    cav = cache_a_v[0].reshape(ns, la_c, A_W)
    cbk, cbv = dup(cache_b_k[0]), dup(cache_b_v[0])

    mods = mods_all[0].reshape(n_cond_pad, 1, -1)
    state0 = {}
    for name, gr in groups.items():
        x, off, G, R = gr["x"], gr["blk_off"] // gr["G"], gr["G"], gr["R"]
        x = _ffn(x, mods, off, 0, norm_g[0, 0], ffn_w, 0, 0, G, R)
        ta, tb = (la_p, lb_p) if name == "p" else (TS, TS)
        qa, ka, va, qb, kb, vb, ka32, va32, kb32, vb32 = _proj_ab(x, mods, off, norm_g[0, 1], w_ab, G, R, ta, tb)
        if name == "p":
            prev = lambda arr, n: [(arr, BAND_TQ, d - n) for d in range(n + 1)]
            oa = _band_attn(qa, prev(ka, 2), prev(va, 2), bias_ap, None,
                            Tq=BAND_TQ, n_prev_rows=a_prev_rows, pair_map=pair_a)
            ob = _band_attn(qb, prev(kb, 1), prev(vb, 1), bias_bp, sinks_b[0],
                            Tq=BAND_TQ, n_prev_rows=b_prev_rows, pair_map=pair_b, windows=windows_b)
        else:
            oa = _band_attn(qa, [(cak, la_c, None), (ka, TS, None)], [(cav, la_c, None), (va, TS, None)],
                            bias_as, None, Tq=TS, n_prev_rows=0, pair_map=pair_a)
            ob = _band_attn(qb, [(cbk, lb_c, None), (kb, TS, None)], [(cbv, lb_c, None), (vb, TS, None)],
                            bias_bs, sinks_b[0], Tq=TS, n_prev_rows=0, pair_map=pair_b)
        x = _ffn(x, mods, off, 6, norm_g[0, 2], ffn_w, 0, 1, G, R, attn=([oa, ob], [w_oa, w_ob]))
        gr["x"] = x
        n_out = x.shape[0]
        state0[name] = (ka32.reshape(1, n_out, ta, A_HEADS, HEAD_DIM), va32.reshape(1, n_out, ta, A_HEADS, HEAD_DIM),
                        kb32.reshape(1, n_out, tb, B_KV_HEADS, HEAD_DIM), vb32.reshape(1, n_out, tb, B_KV_HEADS, HEAD_DIM))

    hw = C_NOPE + C_ROPE
    w_in = w_in_c[0]
    w_kr = w_in[:, C_Q_LORA + C_KV_LORA:]
    win_ext = jnp.concatenate([w_in[:, :C_Q_LORA + C_KV_LORA], jnp.zeros((D, C_NOPE), F32),
                               w_kr, _rot_half_cols(w_kr)], axis=1).astype(BF)
    wq3 = w_qb[0].reshape(C_Q_LORA, C_HEADS, hw)
    wqb_ext = jnp.concatenate([wq3, _rot_half_cols(wq3[..., C_NOPE:])], axis=-1
                              ).reshape(C_Q_LORA, C_HEADS * C_HEAD_W).astype(BF)
    wkv3 = w_kvb[0].reshape(C_KV_LORA, C_HEADS, C_NOPE + C_V)
    wk_ext = jnp.concatenate([wkv3[..., :C_NOPE], jnp.zeros((C_KV_LORA, C_HEADS, C_HEAD_W - C_NOPE), F32)],
                             axis=-1).reshape(C_KV_LORA, C_HEADS * C_HEAD_W).astype(BF)
    wvt = jnp.concatenate([wkv3[..., C_NOPE:], jnp.zeros((C_KV_LORA, C_HEADS, C_VT_ROWS - C_V), F32)], axis=-1
                          ).reshape(C_KV_LORA, C_HEADS * C_VT_ROWS).T.astype(BF)
    w_oc = w_out_c[0].astype(BF)
    tables = {"p": _rope_tables(np.arange(S)), "s": _rope_tables(past + np.arange(TS))}

    mods = mods_all[1].reshape(n_cond_pad, 1, -1)
    state1 = {}
    for name, gr in groups.items():
        x, off, G, R = gr["x"], gr["blk_off"] // gr["G"], gr["G"], gr["R"]
        x = _ffn(x, mods, off, 0, norm_g[1, 0], ffn_w, 1, 0, G, R)
        q, kv32, kr32 = _proj_c(x, mods, off, norm_g[1, 1], win_ext, c_q_norm_g[0], c_kv_norm_g[0],
                                wqb_ext, tables[name], G, R, q_transposed=(name == "p"))
        if name == "p":
            k, vt = _kv_expand(kv32, kr32, wk_ext, wvt, ROW_TILE)
            o = _mla_attn(q, k, vt.reshape(nb, C_HEADS, C_VT_ROWS, S), Tq=MLA_T, Tk=MLA_T, SK=MLA_SK, QW=MLA_QW,
                          causal=True)
        else:
            n_keys = past + TS
            n_pad = -(-n_keys // LANES) * LANES
            kv_all = jnp.zeros((ns, n_pad, C_KV_LORA), F32).at[:, :past].set(cache_c_kv[0]).at[:, past:n_keys].set(kv32)
            kr_all = jnp.zeros((ns, n_pad, C_ROPE), F32).at[:, :past].set(cache_c_kr[0]).at[:, past:n_keys].set(kr32)
            k, vt = _kv_expand(kv_all, kr_all, wk_ext, wvt, n_pad)
            o = _mla_attn(jnp.swapaxes(q, 2, 3), k, vt.reshape(ns, C_HEADS, C_VT_ROWS, n_pad), Tq=TS, Tk=n_pad,
                          SK=LANES, QW=TS,
                          causal=False, valid_len=n_keys)
        x = _ffn(x, mods, off, 6, norm_g[1, 2], ffn_w, 1, 1, G, R, final_g=final_norm_g, attn=([o], [w_oc]))
        gr["x"] = x
        state1[name] = (kv32[None], kr32[None])

    return (groups["p"]["x"], groups["s"]["x"],
            *state0["p"], *state1["p"], *state0["s"], *state1["s"])
```

```python
import functools
import math

import numpy as np
import jax
import jax.numpy as jnp
from jax import lax
from jax.experimental import pallas as pl
from jax.experimental.pallas import tpu as pltpu

F32 = jnp.float32
BF = jnp.bfloat16

CHUNK = 64
HEAD_DIM = 64
A_HEADS = 8
A_PREV_CHUNKS = 8
A_REL_CLIP = 128
B_Q_HEADS = 8
B_KV_HEADS = 2
B_PREV_CHUNKS = 2
T5_BUCKETS = 32
T5_MAX_DIST = 128
C_HEADS = 16
C_Q_LORA = 384
C_KV_LORA = 256
C_NOPE = 64
C_ROPE = 32
C_V = 64
ROPE_BASE = 10000.0
EPS = 1e-6
NEG = -1e30
A_W = A_HEADS * HEAD_DIM
BQ_W = B_Q_HEADS * HEAD_DIM
BKV_W = B_KV_HEADS * HEAD_DIM

LANES = 128
VMEM_LIMIT = 56 * 1024 * 1024

ROW_TILE = 1024
FF_CHUNK = 256
BAND_TQ = 256
BAND_AHEAD = 2
LOG2E = math.log2(math.e)
MLA_T = 1024
MLA_SK = 256
MLA_QW = 256
MLA_HU = 8
MLA_AHEAD = 5
C_VT_ROWS = 80
C_IN_EXT = 768
C_HEAD_W = 128


def _cparams(n_axes, vmem=VMEM_LIMIT):
    return pltpu.CompilerParams(dimension_semantics=("arbitrary",) * n_axes,
                                vmem_limit_bytes=vmem)


def _const_spec(shape):
    n = len(shape)
    return pl.BlockSpec(shape, lambda *_: (0,) * n)


def _rms(x):
    return x * lax.rsqrt(jnp.mean(x * x, axis=-1, keepdims=True) + EPS)


def _norm_mod(x, g, shift, scale):
    return (_rms(x) * g) * (1.0 + scale) + shift


def _mod_spec(G, blk_off, k, D):
    return pl.BlockSpec((G, 1, D), lambda i, j: (blk_off + i, 0, k))


def _adaln_body(c_ref, w_ref, b_ref, o_ref):
    c = c_ref[...]
    s = (c * jax.nn.sigmoid(c)).astype(BF)
    o_ref[0] = jnp.dot(s, w_ref[0].astype(BF), preferred_element_type=F32) + b_ref[0]


def _adaln(c_all, w_ada, b_ada):
    L, D, N = w_ada.shape
    R = c_all.shape[0]
    tn = 1024
    return pl.pallas_call(
        _adaln_body,
        grid=(L, N // tn),
        in_specs=[pl.BlockSpec((R, D), lambda l, j: (0, 0)),
                  pl.BlockSpec((1, D, tn), lambda l, j: (l, 0, j)),
                  pl.BlockSpec((1, 1, tn), lambda l, j: (l, 0, j))],
        out_specs=pl.BlockSpec((1, R, tn), lambda l, j: (l, 0, j)),
        out_shape=jax.ShapeDtypeStruct((L, R, N), F32),
        compiler_params=_cparams(2),
        name="adaln",
    )(c_all, w_ada, b_ada.reshape(L, 1, N))


def _ffn_body(*refs, n_attn, final):
    attn_refs, wo_refs = refs[:n_attn], refs[n_attn:2 * n_attn]
    refs = refs[2 * n_attn:]
    if n_attn:
        ga_ref, refs = refs[0], refs[1:]
    x_ref, sh_ref, sc_ref, gt_ref, g_ref, wg_ref, wu_ref, wd_ref = refs[:8]
    if final:
        fg_ref, o_ref, a_ref = refs[8:]
    else:
        o_ref, a_ref = refs[8:]
    x = x_ref[...]
    G, R, D = x.shape
    F = wg_ref.shape[1]
    if n_attn:
        mix = None
        for at_ref, wo_ref in zip(attn_refs, wo_refs):
            y = jnp.dot(at_ref[...].reshape(G * R, at_ref.shape[2]), wo_ref[...], preferred_element_type=F32)
            mix = y if mix is None else mix + y
        x = x + ga_ref[...] * mix.reshape(G, R, D)
    hb = _norm_mod(x, g_ref[...], sh_ref[...], sc_ref[...]).reshape(G * R, D).astype(BF)
    for c in range(F // FF_CHUNK):
        lo, hi = c * FF_CHUNK, (c + 1) * FF_CHUNK
        g = jnp.dot(hb, wg_ref[:, lo:hi], preferred_element_type=F32)
        u = jnp.dot(hb, wu_ref[:, lo:hi], preferred_element_type=F32)
        a_ref[:, lo:hi] = (g * jax.nn.sigmoid(g) * u).astype(BF)
    ff = jnp.dot(a_ref[...], wd_ref[...], preferred_element_type=F32)
    y = x + (0.5 * gt_ref[...]) * ff.reshape(G, R, D)
    if final:
        y = _rms(y) * fg_ref[...]
    o_ref[...] = y


def _ffn(x, mods, blk_off, kbase, g, weights, layer, which, G, R, final_g=None, attn=None):
    nseq, T, D = x.shape
    wg, wu, wd = weights
    F = wg.shape[-1]
    final = final_g is not None
    row = lambda W: pl.BlockSpec((G, R, W), lambda i, j: (i, j, 0))
    stacked = lambda w: pl.BlockSpec((None, None) + w.shape[2:], lambda i, j: (layer, which, 0, 0),
                                     pipeline_mode=pl.Buffered(1))
    in_specs, args = [], []
    n_attn = 0
    if attn is not None:
        os_, ws = attn
        n_attn = len(os_)
        in_specs += [row(o.shape[2]) for o in os_] + [_const_spec(w.shape) for w in ws]
        in_specs.append(_mod_spec(G, blk_off, 5, D))
        args += [*os_, *ws, mods]
    in_specs += [row(D), _mod_spec(G, blk_off, kbase, D), _mod_spec(G, blk_off, kbase + 1, D),
                 _mod_spec(G, blk_off, kbase + 2, D), _const_spec((1, D)), stacked(wg), stacked(wu), stacked(wd)]
    args += [x, mods, mods, mods, g.reshape(1, D), wg, wu, wd]
    if final:
        in_specs.append(_const_spec((1, D)))
        args.append(final_g.reshape(1, D))
    return pl.pallas_call(
        functools.partial(_ffn_body, n_attn=n_attn, final=final),
        grid=(nseq // G, T // R),
        in_specs=in_specs,
        out_specs=pl.BlockSpec((G, R, D), lambda i, j: (i, j, 0)),
        out_shape=jax.ShapeDtypeStruct((nseq, T, D), F32),
        scratch_shapes=[pltpu.VMEM((G * R, F), BF)],
        compiler_params=_cparams(2),
        name="ffn",
    )(*args)


def _dup_halves(k):
    rolled = pltpu.roll(k, HEAD_DIM, 1)
    lo = lax.broadcasted_iota(jnp.int32, k.shape, 1) < HEAD_DIM
    return jnp.concatenate([jnp.where(lo, k, rolled), jnp.where(lo, rolled, k)], axis=1)


def _proj_ab_body(x_ref, sh_ref, sc_ref, g_ref, w_ref,
                  qa_ref, ka_ref, va_ref, qb_ref, kb_ref, vb_ref,
                  ka32_ref, va32_ref, kb32_ref, vb32_ref, *, ta, tb):
    x = x_ref[...]
    G, R, D = x.shape
    hb = _norm_mod(x, g_ref[...], sh_ref[...], sc_ref[...]).reshape(G * R, D).astype(BF)
    res = jnp.dot(hb, w_ref[...], preferred_element_type=F32)
    o_ka, o_va, o_qb, o_kb, o_vb = A_W, 2 * A_W, 3 * A_W, 3 * A_W + BQ_W, 3 * A_W + BQ_W + BKV_W
    qscale = HEAD_DIM ** -0.5 * LOG2E
    qa_ref[...] = (res[:, :o_ka] * qscale).astype(BF).reshape(G, R, A_W)
    ka_ref[...] = res[:, o_ka:o_va].astype(BF).reshape(G, R, A_W)
    va_ref[...] = res[:, o_va:o_qb].astype(BF).reshape(G, R, A_W)
    qb_ref[...] = (res[:, o_qb:o_kb] * qscale).astype(BF).reshape(G, R, BQ_W)
    kb = res[:, o_kb:o_vb]
    vb = res[:, o_vb:o_vb + BKV_W]
    kb_ref[...] = _dup_halves(kb).astype(BF).reshape(G, R, 2 * BKV_W)
    vb_ref[...] = _dup_halves(vb).astype(BF).reshape(G, R, 2 * BKV_W)

    @pl.when(pl.program_id(1) == pl.num_programs(1) - 1)
    def _():
        ka32_ref[...] = res[:, o_ka:o_va].reshape(G, R, A_W)[:, R - ta:, :]
        va32_ref[...] = res[:, o_va:o_qb].reshape(G, R, A_W)[:, R - ta:, :]
        kb32_ref[...] = kb.reshape(G, R, BKV_W)[:, R - tb:, :]
        vb32_ref[...] = vb.reshape(G, R, BKV_W)[:, R - tb:, :]


def _proj_ab(x, mods, blk_off, g, w, G, R, ta, tb):
    nseq, T, D = x.shape
    row = lambda W: pl.BlockSpec((G, R, W), lambda i, j: (i, j, 0))
    tail = lambda t, W: pl.BlockSpec((G, t, W), lambda i, j: (i, 0, 0))
    bshape = lambda W: jax.ShapeDtypeStruct((nseq, T, W), BF)
    return pl.pallas_call(
        functools.partial(_proj_ab_body, ta=ta, tb=tb),
        grid=(nseq // G, T // R),
        in_specs=[row(D), _mod_spec(G, blk_off, 3, D), _mod_spec(G, blk_off, 4, D),
                  _const_spec((1, D)), _const_spec(w.shape)],
        out_specs=[row(A_W), row(A_W), row(A_W), row(BQ_W), row(2 * BKV_W), row(2 * BKV_W),
                   tail(ta, A_W), tail(ta, A_W), tail(tb, BKV_W), tail(tb, BKV_W)],
        out_shape=[bshape(A_W), bshape(A_W), bshape(A_W), bshape(BQ_W), bshape(2 * BKV_W), bshape(2 * BKV_W),
                   jax.ShapeDtypeStruct((nseq, ta, A_W), F32), jax.ShapeDtypeStruct((nseq, ta, A_W), F32),
                   jax.ShapeDtypeStruct((nseq, tb, BKV_W), F32), jax.ShapeDtypeStruct((nseq, tb, BKV_W), F32)],
        compiler_params=_cparams(2),
        name="proj_ab",
    )(x, mods, mods, g.reshape(1, D), w)


def _t5_bucket_np(rel):
    nb = T5_BUCKETS // 2
    ret = np.where(rel > 0, nb, 0)
    n = np.abs(rel)
    max_exact = nb // 2
    ratio = np.maximum(n, 1).astype(np.float32) / np.float32(max_exact)
    large = max_exact + (np.log(ratio).astype(np.float32) / np.float32(math.log(T5_MAX_DIST / max_exact))
                         * np.float32(nb - max_exact)).astype(np.int32)
    large = np.minimum(large, nb - 1)
    return ret + np.where(n < max_exact, n, large)


def _bias_body(thi_ref, tlo_ref, idx_ref, o_ref, *, Tq, span, prev_chunks, band_chunks):
    E = thi_ref.shape[1]
    L = idx_ref.shape[1]
    onehot = (lax.broadcasted_iota(jnp.int32, (E, L), 0) == idx_ref[...]).astype(BF)
    t = (jnp.dot(thi_ref[...], onehot, preferred_element_type=F32)
         + jnp.dot(tlo_ref[...], onehot, preferred_element_type=F32)) * LOG2E
    if band_chunks is not None:
        qc = lax.broadcasted_iota(jnp.int32, (Tq, span), 0) // CHUNK
        kc = lax.broadcasted_iota(jnp.int32, (Tq, span), 1) // CHUNK - prev_chunks
        valid = (kc <= qc) & (kc >= qc - band_chunks)
    for h in range(o_ref.shape[0]):
        x = jnp.broadcast_to(t[h:h + 1, :], (Tq, L))
        y = pltpu.roll(x, L - Tq + 1, 1, stride=1, stride_axis=0)[:, :span]
        if band_chunks is not None:
            y = jnp.where(valid, y, NEG)
        o_ref[h] = y


def _bias_tile(table, kind, Tq, span, n_prev_rows, band_chunks):
    H = table.shape[1]
    L = -(-(Tq + span - 1) // LANES) * LANES
    rel = np.arange(L) - (Tq - 1) - n_prev_rows
    if kind == "clip":
        idx = np.clip(rel, -A_REL_CLIP, A_REL_CLIP) + A_REL_CLIP
    else:
        idx = _t5_bucket_np(rel)
    E = -(-table.shape[0] // LANES) * LANES
    tt = jnp.zeros((H, E), F32).at[:, :table.shape[0]].set(table.T.astype(F32))
    thi = tt.astype(BF)
    tlo = (tt - thi.astype(F32)).astype(BF)
    return pl.pallas_call(
        functools.partial(_bias_body, Tq=Tq, span=span, prev_chunks=n_prev_rows // CHUNK,
                          band_chunks=band_chunks),
        out_shape=jax.ShapeDtypeStruct((H, Tq, span), F32),
        compiler_params=pltpu.CompilerParams(vmem_limit_bytes=VMEM_LIMIT),
        name="bias_tile",
    )(thi, tlo, jnp.asarray(idx.reshape(1, L), jnp.int32))


def _band_body(*refs, n_parts, n_prev_rows, pair_map, has_sink, Tq, windows):
    q_ref = refs[0]
    k_refs = refs[1:1 + n_parts]
    v_refs = refs[1 + n_parts:1 + 2 * n_parts]
    bias_ref = refs[1 + 2 * n_parts]
    sink_ref = refs[2 + 2 * n_parts] if has_sink else None
    o_ref = refs[-1]
    i = pl.program_id(1)
    k = jnp.concatenate([r[0].astype(BF) for r in k_refs], axis=0)
    v = jnp.concatenate([r[0].astype(BF) for r in v_refs], axis=0)
    q = q_ref[0]
    span = k.shape[0]
    n_heads = 2 * (q.shape[1] // LANES)
    lane_lo = lax.broadcasted_iota(jnp.int32, (1, LANES), 1) < HEAD_DIM

    items = [(w, h) for w in range(len(windows)) for h in range(n_heads)]

    def sweep(mask_start):
        def scores(w, h):
            qs, qr, ks, kr = windows[w]
            p = h // 2
            qp = q[qs:qs + qr, p * LANES:(p + 1) * LANES]
            qm = jnp.where(lane_lo if h % 2 == 0 else jnp.logical_not(lane_lo), qp, jnp.zeros_like(qp))
            kp = k[ks:ks + kr, pair_map[p] * LANES:(pair_map[p] + 1) * LANES]
            s = lax.dot_general(qm, kp, (((1,), (1,)), ((), ())), preferred_element_type=F32) + bias_ref[h]
            if mask_start and ks < n_prev_rows:
                s = jnp.where(lax.broadcasted_iota(jnp.int32, (1, kr), 1) >= (n_prev_rows - ks - i * Tq), s, NEG)
            return s

        pending = [scores(*it) for it in items[:BAND_AHEAD]]
        halves = []
        for n, (w, h) in enumerate(items):
            qs, qr, ks, kr = windows[w]
            s = pending.pop(0)
            if n + BAND_AHEAD < len(items):
                pending.append(scores(*items[n + BAND_AHEAD]))
            p = h // 2
            vp = v[ks:ks + kr, pair_map[p] * LANES:(pair_map[p] + 1) * LANES]
            m = jnp.max(s, axis=-1, keepdims=True)
            if has_sink:
                sink = sink_ref[h] * LOG2E
                m = jnp.maximum(m, sink)
            e = jnp.exp2(s - m)
            den = jnp.sum(e, axis=-1, keepdims=True)
            if has_sink:
                den = den + jnp.exp2(sink - m)
            o = jnp.dot(e.astype(BF), vp, preferred_element_type=F32)
            halves.append(o * (1.0 / den))
            if h % 2 == 1:
                o_ref[0, qs:qs + qr, p * LANES:(p + 1) * LANES] = jnp.where(lane_lo, halves[0], halves[1]).astype(BF)
                halves = []

    if n_prev_rows:
        pl.when(i * Tq < n_prev_rows)(lambda: sweep(True))
        pl.when(i * Tq >= n_prev_rows)(lambda: sweep(False))
    else:
        sweep(False)


def _band_attn(q, kparts, vparts, bias, sinks, *, Tq, n_prev_rows, pair_map, windows=None):
    nseq, T, W = q.shape
    if windows is None:
        windows = ((0, Tq, 0, sum(p[1] for p in kparts)),)

    def part_spec(arr, rows, off):
        if off is None:
            return pl.BlockSpec((1, rows, arr.shape[2]), lambda b, i: (b, 0, 0))
        return pl.BlockSpec((1, rows, arr.shape[2]), lambda b, i: (b, jnp.maximum(i + off, 0), 0))

    in_specs = [pl.BlockSpec((1, Tq, W), lambda b, i: (b, i, 0))]
    in_specs += [part_spec(*p) for p in kparts] + [part_spec(*p) for p in vparts]
    in_specs.append(_const_spec(bias.shape))
    args = [q] + [p[0] for p in kparts] + [p[0] for p in vparts] + [bias]
    if sinks is not None:
        in_specs.append(pl.BlockSpec(memory_space=pltpu.SMEM))
        args.append(sinks.astype(F32))
    return pl.pallas_call(
        functools.partial(_band_body, n_parts=len(kparts), n_prev_rows=n_prev_rows,
                          pair_map=pair_map, has_sink=sinks is not None, Tq=Tq, windows=tuple(windows)),
        grid=(nseq, T // Tq),
        in_specs=in_specs,
        out_specs=pl.BlockSpec((1, Tq, W), lambda b, i: (b, i, 0)),
        out_shape=jax.ShapeDtypeStruct((nseq, T, W), BF),
        compiler_params=_cparams(2),
        name="band_attn",
    )(*args)


def _proj_c_body(x_ref, sh_ref, sc_ref, g_ref, win_ref, qn_ref, kvn_ref, wqb_ref, cos_ref, sin_ref,
                 cost_ref, sint_ref, q_ref, kv32_ref, kr32_ref, *, q_transposed):
    x = x_ref[...]
    G, R, D = x.shape
    hb = _norm_mod(x, g_ref[...], sh_ref[...], sc_ref[...]).reshape(G * R, D).astype(BF)
    res = jnp.dot(hb, win_ref[...], preferred_element_type=F32)
    q_lat = res[:, :C_Q_LORA]
    kv_lat = res[:, C_Q_LORA:C_Q_LORA + C_KV_LORA]
    krg = res[:, C_Q_LORA + C_KV_LORA:]
    cos = cos_ref[...]
    sin = sin_ref[...]
    qn = (_rms(q_lat) * qn_ref[...]).astype(BF)
    scale = (C_NOPE + C_ROPE) ** -0.5 * math.log2(math.e)
    nh = q_ref.shape[1]
    if q_transposed:
        qt = lax.dot_general(wqb_ref[...], qn, (((1,), (1,)), ((), ())), preferred_element_type=F32)
        cost, sint = cost_ref[...], sint_ref[...]
        for h in range(nh):
            blk = qt[h * C_HEAD_W:(h + 1) * C_HEAD_W]
            roped = blk[C_NOPE:C_NOPE + C_ROPE] * cost + blk[C_NOPE + C_ROPE:] * sint
            q_ref[0, h] = (jnp.concatenate([blk[:C_NOPE], roped, jnp.zeros_like(roped)], axis=0) * scale).astype(BF)
    else:
        qr = jnp.dot(qn, wqb_ref[...], preferred_element_type=F32)
        qrot = pltpu.roll(qr, qr.shape[1] - C_ROPE, 1)
        for h in range(nh):
            a = qr[:, h * C_HEAD_W:(h + 1) * C_HEAD_W].reshape(G, R, C_HEAD_W)
            b = qrot[:, h * C_HEAD_W:(h + 1) * C_HEAD_W].reshape(G, R, C_HEAD_W)
            q_ref[:, h] = ((a * cos + b * sin) * scale).astype(BF)
    kv32_ref[...] = (_rms(kv_lat) * kvn_ref[...]).reshape(G, R, C_KV_LORA)
    krot = pltpu.roll(krg, C_HEAD_W - C_ROPE, 1)
    krf = krg.reshape(G, R, C_HEAD_W) * cos + krot.reshape(G, R, C_HEAD_W) * sin
    kr32_ref[...] = krf[:, :, C_NOPE:C_NOPE + C_ROPE]


def _proj_c(x, mods, blk_off, g, win, qn_g, kvn_g, wqb, tables, G, R, q_transposed):
    nseq, T, D = x.shape
    nh = C_HEADS
    cos_t, sin_t, cost_t, sint_t = tables
    tab = pl.BlockSpec((1, R, C_HEAD_W), lambda i, j: (0, j, 0))
    tabt = pl.BlockSpec((C_ROPE, R), lambda i, j: (0, j))
    if q_transposed:
        assert G == 1
        wqb = wqb.T
        q_spec = pl.BlockSpec((1, nh, C_HEAD_W, R), lambda i, j: (i, 0, 0, j))
        q_shape = jax.ShapeDtypeStruct((nseq, nh, C_HEAD_W, T), BF)
    else:
        q_spec = pl.BlockSpec((G, nh, R, C_HEAD_W), lambda i, j: (i, 0, j, 0))
        q_shape = jax.ShapeDtypeStruct((nseq, nh, T, C_HEAD_W), BF)
    return pl.pallas_call(
        functools.partial(_proj_c_body, q_transposed=q_transposed),
        grid=(nseq // G, T // R),
        in_specs=[pl.BlockSpec((G, R, D), lambda i, j: (i, j, 0)),
                  _mod_spec(G, blk_off, 3, D), _mod_spec(G, blk_off, 4, D), _const_spec((1, D)),
                  _const_spec(win.shape), _const_spec((1, C_Q_LORA)), _const_spec((1, C_KV_LORA)),
                  _const_spec(wqb.shape), tab, tab, tabt, tabt],
        out_specs=[q_spec,
                   pl.BlockSpec((G, R, C_KV_LORA), lambda i, j: (i, j, 0)),
                   pl.BlockSpec((G, R, C_ROPE), lambda i, j: (i, j, 0))],
        out_shape=[q_shape,
                   jax.ShapeDtypeStruct((nseq, T, C_KV_LORA), F32),
                   jax.ShapeDtypeStruct((nseq, T, C_ROPE), F32)],
        compiler_params=_cparams(2),
        name="proj_c",
    )(x, mods, mods, g.reshape(1, D), win, qn_g.reshape(1, -1), kvn_g.reshape(1, -1), wqb,
      cos_t, sin_t, cost_t, sint_t)


def _kvexp_body(kv_ref, kr_ref, wk_ref, wvt_ref, place_ref, ones_ref, k_ref, vt_ref):
    kv = kv_ref[0].astype(BF)
    krf = jnp.dot(kr_ref[0].astype(BF), place_ref[...], preferred_element_type=F32)
    kx = jnp.dot(kv, wk_ref[...], preferred_element_type=F32)
    for h in range(k_ref.shape[1]):
        k_ref[0, h] = (kx[:, h * C_HEAD_W:(h + 1) * C_HEAD_W] + krf).astype(BF)
    vt = lax.dot_general(wvt_ref[...], kv, (((1,), (1,)), ((), ())), preferred_element_type=F32)
    vt_ref[0] = (vt + ones_ref[...]).astype(BF)


def _kv_expand(kv, kr, wk, wvt, R):
    nseq, T, _ = kv.shape
    nh = wk.shape[1] // C_HEAD_W
    place = np.zeros((C_ROPE, C_HEAD_W), np.float32)
    place[np.arange(C_ROPE), C_NOPE + np.arange(C_ROPE)] = 1.0
    ones = np.zeros((nh, C_VT_ROWS, 1), np.float32)
    ones[:, C_V] = 1.0
    ones = ones.reshape(nh * C_VT_ROWS, 1)
    return pl.pallas_call(
        _kvexp_body,
        grid=(nseq, T // R),
        in_specs=[pl.BlockSpec((1, R, C_KV_LORA), lambda b, j: (b, j, 0)),
                  pl.BlockSpec((1, R, C_ROPE), lambda b, j: (b, j, 0)),
                  _const_spec(wk.shape), _const_spec(wvt.shape), _const_spec(place.shape),
                  _const_spec(ones.shape)],
        out_specs=[pl.BlockSpec((1, nh, R, C_HEAD_W), lambda b, j: (b, 0, j, 0)),
                   pl.BlockSpec((1, nh * C_VT_ROWS, R), lambda b, j: (b, 0, j))],
        out_shape=[jax.ShapeDtypeStruct((nseq, nh, T, C_HEAD_W), BF),
                   jax.ShapeDtypeStruct((nseq, nh * C_VT_ROWS, T), BF)],
        compiler_params=_cparams(2),
        name="kv_expand",
    )(kv, kr, wk, wvt, jnp.asarray(place, BF), jnp.asarray(ones))


def _flash_body(qi_ref, kj_ref, q_ref, k_ref, vt_ref, o_ref, acc_ref, m_ref,
                *, T, SK, QW):
    t = pl.program_id(1)
    i = qi_ref[t]
    j = kj_ref[t]
    nh = q_ref.shape[1]
    n_stripes = T // QW

    @pl.when(j == 0)
    def _():
        m_ref[...] = jnp.full(m_ref.shape, NEG, F32)
        acc_ref[...] = jnp.zeros(acc_ref.shape, F32)

    def sweep(diag):
        tiles = []
        for r in range(n_stripes):
            for c in range(T // SK):
                if diag and c * SK >= (r + 1) * QW:
                    continue
                tiles.append((c, r, diag and (c + 1) * SK > r * QW + CHUNK))

        def head_group(g, carry):
            heads = [g * MLA_HU + u for u in range(MLA_HU)]
            stream = [(u, c, r, mask) for u in range(MLA_HU) for (c, r, mask) in tiles]
            state = {(u, r): (m_ref[heads[u], :, r * QW:(r + 1) * QW], acc_ref[heads[u], :, r * QW:(r + 1) * QW])
                     for u in range(MLA_HU) for r in range(n_stripes)}

            def scores(u, c, r, mask):
                s = jnp.dot(k_ref[0, heads[u], c * SK:(c + 1) * SK, :], q_ref[0, heads[u], :, r * QW:(r + 1) * QW],
                            preferred_element_type=F32)
                if mask:
                    kpos = lax.broadcasted_iota(jnp.int32, (SK, 1), 0) + c * SK
                    qpos = lax.broadcasted_iota(jnp.int32, (1, QW), 1) + r * QW
                    s = jnp.where(kpos // CHUNK <= qpos // CHUNK, s, NEG)
                return s

            pending = [scores(*tl) for tl in stream[:MLA_AHEAD]]
            for n, (u, c, r, _) in enumerate(stream):
                s = pending.pop(0)
                if n + MLA_AHEAD < len(stream):
                    pending.append(scores(*stream[n + MLA_AHEAD]))
                m, acc = state[u, r]
                m_new = jnp.maximum(m, jnp.max(s, axis=0, keepdims=True))
                p = jnp.exp2(s - m_new).astype(BF)
                acc = jnp.exp2(m - m_new) * acc + jnp.dot(vt_ref[0, heads[u], :, c * SK:(c + 1) * SK], p,
                                                          preferred_element_type=F32)
                state[u, r] = (m_new, acc)
            for (u, r), (m, acc) in state.items():
                m_ref[heads[u], :, r * QW:(r + 1) * QW] = m
                acc_ref[heads[u], :, r * QW:(r + 1) * QW] = acc
            return carry
        lax.fori_loop(0, nh // MLA_HU, head_group, 0)

    pl.when(j == i)(lambda: sweep(True))
    pl.when(j != i)(lambda: sweep(False))

    @pl.when(j == i)
    def _():
        for p in range(nh // 2):
            halves = [acc_ref[h, :C_V, :] * (1.0 / acc_ref[h, C_V:C_V + 1, :]) for h in (2 * p, 2 * p + 1)]
            o_ref[0, :, p * LANES:(p + 1) * LANES] = jnp.concatenate(halves, axis=0).T.astype(BF)


def _mla_attn(q, k, vt, *, T, SK, QW):
    nseq, nh, _, S = q.shape
    assert S % T == 0 and T % QW == 0 and T % SK == 0 and SK % CHUNK == 0 and QW % CHUNK == 0
    Tq = Tk = T
    pairs = [(i, j) for i in range(S // T) for j in range(i + 1)]
    qi = jnp.asarray([p[0] for p in pairs], jnp.int32)
    kj = jnp.asarray([p[1] for p in pairs], jnp.int32)
    grid_spec = pltpu.PrefetchScalarGridSpec(
        num_scalar_prefetch=2,
        grid=(nseq, len(pairs)),
        in_specs=[pl.BlockSpec((1, nh, C_HEAD_W, Tq), lambda b, t, qi, kj: (b, 0, 0, qi[t])),
                  pl.BlockSpec((1, nh, Tk, C_HEAD_W), lambda b, t, qi, kj: (b, 0, kj[t], 0)),
                  pl.BlockSpec((1, nh, C_VT_ROWS, Tk), lambda b, t, qi, kj: (b, 0, 0, kj[t]))],
        out_specs=pl.BlockSpec((1, Tq, nh * C_V), lambda b, t, qi, kj: (b, qi[t], 0)),
        scratch_shapes=[pltpu.VMEM((nh, C_VT_ROWS, Tq), F32), pltpu.VMEM((nh, 1, Tq), F32)],
    )
    return pl.pallas_call(
        functools.partial(_flash_body, T=T, SK=SK, QW=QW),
        grid_spec=grid_spec,
        out_shape=jax.ShapeDtypeStruct((nseq, S, nh * C_V), BF),
        compiler_params=_cparams(2),
        name="mla_attn",
    )(qi, kj, q, k, vt)


def _decode_body(q_ref, ckv_ref, ckr_ref, nkv_ref, nkr_ref, wk_ref, wv_ref, place_ref, o_ref, *, n_pad):
    nh, tq, _ = q_ref.shape[1:]
    n_keys = ckv_ref.shape[1] + nkv_ref.shape[1]
    kv = jnp.concatenate([ckv_ref[0].astype(BF), nkv_ref[0].astype(BF),
                          jnp.zeros((n_pad - n_keys, C_KV_LORA), BF)], axis=0)
    kr = jnp.concatenate([ckr_ref[0].astype(BF), nkr_ref[0].astype(BF),
                          jnp.zeros((n_pad - n_keys, C_ROPE), BF)], axis=0)
    krf = jnp.dot(kr, place_ref[...], preferred_element_type=F32).astype(BF)
    keys = jnp.concatenate([kv, krf], axis=1)
    q = q_ref[0]
    q_lat = jnp.concatenate([jnp.dot(q[h], wk_ref[h], preferred_element_type=F32) for h in range(nh)], axis=0)
    queries = jnp.concatenate([q_lat.astype(BF), q.reshape(nh * tq, C_HEAD_W)], axis=1)
    s = lax.dot_general(queries, keys, (((1,), (1,)), ((), ())), preferred_element_type=F32)
    s = jnp.where(lax.broadcasted_iota(jnp.int32, (1, n_pad), 1) < n_keys, s, NEG)
    m = jnp.max(s, axis=-1, keepdims=True)
    e = jnp.exp2(s - m)
    den = jnp.sum(e, axis=-1, keepdims=True)
    lat = (jnp.dot(e.astype(BF), kv, preferred_element_type=F32) * (1.0 / den)).astype(BF)
    for p in range(nh // 2):
        pair = jnp.concatenate([lat[2 * p * tq:(2 * p + 1) * tq], lat[(2 * p + 1) * tq:(2 * p + 2) * tq]], axis=1)
        o_ref[0, :, p * LANES:(p + 1) * LANES] = jnp.dot(pair, wv_ref[p], preferred_element_type=F32).astype(BF)


def _mla_decode(q, cache_kv, cache_kr, new_kv, new_kr, wk_heads, wv_pairs):
    nseq, nh, tq, _ = q.shape
    past = cache_kv.shape[1]
    n_pad = -(-(past + tq) // LANES) * LANES
    place = np.zeros((C_ROPE, C_HEAD_W), np.float32)
    place[np.arange(C_ROPE), C_NOPE + np.arange(C_ROPE)] = 1.0
    per_seq = lambda a: pl.BlockSpec((1,) + a.shape[1:], lambda b: (b,) + (0,) * (a.ndim - 1))
    return pl.pallas_call(
        functools.partial(_decode_body, n_pad=n_pad),
        grid=(nseq,),
        in_specs=[per_seq(q), per_seq(cache_kv), per_seq(cache_kr), per_seq(new_kv), per_seq(new_kr),
                  _const_spec(wk_heads.shape), _const_spec(wv_pairs.shape), _const_spec(place.shape)],
        out_specs=pl.BlockSpec((1, tq, nh * C_V), lambda b: (b, 0, 0)),
        out_shape=jax.ShapeDtypeStruct((nseq, tq, nh * C_V), BF),
        compiler_params=_cparams(1),
        name="mla_decode",
    )(q, cache_kv, cache_kr, new_kv, new_kr, wk_heads, wv_pairs, jnp.asarray(place, BF))


def _rope_tables(pos):
    half = C_ROPE // 2
    inv = (np.float32(ROPE_BASE) ** (-np.arange(half, dtype=np.float32) / np.float32(half))).astype(np.float32)
    ang = (pos.astype(np.float32)[:, None] * inv[None, :]).astype(np.float32).astype(np.float64)
    n = pos.shape[0]
    cos = np.zeros((1, n, C_HEAD_W), np.float32)
    sin = np.zeros((1, n, C_HEAD_W), np.float32)
    cos[0, :, :C_NOPE] = 1.0
    cos[0, :, C_NOPE:C_NOPE + C_ROPE] = np.concatenate([np.cos(ang), np.cos(ang)], axis=1)
    sin[0, :, C_NOPE:C_NOPE + C_ROPE] = np.concatenate([np.sin(ang), np.sin(ang)], axis=1)
    cos32, sin32 = cos[0, :, C_NOPE:C_NOPE + C_ROPE].T, sin[0, :, C_NOPE:C_NOPE + C_ROPE].T
    return jnp.asarray(cos), jnp.asarray(sin), jnp.asarray(cos32), jnp.asarray(sin32)


def _rot_half_cols(w):
    half = w.shape[-1] // 2
    return jnp.concatenate([-w[..., half:], w[..., :half]], axis=-1)


def kernel(x_prompt, x_sample, c_prompt, c_sample, cache_a_k, cache_a_v, cache_b_k, cache_b_v, cache_c_kv, cache_c_kr, w_ada, b_ada, norm_g, final_norm_g, ffn_w_gate, ffn_w_up, ffn_w_down, w_in_ab, w_out_ab, rel_bias_a, t5_bias, sinks_b, w_in_c, c_q_norm_g, c_kv_norm_g, w_qb, w_kvb, w_out_c):
    nb, S, D = x_prompt.shape
    ns, TS, _ = x_sample.shape
    la_c, lb_c = cache_a_k.shape[2], cache_b_k.shape[2]
    past = cache_c_kv.shape[2]
    la_p, lb_p = min(A_PREV_CHUNKS * CHUNK, S), min(B_PREV_CHUNKS * CHUNK, S)
    assert la_p <= ROW_TILE and S % MLA_T == 0 and S % ROW_TILE == 0 and TS % 8 == 0

    n_cond = ns + nb
    n_cond_pad = -(-n_cond // 8) * 8
    c_all = jnp.zeros((n_cond_pad, D), F32).at[:ns].set(c_sample).at[ns:n_cond].set(c_prompt)
    mods_all = _adaln(c_all, w_ada, b_ada)
    groups = {
        "p": dict(x=x_prompt, blk_off=ns, G=1, R=ROW_TILE),
        "s": dict(x=x_sample, blk_off=0, G=ns, R=TS),
    }

    ffn_w = tuple(w.astype(BF) for w in (ffn_w_gate, ffn_w_up, ffn_w_down))
    w_ab = w_in_ab[0].astype(BF)
    w_oa, w_ob = w_out_ab[0, :A_W].astype(BF), w_out_ab[0, A_W:].astype(BF)
    pair_a = tuple(range(A_HEADS // 2))
    pair_b = tuple(p // (B_Q_HEADS // B_KV_HEADS // 2) for p in range(B_Q_HEADS // 2))
    a_prev_rows = 2 * BAND_TQ
    b_prev_rows = BAND_TQ
    assert a_prev_rows == A_PREV_CHUNKS * CHUNK and b_prev_rows >= B_PREV_CHUNKS * CHUNK
    a_half = BAND_TQ // 2
    windows_a = tuple((qs, a_half, qs, a_prev_rows + a_half) for qs in (0, a_half))
    bias_ap = _bias_tile(rel_bias_a[0], "clip", a_half, a_prev_rows + a_half, a_prev_rows, A_PREV_CHUNKS)
    b_half = BAND_TQ // 2
    b_back = B_PREV_CHUNKS * CHUNK
    assert b_half % CHUNK == 0 and b_back <= b_prev_rows
    windows_b = tuple((qs, b_half, b_prev_rows + qs - b_back, b_back + b_half) for qs in (0, b_half))
    bias_bp = _bias_tile(t5_bias, "t5", b_half, b_back + b_half, b_back, B_PREV_CHUNKS)
    bias_as = _bias_tile(rel_bias_a[0], "clip", TS, la_c + TS, la_c, None)
    bias_bs = _bias_tile(t5_bias, "t5", TS, lb_c + TS, lb_c, None)
    dup = lambda c: jnp.repeat(c, 2, axis=2).reshape(c.shape[0], c.shape[1], 2 * BKV_W)
    cak = cache_a_k[0].reshape(ns, la_c, A_W)
    cav = cache_a_v[0].reshape(ns, la_c, A_W)
    cbk, cbv = dup(cache_b_k[0]), dup(cache_b_v[0])

    mods = mods_all[0].reshape(n_cond_pad, 1, -1)
    state0 = {}
    for name, gr in groups.items():
        x, off, G, R = gr["x"], gr["blk_off"] // gr["G"], gr["G"], gr["R"]
        x = _ffn(x, mods, off, 0, norm_g[0, 0], ffn_w, 0, 0, G, R)
        ta, tb = (la_p, lb_p) if name == "p" else (TS, TS)
        qa, ka, va, qb, kb, vb, ka32, va32, kb32, vb32 = _proj_ab(x, mods, off, norm_g[0, 1], w_ab, G, R, ta, tb)
        if name == "p":
            prev = lambda arr, n: [(arr, BAND_TQ, d - n) for d in range(n + 1)]
            oa = _band_attn(qa, prev(ka, 2), prev(va, 2), bias_ap, None,
                            Tq=BAND_TQ, n_prev_rows=a_prev_rows, pair_map=pair_a, windows=windows_a)
            ob = _band_attn(qb, prev(kb, 1), prev(vb, 1), bias_bp, sinks_b[0],
                            Tq=BAND_TQ, n_prev_rows=b_prev_rows, pair_map=pair_b, windows=windows_b)
        else:
            oa = _band_attn(qa, [(cak, la_c, None), (ka, TS, None)], [(cav, la_c, None), (va, TS, None)],
                            bias_as, None, Tq=TS, n_prev_rows=0, pair_map=pair_a)
            ob = _band_attn(qb, [(cbk, lb_c, None), (kb, TS, None)], [(cbv, lb_c, None), (vb, TS, None)],
                            bias_bs, sinks_b[0], Tq=TS, n_prev_rows=0, pair_map=pair_b)
        x = _ffn(x, mods, off, 6, norm_g[0, 2], ffn_w, 0, 1, G, R, attn=([oa, ob], [w_oa, w_ob]))
        gr["x"] = x
        n_out = x.shape[0]
        state0[name] = (ka32.reshape(1, n_out, ta, A_HEADS, HEAD_DIM), va32.reshape(1, n_out, ta, A_HEADS, HEAD_DIM),
                        kb32.reshape(1, n_out, tb, B_KV_HEADS, HEAD_DIM), vb32.reshape(1, n_out, tb, B_KV_HEADS, HEAD_DIM))

    hw = C_NOPE + C_ROPE
    w_in = w_in_c[0]
    w_kr = w_in[:, C_Q_LORA + C_KV_LORA:]
    win_ext = jnp.concatenate([w_in[:, :C_Q_LORA + C_KV_LORA], jnp.zeros((D, C_NOPE), F32),
                               w_kr, _rot_half_cols(w_kr)], axis=1).astype(BF)
    wq3 = w_qb[0].reshape(C_Q_LORA, C_HEADS, hw)
    wqb_ext = jnp.concatenate([wq3, _rot_half_cols(wq3[..., C_NOPE:])], axis=-1
                              ).reshape(C_Q_LORA, C_HEADS * C_HEAD_W).astype(BF)
    wkv3 = w_kvb[0].reshape(C_KV_LORA, C_HEADS, C_NOPE + C_V)
    wk_ext = jnp.concatenate([wkv3[..., :C_NOPE], jnp.zeros((C_KV_LORA, C_HEADS, C_HEAD_W - C_NOPE), F32)],
                             axis=-1).reshape(C_KV_LORA, C_HEADS * C_HEAD_W).astype(BF)
    wvt = jnp.concatenate([wkv3[..., C_NOPE:], jnp.zeros((C_KV_LORA, C_HEADS, C_VT_ROWS - C_V), F32)], axis=-1
                          ).reshape(C_KV_LORA, C_HEADS * C_VT_ROWS).T.astype(BF)
    w_oc = w_out_c[0].astype(BF)
    wk_heads = jnp.concatenate([jnp.transpose(wkv3[..., :C_NOPE], (1, 2, 0)),
                                jnp.zeros((C_HEADS, C_HEAD_W - C_NOPE, C_KV_LORA), F32)], axis=1).astype(BF)
    wv4 = jnp.transpose(wkv3[..., C_NOPE:], (1, 0, 2)).reshape(C_HEADS // 2, 2, C_KV_LORA, C_V)
    zv = jnp.zeros((C_HEADS // 2, C_KV_LORA, C_V), F32)
    wv_pairs = jnp.concatenate([jnp.concatenate([wv4[:, 0], zv], axis=2),
                                jnp.concatenate([zv, wv4[:, 1]], axis=2)], axis=1).astype(BF)
    tables = {"p": _rope_tables(np.arange(S)), "s": _rope_tables(past + np.arange(TS))}

    mods = mods_all[1].reshape(n_cond_pad, 1, -1)
    state1 = {}
    for name, gr in groups.items():
        x, off, G, R = gr["x"], gr["blk_off"] // gr["G"], gr["G"], gr["R"]
        x = _ffn(x, mods, off, 0, norm_g[1, 0], ffn_w, 1, 0, G, R)
        q, kv32, kr32 = _proj_c(x, mods, off, norm_g[1, 1], win_ext, c_q_norm_g[0], c_kv_norm_g[0],
                                wqb_ext, tables[name], G, R, q_transposed=(name == "p"))
        if name == "p":
            k, vt = _kv_expand(kv32, kr32, wk_ext, wvt, ROW_TILE)
            o = _mla_attn(q, k, vt.reshape(nb, C_HEADS, C_VT_ROWS, S), T=MLA_T, SK=MLA_SK, QW=MLA_QW)
        else:
            o = _mla_decode(q, cache_c_kv[0], cache_c_kr[0], kv32, kr32, wk_heads, wv_pairs)
        x = _ffn(x, mods, off, 6, norm_g[1, 2], ffn_w, 1, 1, G, R, final_g=final_norm_g, attn=([o], [w_oc]))
        gr["x"] = x
        state1[name] = (kv32[None], kr32[None])

    return (groups["p"]["x"], groups["s"]["x"],
            *state0["p"], *state1["p"], *state0["s"], *state1["s"])
```

```python
import functools
import math

import numpy as np
import jax
import jax.numpy as jnp
from jax import lax
from jax.experimental import pallas as pl
from jax.experimental.pallas import tpu as pltpu

F32 = jnp.float32
BF = jnp.bfloat16

CHUNK = 64
HEAD_DIM = 64
A_HEADS = 8
A_PREV_CHUNKS = 8
A_REL_CLIP = 128
B_Q_HEADS = 8
B_KV_HEADS = 2
B_PREV_CHUNKS = 2
T5_BUCKETS = 32
T5_MAX_DIST = 128
C_HEADS = 16
C_Q_LORA = 384
C_KV_LORA = 256
C_NOPE = 64
C_ROPE = 32
C_V = 64
ROPE_BASE = 10000.0
EPS = 1e-6
NEG = -1e30
A_W = A_HEADS * HEAD_DIM
BQ_W = B_Q_HEADS * HEAD_DIM
BKV_W = B_KV_HEADS * HEAD_DIM

LANES = 128
VMEM_LIMIT = 56 * 1024 * 1024

ROW_TILE = 1024
FF_CHUNK = 256
BAND_TQ = 512
BAND_WIN = 128
BAND_AHEAD = 2
LOG2E = math.log2(math.e)
MLA_T = 1024
MLA_SK = 256
MLA_QW = 256
MLA_HU = 8
MLA_AHEAD = 5
C_VT_ROWS = 80
C_IN_EXT = 768
C_HEAD_W = 128


def _cparams(n_axes, vmem=VMEM_LIMIT):
    return pltpu.CompilerParams(dimension_semantics=("arbitrary",) * n_axes,
                                vmem_limit_bytes=vmem)


def _const_spec(shape):
    n = len(shape)
    return pl.BlockSpec(shape, lambda *_: (0,) * n)


def _rms(x):
    return x * lax.rsqrt(jnp.mean(x * x, axis=-1, keepdims=True) + EPS)


def _norm_mod(x, g, shift, scale):
    return (_rms(x) * g) * (1.0 + scale) + shift


def _mod_spec(G, blk_off, k, D):
    return pl.BlockSpec((G, 1, D), lambda i, j: (blk_off + i, 0, k))


def _adaln_body(c_ref, w_ref, b_ref, o_ref):
    c = c_ref[...]
    s = (c * jax.nn.sigmoid(c)).astype(BF)
    o_ref[0] = jnp.dot(s, w_ref[0].astype(BF), preferred_element_type=F32) + b_ref[0]


def _adaln(c_all, w_ada, b_ada):
    L, D, N = w_ada.shape
    R = c_all.shape[0]
    tn = 1024
    return pl.pallas_call(
        _adaln_body,
        grid=(L, N // tn),
        in_specs=[pl.BlockSpec((R, D), lambda l, j: (0, 0)),
                  pl.BlockSpec((1, D, tn), lambda l, j: (l, 0, j)),
                  pl.BlockSpec((1, 1, tn), lambda l, j: (l, 0, j))],
        out_specs=pl.BlockSpec((1, R, tn), lambda l, j: (l, 0, j)),
        out_shape=jax.ShapeDtypeStruct((L, R, N), F32),
        compiler_params=_cparams(2),
        name="adaln",
    )(c_all, w_ada, b_ada.reshape(L, 1, N))


def _ffn_body(*refs, n_attn, final):
    attn_refs, wo_refs = refs[:n_attn], refs[n_attn:2 * n_attn]
    refs = refs[2 * n_attn:]
    if n_attn:
        ga_ref, refs = refs[0], refs[1:]
    x_ref, sh_ref, sc_ref, gt_ref, g_ref, wg_ref, wu_ref, wd_ref = refs[:8]
    if final:
        fg_ref, o_ref, a_ref = refs[8:]
    else:
        o_ref, a_ref = refs[8:]
    x = x_ref[...]
    G, R, D = x.shape
    F = wg_ref.shape[1]
    if n_attn:
        mix = None
        for at_ref, wo_ref in zip(attn_refs, wo_refs):
            y = jnp.dot(at_ref[...].reshape(G * R, at_ref.shape[2]), wo_ref[...], preferred_element_type=F32)
            mix = y if mix is None else mix + y
        x = x + ga_ref[...] * mix.reshape(G, R, D)
    hb = _norm_mod(x, g_ref[...], sh_ref[...], sc_ref[...]).reshape(G * R, D).astype(BF)
    for c in range(F // FF_CHUNK):
        lo, hi = c * FF_CHUNK, (c + 1) * FF_CHUNK
        g = jnp.dot(hb, wg_ref[:, lo:hi], preferred_element_type=F32)
        u = jnp.dot(hb, wu_ref[:, lo:hi], preferred_element_type=F32)
        a_ref[:, lo:hi] = (g * jax.nn.sigmoid(g) * u).astype(BF)
    ff = jnp.dot(a_ref[...], wd_ref[...], preferred_element_type=F32)
    y = x + (0.5 * gt_ref[...]) * ff.reshape(G, R, D)
    if final:
        y = _rms(y) * fg_ref[...]
    o_ref[...] = y


def _ffn(x, mods, blk_off, kbase, g, weights, layer, which, G, R, final_g=None, attn=None):
    nseq, T, D = x.shape
    wg, wu, wd = weights
    F = wg.shape[-1]
    final = final_g is not None
    row = lambda W: pl.BlockSpec((G, R, W), lambda i, j: (i, j, 0))
    stacked = lambda w: pl.BlockSpec((None, None) + w.shape[2:], lambda i, j: (layer, which, 0, 0),
                                     pipeline_mode=pl.Buffered(1))
    in_specs, args = [], []
    n_attn = 0
    if attn is not None:
        os_, ws = attn
        n_attn = len(os_)
        in_specs += [row(o.shape[2]) for o in os_] + [_const_spec(w.shape) for w in ws]
        in_specs.append(_mod_spec(G, blk_off, 5, D))
        args += [*os_, *ws, mods]
    in_specs += [row(D), _mod_spec(G, blk_off, kbase, D), _mod_spec(G, blk_off, kbase + 1, D),
                 _mod_spec(G, blk_off, kbase + 2, D), _const_spec((1, D)), stacked(wg), stacked(wu), stacked(wd)]
    args += [x, mods, mods, mods, g.reshape(1, D), wg, wu, wd]
    if final:
        in_specs.append(_const_spec((1, D)))
        args.append(final_g.reshape(1, D))
    return pl.pallas_call(
        functools.partial(_ffn_body, n_attn=n_attn, final=final),
        grid=(nseq // G, T // R),
        in_specs=in_specs,
        out_specs=pl.BlockSpec((G, R, D), lambda i, j: (i, j, 0)),
        out_shape=jax.ShapeDtypeStruct((nseq, T, D), F32),
        scratch_shapes=[pltpu.VMEM((G * R, F), BF)],
        compiler_params=_cparams(2),
        name="ffn",
    )(*args)


def _dup_halves(k):
    rolled = pltpu.roll(k, HEAD_DIM, 1)
    lo = lax.broadcasted_iota(jnp.int32, k.shape, 1) < HEAD_DIM
    return jnp.concatenate([jnp.where(lo, k, rolled), jnp.where(lo, rolled, k)], axis=1)


def _proj_ab_body(x_ref, sh_ref, sc_ref, g_ref, w_ref,
                  qa_ref, ka_ref, va_ref, qb_ref, kb_ref, vb_ref,
                  ka32_ref, va32_ref, kb32_ref, vb32_ref, *, ta, tb):
    x = x_ref[...]
    G, R, D = x.shape
    hb = _norm_mod(x, g_ref[...], sh_ref[...], sc_ref[...]).reshape(G * R, D).astype(BF)
    res = jnp.dot(hb, w_ref[...], preferred_element_type=F32)
    o_ka, o_va, o_qb, o_kb, o_vb = A_W, 2 * A_W, 3 * A_W, 3 * A_W + BQ_W, 3 * A_W + BQ_W + BKV_W
    qscale = HEAD_DIM ** -0.5 * LOG2E
    qa_ref[...] = (res[:, :o_ka] * qscale).astype(BF).reshape(G, R, A_W)
    ka_ref[...] = res[:, o_ka:o_va].astype(BF).reshape(G, R, A_W)
    va_ref[...] = res[:, o_va:o_qb].astype(BF).reshape(G, R, A_W)
    qb_ref[...] = (res[:, o_qb:o_kb] * qscale).astype(BF).reshape(G, R, BQ_W)
    kb = res[:, o_kb:o_vb]
    vb = res[:, o_vb:o_vb + BKV_W]
    kb_ref[...] = _dup_halves(kb).astype(BF).reshape(G, R, 2 * BKV_W)
    vb_ref[...] = _dup_halves(vb).astype(BF).reshape(G, R, 2 * BKV_W)

    @pl.when(pl.program_id(1) == pl.num_programs(1) - 1)
    def _():
        ka32_ref[...] = res[:, o_ka:o_va].reshape(G, R, A_W)[:, R - ta:, :]
        va32_ref[...] = res[:, o_va:o_qb].reshape(G, R, A_W)[:, R - ta:, :]
        kb32_ref[...] = kb.reshape(G, R, BKV_W)[:, R - tb:, :]
        vb32_ref[...] = vb.reshape(G, R, BKV_W)[:, R - tb:, :]


def _proj_ab(x, mods, blk_off, g, w, G, R, ta, tb):
    nseq, T, D = x.shape
    row = lambda W: pl.BlockSpec((G, R, W), lambda i, j: (i, j, 0))
    tail = lambda t, W: pl.BlockSpec((G, t, W), lambda i, j: (i, 0, 0))
    bshape = lambda W: jax.ShapeDtypeStruct((nseq, T, W), BF)
    return pl.pallas_call(
        functools.partial(_proj_ab_body, ta=ta, tb=tb),
        grid=(nseq // G, T // R),
        in_specs=[row(D), _mod_spec(G, blk_off, 3, D), _mod_spec(G, blk_off, 4, D),
                  _const_spec((1, D)), _const_spec(w.shape)],
        out_specs=[row(A_W), row(A_W), row(A_W), row(BQ_W), row(2 * BKV_W), row(2 * BKV_W),
                   tail(ta, A_W), tail(ta, A_W), tail(tb, BKV_W), tail(tb, BKV_W)],
        out_shape=[bshape(A_W), bshape(A_W), bshape(A_W), bshape(BQ_W), bshape(2 * BKV_W), bshape(2 * BKV_W),
                   jax.ShapeDtypeStruct((nseq, ta, A_W), F32), jax.ShapeDtypeStruct((nseq, ta, A_W), F32),
                   jax.ShapeDtypeStruct((nseq, tb, BKV_W), F32), jax.ShapeDtypeStruct((nseq, tb, BKV_W), F32)],
        compiler_params=_cparams(2),
        name="proj_ab",
    )(x, mods, mods, g.reshape(1, D), w)


def _t5_bucket_np(rel):
    nb = T5_BUCKETS // 2
    ret = np.where(rel > 0, nb, 0)
    n = np.abs(rel)
    max_exact = nb // 2
    ratio = np.maximum(n, 1).astype(np.float32) / np.float32(max_exact)
    large = max_exact + (np.log(ratio).astype(np.float32) / np.float32(math.log(T5_MAX_DIST / max_exact))
                         * np.float32(nb - max_exact)).astype(np.int32)
    large = np.minimum(large, nb - 1)
    return ret + np.where(n < max_exact, n, large)


def _bias_body(thi_ref, tlo_ref, idx_ref, o_ref, *, Tq, span, prev_chunks, band_chunks):
    E = thi_ref.shape[1]
    L = idx_ref.shape[1]
    onehot = (lax.broadcasted_iota(jnp.int32, (E, L), 0) == idx_ref[...]).astype(BF)
    t = (jnp.dot(thi_ref[...], onehot, preferred_element_type=F32)
         + jnp.dot(tlo_ref[...], onehot, preferred_element_type=F32)) * LOG2E
    if band_chunks is not None:
        qc = lax.broadcasted_iota(jnp.int32, (Tq, span), 0) // CHUNK
        kc = lax.broadcasted_iota(jnp.int32, (Tq, span), 1) // CHUNK - prev_chunks
        valid = (kc <= qc) & (kc >= qc - band_chunks)
    for h in range(o_ref.shape[0]):
        x = jnp.broadcast_to(t[h:h + 1, :], (Tq, L))
        y = pltpu.roll(x, L - Tq + 1, 1, stride=1, stride_axis=0)[:, :span]
        if band_chunks is not None:
            y = jnp.where(valid, y, NEG)
        o_ref[h] = y


def _bias_tile(table, kind, Tq, span, n_prev_rows, band_chunks):
    H = table.shape[1]
    L = -(-(Tq + span - 1) // LANES) * LANES
    rel = np.arange(L) - (Tq - 1) - n_prev_rows
    if kind == "clip":
        idx = np.clip(rel, -A_REL_CLIP, A_REL_CLIP) + A_REL_CLIP
    else:
        idx = _t5_bucket_np(rel)
    E = -(-table.shape[0] // LANES) * LANES
    tt = jnp.zeros((H, E), F32).at[:, :table.shape[0]].set(table.T.astype(F32))
    thi = tt.astype(BF)
    tlo = (tt - thi.astype(F32)).astype(BF)
    return pl.pallas_call(
        functools.partial(_bias_body, Tq=Tq, span=span, prev_chunks=n_prev_rows // CHUNK,
                          band_chunks=band_chunks),
        out_shape=jax.ShapeDtypeStruct((H, Tq, span), F32),
        compiler_params=pltpu.CompilerParams(vmem_limit_bytes=VMEM_LIMIT),
        name="bias_tile",
    )(thi, tlo, jnp.asarray(idx.reshape(1, L), jnp.int32))


def _band_body(*refs, n_parts, n_prev_rows, pair_map, has_sink, Tq, windows):
    q_ref = refs[0]
    k_refs = refs[1:1 + n_parts]
    v_refs = refs[1 + n_parts:1 + 2 * n_parts]
    bias_ref = refs[1 + 2 * n_parts]
    sink_ref = refs[2 + 2 * n_parts] if has_sink else None
    o_ref = refs[-1]
    i = pl.program_id(1)
    k = jnp.concatenate([r[0].astype(BF) for r in k_refs], axis=0)
    v = jnp.concatenate([r[0].astype(BF) for r in v_refs], axis=0)
    q = q_ref[0]
    span = k.shape[0]
    n_heads = 2 * (q.shape[1] // LANES)
    lane_lo = lax.broadcasted_iota(jnp.int32, (1, LANES), 1) < HEAD_DIM

    items = [(w, h) for w in range(len(windows)) for h in range(n_heads)]

    def sweep(mask_start):
        def scores(w, h):
            qs, qr, ks, kr = windows[w]
            p = h // 2
            qp = q[qs:qs + qr, p * LANES:(p + 1) * LANES]
            qm = jnp.where(lane_lo if h % 2 == 0 else jnp.logical_not(lane_lo), qp, jnp.zeros_like(qp))
            kp = k[ks:ks + kr, pair_map[p] * LANES:(pair_map[p] + 1) * LANES]
            s = lax.dot_general(qm, kp, (((1,), (1,)), ((), ())), preferred_element_type=F32) + bias_ref[h]
            if mask_start and ks < n_prev_rows:
                s = jnp.where(lax.broadcasted_iota(jnp.int32, (1, kr), 1) >= (n_prev_rows - ks - i * Tq), s, NEG)
            return s

        pending = [scores(*it) for it in items[:BAND_AHEAD]]
        halves = []
        for n, (w, h) in enumerate(items):
            qs, qr, ks, kr = windows[w]
            s = pending.pop(0)
            if n + BAND_AHEAD < len(items):
                pending.append(scores(*items[n + BAND_AHEAD]))
            p = h // 2
            vp = v[ks:ks + kr, pair_map[p] * LANES:(pair_map[p] + 1) * LANES]
            m = jnp.max(s, axis=-1, keepdims=True)
            if has_sink:
                sink = sink_ref[h] * LOG2E
                m = jnp.maximum(m, sink)
            e = jnp.exp2(s - m)
            den = jnp.sum(e, axis=-1, keepdims=True)
            if has_sink:
                den = den + jnp.exp2(sink - m)
            o = jnp.dot(e.astype(BF), vp, preferred_element_type=F32)
            halves.append(o * (1.0 / den))
            if h % 2 == 1:
                o_ref[0, qs:qs + qr, p * LANES:(p + 1) * LANES] = jnp.where(lane_lo, halves[0], halves[1]).astype(BF)
                halves = []

    if n_prev_rows:
        pl.when(i * Tq < n_prev_rows)(lambda: sweep(True))
        pl.when(i * Tq >= n_prev_rows)(lambda: sweep(False))
    else:
        sweep(False)


def _band_attn(q, kparts, vparts, bias, sinks, *, Tq, n_prev_rows, pair_map, windows=None):
    nseq, T, W = q.shape
    if windows is None:
        windows = ((0, Tq, 0, sum(p[1] for p in kparts)),)

    def part_spec(arr, rows, off):
        if off is None:
            return pl.BlockSpec((1, rows, arr.shape[2]), lambda b, i: (b, 0, 0))
        return pl.BlockSpec((1, rows, arr.shape[2]), lambda b, i: (b, jnp.maximum(i + off, 0), 0))

    in_specs = [pl.BlockSpec((1, Tq, W), lambda b, i: (b, i, 0))]
    in_specs += [part_spec(*p) for p in kparts] + [part_spec(*p) for p in vparts]
    in_specs.append(_const_spec(bias.shape))
    args = [q] + [p[0] for p in kparts] + [p[0] for p in vparts] + [bias]
    if sinks is not None:
        in_specs.append(pl.BlockSpec(memory_space=pltpu.SMEM))
        args.append(sinks.astype(F32))
    return pl.pallas_call(
        functools.partial(_band_body, n_parts=len(kparts), n_prev_rows=n_prev_rows,
                          pair_map=pair_map, has_sink=sinks is not None, Tq=Tq, windows=tuple(windows)),
        grid=(nseq, T // Tq),
        in_specs=in_specs,
        out_specs=pl.BlockSpec((1, Tq, W), lambda b, i: (b, i, 0)),
        out_shape=jax.ShapeDtypeStruct((nseq, T, W), BF),
        compiler_params=_cparams(2),
        name="band_attn",
    )(*args)


def _proj_c_body(x_ref, sh_ref, sc_ref, g_ref, win_ref, qn_ref, kvn_ref, wqb_ref, cos_ref, sin_ref,
                 cost_ref, sint_ref, q_ref, kv32_ref, kr32_ref, *, q_transposed):
    x = x_ref[...]
    G, R, D = x.shape
    hb = _norm_mod(x, g_ref[...], sh_ref[...], sc_ref[...]).reshape(G * R, D).astype(BF)
    res = jnp.dot(hb, win_ref[...], preferred_element_type=F32)
    q_lat = res[:, :C_Q_LORA]
    kv_lat = res[:, C_Q_LORA:C_Q_LORA + C_KV_LORA]
    krg = res[:, C_Q_LORA + C_KV_LORA:]
    cos = cos_ref[...]
    sin = sin_ref[...]
    qn = (_rms(q_lat) * qn_ref[...]).astype(BF)
    scale = (C_NOPE + C_ROPE) ** -0.5 * math.log2(math.e)
    nh = q_ref.shape[1]
    if q_transposed:
        qt = lax.dot_general(wqb_ref[...], qn, (((1,), (1,)), ((), ())), preferred_element_type=F32)
        cost, sint = cost_ref[...], sint_ref[...]
        for h in range(nh):
            blk = qt[h * C_HEAD_W:(h + 1) * C_HEAD_W]
            roped = blk[C_NOPE:C_NOPE + C_ROPE] * cost + blk[C_NOPE + C_ROPE:] * sint
            q_ref[0, h] = (jnp.concatenate([blk[:C_NOPE], roped, jnp.zeros_like(roped)], axis=0) * scale).astype(BF)
    else:
        qr = jnp.dot(qn, wqb_ref[...], preferred_element_type=F32)
        qrot = pltpu.roll(qr, qr.shape[1] - C_ROPE, 1)
        for h in range(nh):
            a = qr[:, h * C_HEAD_W:(h + 1) * C_HEAD_W].reshape(G, R, C_HEAD_W)
            b = qrot[:, h * C_HEAD_W:(h + 1) * C_HEAD_W].reshape(G, R, C_HEAD_W)
            q_ref[:, h] = ((a * cos + b * sin) * scale).astype(BF)
    kv32_ref[...] = (_rms(kv_lat) * kvn_ref[...]).reshape(G, R, C_KV_LORA)
    krot = pltpu.roll(krg, C_HEAD_W - C_ROPE, 1)
    krf = krg.reshape(G, R, C_HEAD_W) * cos + krot.reshape(G, R, C_HEAD_W) * sin
    kr32_ref[...] = krf[:, :, C_NOPE:C_NOPE + C_ROPE]


def _proj_c(x, mods, blk_off, g, win, qn_g, kvn_g, wqb, tables, G, R, q_transposed):
    nseq, T, D = x.shape
    nh = C_HEADS
    cos_t, sin_t, cost_t, sint_t = tables
    tab = pl.BlockSpec((1, R, C_HEAD_W), lambda i, j: (0, j, 0))
    tabt = pl.BlockSpec((C_ROPE, R), lambda i, j: (0, j))
    if q_transposed:
        assert G == 1
        wqb = wqb.T
        q_spec = pl.BlockSpec((1, nh, C_HEAD_W, R), lambda i, j: (i, 0, 0, j))
        q_shape = jax.ShapeDtypeStruct((nseq, nh, C_HEAD_W, T), BF)
    else:
        q_spec = pl.BlockSpec((G, nh, R, C_HEAD_W), lambda i, j: (i, 0, j, 0))
        q_shape = jax.ShapeDtypeStruct((nseq, nh, T, C_HEAD_W), BF)
    return pl.pallas_call(
        functools.partial(_proj_c_body, q_transposed=q_transposed),
        grid=(nseq // G, T // R),
        in_specs=[pl.BlockSpec((G, R, D), lambda i, j: (i, j, 0)),
                  _mod_spec(G, blk_off, 3, D), _mod_spec(G, blk_off, 4, D), _const_spec((1, D)),
                  _const_spec(win.shape), _const_spec((1, C_Q_LORA)), _const_spec((1, C_KV_LORA)),
                  _const_spec(wqb.shape), tab, tab, tabt, tabt],
        out_specs=[q_spec,
                   pl.BlockSpec((G, R, C_KV_LORA), lambda i, j: (i, j, 0)),
                   pl.BlockSpec((G, R, C_ROPE), lambda i, j: (i, j, 0))],
        out_shape=[q_shape,
                   jax.ShapeDtypeStruct((nseq, T, C_KV_LORA), F32),
                   jax.ShapeDtypeStruct((nseq, T, C_ROPE), F32)],
        compiler_params=_cparams(2),
        name="proj_c",
    )(x, mods, mods, g.reshape(1, D), win, qn_g.reshape(1, -1), kvn_g.reshape(1, -1), wqb,
      cos_t, sin_t, cost_t, sint_t)


def _kvexp_body(kv_ref, kr_ref, wk_ref, wvt_ref, place_ref, ones_ref, k_ref, vt_ref):
    kv = kv_ref[0].astype(BF)
    krf = jnp.dot(kr_ref[0].astype(BF), place_ref[...], preferred_element_type=F32)
    kx = jnp.dot(kv, wk_ref[...], preferred_element_type=F32)
    for h in range(k_ref.shape[1]):
        k_ref[0, h] = (kx[:, h * C_HEAD_W:(h + 1) * C_HEAD_W] + krf).astype(BF)
    vt = lax.dot_general(wvt_ref[...], kv, (((1,), (1,)), ((), ())), preferred_element_type=F32)
    vt_ref[0] = (vt + ones_ref[...]).astype(BF)


def _kv_expand(kv, kr, wk, wvt, R):
    nseq, T, _ = kv.shape
    nh = wk.shape[1] // C_HEAD_W
    place = np.zeros((C_ROPE, C_HEAD_W), np.float32)
    place[np.arange(C_ROPE), C_NOPE + np.arange(C_ROPE)] = 1.0
    ones = np.zeros((nh, C_VT_ROWS, 1), np.float32)
    ones[:, C_V] = 1.0
    ones = ones.reshape(nh * C_VT_ROWS, 1)
    return pl.pallas_call(
        _kvexp_body,
        grid=(nseq, T // R),
        in_specs=[pl.BlockSpec((1, R, C_KV_LORA), lambda b, j: (b, j, 0)),
                  pl.BlockSpec((1, R, C_ROPE), lambda b, j: (b, j, 0)),
                  _const_spec(wk.shape), _const_spec(wvt.shape), _const_spec(place.shape),
                  _const_spec(ones.shape)],
        out_specs=[pl.BlockSpec((1, nh, R, C_HEAD_W), lambda b, j: (b, 0, j, 0)),
                   pl.BlockSpec((1, nh * C_VT_ROWS, R), lambda b, j: (b, 0, j))],
        out_shape=[jax.ShapeDtypeStruct((nseq, nh, T, C_HEAD_W), BF),
                   jax.ShapeDtypeStruct((nseq, nh * C_VT_ROWS, T), BF)],
        compiler_params=_cparams(2),
        name="kv_expand",
    )(kv, kr, wk, wvt, jnp.asarray(place, BF), jnp.asarray(ones))


def _flash_body(qi_ref, kj_ref, q_ref, k_ref, vt_ref, o_ref, acc_ref, m_ref,
                *, T, SK, QW):
    t = pl.program_id(1)
    i = qi_ref[t]
    j = kj_ref[t]
    nh = q_ref.shape[1]
    n_stripes = T // QW

    @pl.when(j == 0)
    def _():
        m_ref[...] = jnp.full(m_ref.shape, NEG, F32)
        acc_ref[...] = jnp.zeros(acc_ref.shape, F32)

    def sweep(diag):
        tiles = []
        for r in range(n_stripes):
            for c in range(T // SK):
                if diag and c * SK >= (r + 1) * QW:
                    continue
                tiles.append((c, r, diag and (c + 1) * SK > r * QW + CHUNK))

        def head_group(g, carry):
            heads = [g * MLA_HU + u for u in range(MLA_HU)]
            stream = [(u, c, r, mask) for u in range(MLA_HU) for (c, r, mask) in tiles]
            state = {(u, r): (m_ref[heads[u], :, r * QW:(r + 1) * QW], acc_ref[heads[u], :, r * QW:(r + 1) * QW])
                     for u in range(MLA_HU) for r in range(n_stripes)}

            def scores(u, c, r, mask):
                s = jnp.dot(k_ref[0, heads[u], c * SK:(c + 1) * SK, :], q_ref[0, heads[u], :, r * QW:(r + 1) * QW],
                            preferred_element_type=F32)
                if mask:
                    kpos = lax.broadcasted_iota(jnp.int32, (SK, 1), 0) + c * SK
                    qpos = lax.broadcasted_iota(jnp.int32, (1, QW), 1) + r * QW
                    s = jnp.where(kpos // CHUNK <= qpos // CHUNK, s, NEG)
                return s

            pending = [scores(*tl) for tl in stream[:MLA_AHEAD]]
            for n, (u, c, r, _) in enumerate(stream):
                s = pending.pop(0)
                if n + MLA_AHEAD < len(stream):
                    pending.append(scores(*stream[n + MLA_AHEAD]))
                m, acc = state[u, r]
                m_new = jnp.maximum(m, jnp.max(s, axis=0, keepdims=True))
                p = jnp.exp2(s - m_new).astype(BF)
                acc = jnp.exp2(m - m_new) * acc + jnp.dot(vt_ref[0, heads[u], :, c * SK:(c + 1) * SK], p,
                                                          preferred_element_type=F32)
                state[u, r] = (m_new, acc)
            for (u, r), (m, acc) in state.items():
                m_ref[heads[u], :, r * QW:(r + 1) * QW] = m
                acc_ref[heads[u], :, r * QW:(r + 1) * QW] = acc
            return carry
        lax.fori_loop(0, nh // MLA_HU, head_group, 0)

    pl.when(j == i)(lambda: sweep(True))
    pl.when(j != i)(lambda: sweep(False))

    @pl.when(j == i)
    def _():
        for p in range(nh // 2):
            halves = [acc_ref[h, :C_V, :] * (1.0 / acc_ref[h, C_V:C_V + 1, :]) for h in (2 * p, 2 * p + 1)]
            o_ref[0, :, p * LANES:(p + 1) * LANES] = jnp.concatenate(halves, axis=0).T.astype(BF)


def _mla_attn(q, k, vt, *, T, SK, QW):
    nseq, nh, _, S = q.shape
    assert S % T == 0 and T % QW == 0 and T % SK == 0 and SK % CHUNK == 0 and QW % CHUNK == 0
    Tq = Tk = T
    pairs = [(i, j) for i in range(S // T) for j in range(i + 1)]
    qi = jnp.asarray([p[0] for p in pairs], jnp.int32)
    kj = jnp.asarray([p[1] for p in pairs], jnp.int32)
    grid_spec = pltpu.PrefetchScalarGridSpec(
        num_scalar_prefetch=2,
        grid=(nseq, len(pairs)),
        in_specs=[pl.BlockSpec((1, nh, C_HEAD_W, Tq), lambda b, t, qi, kj: (b, 0, 0, qi[t])),
                  pl.BlockSpec((1, nh, Tk, C_HEAD_W), lambda b, t, qi, kj: (b, 0, kj[t], 0)),
                  pl.BlockSpec((1, nh, C_VT_ROWS, Tk), lambda b, t, qi, kj: (b, 0, 0, kj[t]))],
        out_specs=pl.BlockSpec((1, Tq, nh * C_V), lambda b, t, qi, kj: (b, qi[t], 0)),
        scratch_shapes=[pltpu.VMEM((nh, C_VT_ROWS, Tq), F32), pltpu.VMEM((nh, 1, Tq), F32)],
    )
    return pl.pallas_call(
        functools.partial(_flash_body, T=T, SK=SK, QW=QW),
        grid_spec=grid_spec,
        out_shape=jax.ShapeDtypeStruct((nseq, S, nh * C_V), BF),
        compiler_params=_cparams(2),
        name="mla_attn",
    )(qi, kj, q, k, vt)


def _decode_body(q_ref, ckv_ref, ckr_ref, nkv_ref, nkr_ref, wk_ref, wv_ref, place_ref, o_ref, *, n_pad):
    nh, tq, _ = q_ref.shape[1:]
    n_keys = ckv_ref.shape[1] + nkv_ref.shape[1]
    kv = jnp.concatenate([ckv_ref[0].astype(BF), nkv_ref[0].astype(BF),
                          jnp.zeros((n_pad - n_keys, C_KV_LORA), BF)], axis=0)
    kr = jnp.concatenate([ckr_ref[0].astype(BF), nkr_ref[0].astype(BF),
                          jnp.zeros((n_pad - n_keys, C_ROPE), BF)], axis=0)
    krf = jnp.dot(kr, place_ref[...], preferred_element_type=F32).astype(BF)
    keys = jnp.concatenate([kv, krf], axis=1)
    q = q_ref[0]
    q_lat = jnp.concatenate([jnp.dot(q[h], wk_ref[h], preferred_element_type=F32) for h in range(nh)], axis=0)
    queries = jnp.concatenate([q_lat.astype(BF), q.reshape(nh * tq, C_HEAD_W)], axis=1)
    s = lax.dot_general(queries, keys, (((1,), (1,)), ((), ())), preferred_element_type=F32)
    s = jnp.where(lax.broadcasted_iota(jnp.int32, (1, n_pad), 1) < n_keys, s, NEG)
    m = jnp.max(s, axis=-1, keepdims=True)
    e = jnp.exp2(s - m)
    den = jnp.sum(e, axis=-1, keepdims=True)
    lat = (jnp.dot(e.astype(BF), kv, preferred_element_type=F32) * (1.0 / den)).astype(BF)
    for p in range(nh // 2):
        pair = jnp.concatenate([lat[2 * p * tq:(2 * p + 1) * tq], lat[(2 * p + 1) * tq:(2 * p + 2) * tq]], axis=1)
        o_ref[0, :, p * LANES:(p + 1) * LANES] = jnp.dot(pair, wv_ref[p], preferred_element_type=F32).astype(BF)


def _mla_decode(q, cache_kv, cache_kr, new_kv, new_kr, wk_heads, wv_pairs):
    nseq, nh, tq, _ = q.shape
    past = cache_kv.shape[1]
    n_pad = -(-(past + tq) // LANES) * LANES
    place = np.zeros((C_ROPE, C_HEAD_W), np.float32)
    place[np.arange(C_ROPE), C_NOPE + np.arange(C_ROPE)] = 1.0
    per_seq = lambda a: pl.BlockSpec((1,) + a.shape[1:], lambda b: (b,) + (0,) * (a.ndim - 1))
    return pl.pallas_call(
        functools.partial(_decode_body, n_pad=n_pad),
        grid=(nseq,),
        in_specs=[per_seq(q), per_seq(cache_kv), per_seq(cache_kr), per_seq(new_kv), per_seq(new_kr),
                  _const_spec(wk_heads.shape), _const_spec(wv_pairs.shape), _const_spec(place.shape)],
        out_specs=pl.BlockSpec((1, tq, nh * C_V), lambda b: (b, 0, 0)),
        out_shape=jax.ShapeDtypeStruct((nseq, tq, nh * C_V), BF),
        compiler_params=_cparams(1),
        name="mla_decode",
    )(q, cache_kv, cache_kr, new_kv, new_kr, wk_heads, wv_pairs, jnp.asarray(place, BF))


def _rope_tables(pos):
    half = C_ROPE // 2
    inv = (np.float32(ROPE_BASE) ** (-np.arange(half, dtype=np.float32) / np.float32(half))).astype(np.float32)
    ang = (pos.astype(np.float32)[:, None] * inv[None, :]).astype(np.float32).astype(np.float64)
    n = pos.shape[0]
    cos = np.zeros((1, n, C_HEAD_W), np.float32)
    sin = np.zeros((1, n, C_HEAD_W), np.float32)
    cos[0, :, :C_NOPE] = 1.0
    cos[0, :, C_NOPE:C_NOPE + C_ROPE] = np.concatenate([np.cos(ang), np.cos(ang)], axis=1)
    sin[0, :, C_NOPE:C_NOPE + C_ROPE] = np.concatenate([np.sin(ang), np.sin(ang)], axis=1)
    cos32, sin32 = cos[0, :, C_NOPE:C_NOPE + C_ROPE].T, sin[0, :, C_NOPE:C_NOPE + C_ROPE].T
    return jnp.asarray(cos), jnp.asarray(sin), jnp.asarray(cos32), jnp.asarray(sin32)


def _rot_half_cols(w):
    half = w.shape[-1] // 2
    return jnp.concatenate([-w[..., half:], w[..., :half]], axis=-1)


def kernel(x_prompt, x_sample, c_prompt, c_sample, cache_a_k, cache_a_v, cache_b_k, cache_b_v, cache_c_kv, cache_c_kr, w_ada, b_ada, norm_g, final_norm_g, ffn_w_gate, ffn_w_up, ffn_w_down, w_in_ab, w_out_ab, rel_bias_a, t5_bias, sinks_b, w_in_c, c_q_norm_g, c_kv_norm_g, w_qb, w_kvb, w_out_c):
    nb, S, D = x_prompt.shape
    ns, TS, _ = x_sample.shape
    la_c, lb_c = cache_a_k.shape[2], cache_b_k.shape[2]
    past = cache_c_kv.shape[2]
    la_p, lb_p = min(A_PREV_CHUNKS * CHUNK, S), min(B_PREV_CHUNKS * CHUNK, S)
    assert la_p <= ROW_TILE and S % MLA_T == 0 and S % ROW_TILE == 0 and TS % 8 == 0

    n_cond = ns + nb
    n_cond_pad = -(-n_cond // 8) * 8
    c_all = jnp.zeros((n_cond_pad, D), F32).at[:ns].set(c_sample).at[ns:n_cond].set(c_prompt)
    mods_all = _adaln(c_all, w_ada, b_ada)
    groups = {
        "p": dict(x=x_prompt, blk_off=ns, G=1, R=ROW_TILE),
        "s": dict(x=x_sample, blk_off=0, G=ns, R=TS),
    }

    ffn_w = tuple(w.astype(BF) for w in (ffn_w_gate, ffn_w_up, ffn_w_down))
    w_ab = w_in_ab[0].astype(BF)
    w_oa, w_ob = w_out_ab[0, :A_W].astype(BF), w_out_ab[0, A_W:].astype(BF)
    pair_a = tuple(range(A_HEADS // 2))
    pair_b = tuple(p // (B_Q_HEADS // B_KV_HEADS // 2) for p in range(B_Q_HEADS // 2))
    a_back, b_back = A_PREV_CHUNKS * CHUNK, B_PREV_CHUNKS * CHUNK
    a_prev_blocks, b_prev_blocks = -(-a_back // BAND_TQ), -(-b_back // BAND_TQ)
    a_prev_rows, b_prev_rows = a_prev_blocks * BAND_TQ, b_prev_blocks * BAND_TQ
    assert BAND_TQ % BAND_WIN == 0 and BAND_WIN % CHUNK == 0
    win_starts = range(0, BAND_TQ, BAND_WIN)
    windows_a = tuple((qs, BAND_WIN, a_prev_rows + qs - a_back, a_back + BAND_WIN) for qs in win_starts)
    windows_b = tuple((qs, BAND_WIN, b_prev_rows + qs - b_back, b_back + BAND_WIN) for qs in win_starts)
    bias_ap = _bias_tile(rel_bias_a[0], "clip", BAND_WIN, a_back + BAND_WIN, a_back, A_PREV_CHUNKS)
    bias_bp = _bias_tile(t5_bias, "t5", BAND_WIN, b_back + BAND_WIN, b_back, B_PREV_CHUNKS)
    bias_as = _bias_tile(rel_bias_a[0], "clip", TS, la_c + TS, la_c, None)
    bias_bs = _bias_tile(t5_bias, "t5", TS, lb_c + TS, lb_c, None)
    dup = lambda c: jnp.repeat(c, 2, axis=2).reshape(c.shape[0], c.shape[1], 2 * BKV_W)
    cak = cache_a_k[0].reshape(ns, la_c, A_W)
    cav = cache_a_v[0].reshape(ns, la_c, A_W)
    cbk, cbv = dup(cache_b_k[0]), dup(cache_b_v[0])

    mods = mods_all[0].reshape(n_cond_pad, 1, -1)
    state0 = {}
    for name, gr in groups.items():
        x, off, G, R = gr["x"], gr["blk_off"] // gr["G"], gr["G"], gr["R"]
        x = _ffn(x, mods, off, 0, norm_g[0, 0], ffn_w, 0, 0, G, R)
        ta, tb = (la_p, lb_p) if name == "p" else (TS, TS)
        qa, ka, va, qb, kb, vb, ka32, va32, kb32, vb32 = _proj_ab(x, mods, off, norm_g[0, 1], w_ab, G, R, ta, tb)
        if name == "p":
            prev = lambda arr, n: [(arr, BAND_TQ, d - n) for d in range(n + 1)]
            oa = _band_attn(qa, prev(ka, a_prev_blocks), prev(va, a_prev_blocks), bias_ap, None,
                            Tq=BAND_TQ, n_prev_rows=a_prev_rows, pair_map=pair_a, windows=windows_a)
            ob = _band_attn(qb, prev(kb, b_prev_blocks), prev(vb, b_prev_blocks), bias_bp, sinks_b[0],
                            Tq=BAND_TQ, n_prev_rows=b_prev_rows, pair_map=pair_b, windows=windows_b)
        else:
            oa = _band_attn(qa, [(cak, la_c, None), (ka, TS, None)], [(cav, la_c, None), (va, TS, None)],
                            bias_as, None, Tq=TS, n_prev_rows=0, pair_map=pair_a)
            ob = _band_attn(qb, [(cbk, lb_c, None), (kb, TS, None)], [(cbv, lb_c, None), (vb, TS, None)],
                            bias_bs, sinks_b[0], Tq=TS, n_prev_rows=0, pair_map=pair_b)
        x = _ffn(x, mods, off, 6, norm_g[0, 2], ffn_w, 0, 1, G, R, attn=([oa, ob], [w_oa, w_ob]))
        gr["x"] = x
        n_out = x.shape[0]
        state0[name] = (ka32.reshape(1, n_out, ta, A_HEADS, HEAD_DIM), va32.reshape(1, n_out, ta, A_HEADS, HEAD_DIM),
                        kb32.reshape(1, n_out, tb, B_KV_HEADS, HEAD_DIM), vb32.reshape(1, n_out, tb, B_KV_HEADS, HEAD_DIM))

    hw = C_NOPE + C_ROPE
    w_in = w_in_c[0]
    w_kr = w_in[:, C_Q_LORA + C_KV_LORA:]
    win_ext = jnp.concatenate([w_in[:, :C_Q_LORA + C_KV_LORA], jnp.zeros((D, C_NOPE), F32),
                               w_kr, _rot_half_cols(w_kr)], axis=1).astype(BF)
    wq3 = w_qb[0].reshape(C_Q_LORA, C_HEADS, hw)
    wqb_ext = jnp.concatenate([wq3, _rot_half_cols(wq3[..., C_NOPE:])], axis=-1
                              ).reshape(C_Q_LORA, C_HEADS * C_HEAD_W).astype(BF)
    wkv3 = w_kvb[0].reshape(C_KV_LORA, C_HEADS, C_NOPE + C_V)
    wk_ext = jnp.concatenate([wkv3[..., :C_NOPE], jnp.zeros((C_KV_LORA, C_HEADS, C_HEAD_W - C_NOPE), F32)],
                             axis=-1).reshape(C_KV_LORA, C_HEADS * C_HEAD_W).astype(BF)
    wvt = jnp.concatenate([wkv3[..., C_NOPE:], jnp.zeros((C_KV_LORA, C_HEADS, C_VT_ROWS - C_V), F32)], axis=-1
                          ).reshape(C_KV_LORA, C_HEADS * C_VT_ROWS).T.astype(BF)
    w_oc = w_out_c[0].astype(BF)
    wk_heads = jnp.concatenate([jnp.transpose(wkv3[..., :C_NOPE], (1, 2, 0)),
                                jnp.zeros((C_HEADS, C_HEAD_W - C_NOPE, C_KV_LORA), F32)], axis=1).astype(BF)
    wv4 = jnp.transpose(wkv3[..., C_NOPE:], (1, 0, 2)).reshape(C_HEADS // 2, 2, C_KV_LORA, C_V)
    zv = jnp.zeros((C_HEADS // 2, C_KV_LORA, C_V), F32)
    wv_pairs = jnp.concatenate([jnp.concatenate([wv4[:, 0], zv], axis=2),
                                jnp.concatenate([zv, wv4[:, 1]], axis=2)], axis=1).astype(BF)
    tables = {"p": _rope_tables(np.arange(S)), "s": _rope_tables(past + np.arange(TS))}

    mods = mods_all[1].reshape(n_cond_pad, 1, -1)
    state1 = {}
    for name, gr in groups.items():
        x, off, G, R = gr["x"], gr["blk_off"] // gr["G"], gr["G"], gr["R"]
        x = _ffn(x, mods, off, 0, norm_g[1, 0], ffn_w, 1, 0, G, R)
        q, kv32, kr32 = _proj_c(x, mods, off, norm_g[1, 1], win_ext, c_q_norm_g[0], c_kv_norm_g[0],
                                wqb_ext, tables[name], G, R, q_transposed=(name == "p"))
        if name == "p":
            k, vt = _kv_expand(kv32, kr32, wk_ext, wvt, ROW_TILE)
            o = _mla_attn(q, k, vt.reshape(nb, C_HEADS, C_VT_ROWS, S), T=MLA_T, SK=MLA_SK, QW=MLA_QW)
        else:
            o = _mla_decode(q, cache_c_kv[0], cache_c_kr[0], kv32, kr32, wk_heads, wv_pairs)
        x = _ffn(x, mods, off, 6, norm_g[1, 2], ffn_w, 1, 1, G, R, final_g=final_norm_g, attn=([o], [w_oc]))
        gr["x"] = x
        state1[name] = (kv32[None], kr32[None])

    return (groups["p"]["x"], groups["s"]["x"],
            *state0["p"], *state1["p"], *state0["s"], *state1["s"])
```

```python
import functools
import math

import numpy as np
import jax
import jax.numpy as jnp
from jax import lax
from jax.experimental import pallas as pl
from jax.experimental.pallas import tpu as pltpu

F32 = jnp.float32
BF = jnp.bfloat16

CHUNK = 64
HEAD_DIM = 64
A_HEADS = 8
A_PREV_CHUNKS = 8
A_REL_CLIP = 128
B_Q_HEADS = 8
B_KV_HEADS = 2
B_PREV_CHUNKS = 2
T5_BUCKETS = 32
T5_MAX_DIST = 128
C_HEADS = 16
C_Q_LORA = 384
C_KV_LORA = 256
C_NOPE = 64
C_ROPE = 32
C_V = 64
ROPE_BASE = 10000.0
EPS = 1e-6
NEG = -1e30
A_W = A_HEADS * HEAD_DIM
BQ_W = B_Q_HEADS * HEAD_DIM
BKV_W = B_KV_HEADS * HEAD_DIM

LANES = 128
VMEM_LIMIT = 56 * 1024 * 1024

ROW_TILE = 1024
FF_CHUNK = 256
BAND_TQ = 512
BAND_WIN = 128
BAND_AHEAD = 2
LOG2E = math.log2(math.e)
MLA_T = 1024
MLA_SK = 256
MLA_QW = 256
MLA_HU = 8
MLA_AHEAD = 5
C_VT_ROWS = 80
C_IN_EXT = 768
C_HEAD_W = 128


def _cparams(n_axes, vmem=VMEM_LIMIT):
    return pltpu.CompilerParams(dimension_semantics=("arbitrary",) * n_axes,
                                vmem_limit_bytes=vmem)


def _const_spec(shape):
    n = len(shape)
    return pl.BlockSpec(shape, lambda *_: (0,) * n)


def _rms(x):
    return x * lax.rsqrt(jnp.mean(x * x, axis=-1, keepdims=True) + EPS)


def _norm_mod(x, g, shift, scale):
    return (_rms(x) * g) * (1.0 + scale) + shift


def _mod_spec(G, blk_off, k, D):
    return pl.BlockSpec((G, 1, D), lambda i, j: (blk_off + i, 0, k))


def _adaln_body(c_ref, w_ref, b_ref, o_ref):
    c = c_ref[...]
    s = (c * jax.nn.sigmoid(c)).astype(BF)
    o_ref[0] = jnp.dot(s, w_ref[0].astype(BF), preferred_element_type=F32) + b_ref[0]


def _adaln(c_all, w_ada, b_ada):
    L, D, N = w_ada.shape
    R = c_all.shape[0]
    tn = 1024
    return pl.pallas_call(
        _adaln_body,
        grid=(L, N // tn),
        in_specs=[pl.BlockSpec((R, D), lambda l, j: (0, 0)),
                  pl.BlockSpec((1, D, tn), lambda l, j: (l, 0, j)),
                  pl.BlockSpec((1, 1, tn), lambda l, j: (l, 0, j))],
        out_specs=pl.BlockSpec((1, R, tn), lambda l, j: (l, 0, j)),
        out_shape=jax.ShapeDtypeStruct((L, R, N), F32),
        compiler_params=_cparams(2),
        name="adaln",
    )(c_all, w_ada, b_ada.reshape(L, 1, N))


def _ffn_body(*refs, n_attn, final):
    attn_refs, wo_refs = refs[:n_attn], refs[n_attn:2 * n_attn]
    refs = refs[2 * n_attn:]
    if n_attn:
        ga_ref, refs = refs[0], refs[1:]
    x_ref, sh_ref, sc_ref, gt_ref, g_ref, wg_ref, wu_ref, wd_ref = refs[:8]
    if final:
        fg_ref, o_ref, a_ref = refs[8:]
    else:
        o_ref, a_ref = refs[8:]
    x = x_ref[...]
    G, R, D = x.shape
    F = wg_ref.shape[1]
    if n_attn:
        mix = None
        for at_ref, wo_ref in zip(attn_refs, wo_refs):
            y = jnp.dot(at_ref[...].reshape(G * R, at_ref.shape[2]), wo_ref[...], preferred_element_type=F32)
            mix = y if mix is None else mix + y
        x = x + ga_ref[...] * mix.reshape(G, R, D)
    hb = _norm_mod(x, g_ref[...], sh_ref[...], sc_ref[...]).reshape(G * R, D).astype(BF)
    for c in range(F // FF_CHUNK):
        lo, hi = c * FF_CHUNK, (c + 1) * FF_CHUNK
        g = jnp.dot(hb, wg_ref[:, lo:hi], preferred_element_type=F32)
        u = jnp.dot(hb, wu_ref[:, lo:hi], preferred_element_type=F32)
        a_ref[:, lo:hi] = (g * jax.nn.sigmoid(g) * u).astype(BF)
    ff = jnp.dot(a_ref[...], wd_ref[...], preferred_element_type=F32)
    y = x + (0.5 * gt_ref[...]) * ff.reshape(G, R, D)
    if final:
        y = _rms(y) * fg_ref[...]
    o_ref[...] = y


def _ffn(x, mods, blk_off, kbase, g, weights, layer, which, G, R, final_g=None, attn=None):
    nseq, T, D = x.shape
    wg, wu, wd = weights
    F = wg.shape[-1]
    final = final_g is not None
    row = lambda W: pl.BlockSpec((G, R, W), lambda i, j: (i, j, 0))
    stacked = lambda w: pl.BlockSpec((None, None) + w.shape[2:], lambda i, j: (layer, which, 0, 0),
                                     pipeline_mode=pl.Buffered(1))
    in_specs, args = [], []
    n_attn = 0
    if attn is not None:
        os_, ws = attn
        n_attn = len(os_)
        in_specs += [row(o.shape[2]) for o in os_] + [_const_spec(w.shape) for w in ws]
        in_specs.append(_mod_spec(G, blk_off, 5, D))
        args += [*os_, *ws, mods]
    in_specs += [row(D), _mod_spec(G, blk_off, kbase, D), _mod_spec(G, blk_off, kbase + 1, D),
                 _mod_spec(G, blk_off, kbase + 2, D), _const_spec((1, D)), stacked(wg), stacked(wu), stacked(wd)]
    args += [x, mods, mods, mods, g.reshape(1, D), wg, wu, wd]
    if final:
        in_specs.append(_const_spec((1, D)))
        args.append(final_g.reshape(1, D))
    return pl.pallas_call(
        functools.partial(_ffn_body, n_attn=n_attn, final=final),
        grid=(nseq // G, T // R),
        in_specs=in_specs,
        out_specs=pl.BlockSpec((G, R, D), lambda i, j: (i, j, 0)),
        out_shape=jax.ShapeDtypeStruct((nseq, T, D), F32),
        scratch_shapes=[pltpu.VMEM((G * R, F), BF)],
        compiler_params=_cparams(2),
        name="ffn",
    )(*args)


def _dup_halves(k):
    rolled = pltpu.roll(k, HEAD_DIM, 1)
    lo = lax.broadcasted_iota(jnp.int32, k.shape, 1) < HEAD_DIM
    return jnp.concatenate([jnp.where(lo, k, rolled), jnp.where(lo, rolled, k)], axis=1)


def _proj_ab_body(x_ref, sh_ref, sc_ref, g_ref, w_ref,
                  qa_ref, ka_ref, va_ref, qb_ref, kb_ref, vb_ref,
                  ka32_ref, va32_ref, kb32_ref, vb32_ref, *, ta, tb):
    x = x_ref[...]
    G, R, D = x.shape
    hb = _norm_mod(x, g_ref[...], sh_ref[...], sc_ref[...]).reshape(G * R, D).astype(BF)
    res = jnp.dot(hb, w_ref[...], preferred_element_type=F32)
    o_ka, o_va, o_qb, o_kb, o_vb = A_W, 2 * A_W, 3 * A_W, 3 * A_W + BQ_W, 3 * A_W + BQ_W + BKV_W
    qscale = HEAD_DIM ** -0.5 * LOG2E
    qa_ref[...] = (res[:, :o_ka] * qscale).astype(BF).reshape(G, R, A_W)
    ka_ref[...] = res[:, o_ka:o_va].astype(BF).reshape(G, R, A_W)
    va_ref[...] = res[:, o_va:o_qb].astype(BF).reshape(G, R, A_W)
    qb_ref[...] = (res[:, o_qb:o_kb] * qscale).astype(BF).reshape(G, R, BQ_W)
    kb = res[:, o_kb:o_vb]
    vb = res[:, o_vb:o_vb + BKV_W]
    kb_ref[...] = _dup_halves(kb).astype(BF).reshape(G, R, 2 * BKV_W)
    vb_ref[...] = _dup_halves(vb).astype(BF).reshape(G, R, 2 * BKV_W)

    @pl.when(pl.program_id(1) == pl.num_programs(1) - 1)
    def _():
        ka32_ref[...] = res[:, o_ka:o_va].reshape(G, R, A_W)[:, R - ta:, :]
        va32_ref[...] = res[:, o_va:o_qb].reshape(G, R, A_W)[:, R - ta:, :]
        kb32_ref[...] = kb.reshape(G, R, BKV_W)[:, R - tb:, :]
        vb32_ref[...] = vb.reshape(G, R, BKV_W)[:, R - tb:, :]


def _proj_ab(x, mods, blk_off, g, w, G, R, ta, tb):
    nseq, T, D = x.shape
    row = lambda W: pl.BlockSpec((G, R, W), lambda i, j: (i, j, 0))
    tail = lambda t, W: pl.BlockSpec((G, t, W), lambda i, j: (i, 0, 0))
    bshape = lambda W: jax.ShapeDtypeStruct((nseq, T, W), BF)
    return pl.pallas_call(
        functools.partial(_proj_ab_body, ta=ta, tb=tb),
        grid=(nseq // G, T // R),
        in_specs=[row(D), _mod_spec(G, blk_off, 3, D), _mod_spec(G, blk_off, 4, D),
                  _const_spec((1, D)), _const_spec(w.shape)],
        out_specs=[row(A_W), row(A_W), row(A_W), row(BQ_W), row(2 * BKV_W), row(2 * BKV_W),
                   tail(ta, A_W), tail(ta, A_W), tail(tb, BKV_W), tail(tb, BKV_W)],
        out_shape=[bshape(A_W), bshape(A_W), bshape(A_W), bshape(BQ_W), bshape(2 * BKV_W), bshape(2 * BKV_W),
                   jax.ShapeDtypeStruct((nseq, ta, A_W), F32), jax.ShapeDtypeStruct((nseq, ta, A_W), F32),
                   jax.ShapeDtypeStruct((nseq, tb, BKV_W), F32), jax.ShapeDtypeStruct((nseq, tb, BKV_W), F32)],
        compiler_params=_cparams(2),
        name="proj_ab",
    )(x, mods, mods, g.reshape(1, D), w)


def _t5_bucket_np(rel):
    nb = T5_BUCKETS // 2
    ret = np.where(rel > 0, nb, 0)
    n = np.abs(rel)
    max_exact = nb // 2
    ratio = np.maximum(n, 1).astype(np.float32) / np.float32(max_exact)
    large = max_exact + (np.log(ratio).astype(np.float32) / np.float32(math.log(T5_MAX_DIST / max_exact))
                         * np.float32(nb - max_exact)).astype(np.int32)
    large = np.minimum(large, nb - 1)
    return ret + np.where(n < max_exact, n, large)


def _bias_body(thi_ref, tlo_ref, idx_ref, o_ref, *, Tq, span, prev_chunks, band_chunks):
    E = thi_ref.shape[1]
    L = idx_ref.shape[1]
    onehot = (lax.broadcasted_iota(jnp.int32, (E, L), 0) == idx_ref[...]).astype(BF)
    t = (jnp.dot(thi_ref[...], onehot, preferred_element_type=F32)
         + jnp.dot(tlo_ref[...], onehot, preferred_element_type=F32)) * LOG2E
    if band_chunks is not None:
        qc = lax.broadcasted_iota(jnp.int32, (Tq, span), 0) // CHUNK
        kc = lax.broadcasted_iota(jnp.int32, (Tq, span), 1) // CHUNK - prev_chunks
        valid = (kc <= qc) & (kc >= qc - band_chunks)
    for h in range(o_ref.shape[0]):
        x = jnp.broadcast_to(t[h:h + 1, :], (Tq, L))
        y = pltpu.roll(x, L - Tq + 1, 1, stride=1, stride_axis=0)[:, :span]
        if band_chunks is not None:
            y = jnp.where(valid, y, NEG)
        o_ref[h] = y


def _bias_tile(table, kind, Tq, span, n_prev_rows, band_chunks):
    H = table.shape[1]
    L = -(-(Tq + span - 1) // LANES) * LANES
    rel = np.arange(L) - (Tq - 1) - n_prev_rows
    if kind == "clip":
        idx = np.clip(rel, -A_REL_CLIP, A_REL_CLIP) + A_REL_CLIP
    else:
        idx = _t5_bucket_np(rel)
    E = -(-table.shape[0] // LANES) * LANES
    tt = jnp.zeros((H, E), F32).at[:, :table.shape[0]].set(table.T.astype(F32))
    thi = tt.astype(BF)
    tlo = (tt - thi.astype(F32)).astype(BF)
    return pl.pallas_call(
        functools.partial(_bias_body, Tq=Tq, span=span, prev_chunks=n_prev_rows // CHUNK,
                          band_chunks=band_chunks),
        out_shape=jax.ShapeDtypeStruct((H, Tq, span), F32),
        compiler_params=pltpu.CompilerParams(vmem_limit_bytes=VMEM_LIMIT),
        name="bias_tile",
    )(thi, tlo, jnp.asarray(idx.reshape(1, L), jnp.int32))


def _band_body(*refs, n_parts, n_prev_rows, pair_map, has_sink, Tq, windows):
    q_ref = refs[0]
    k_refs = refs[1:1 + n_parts]
    v_refs = refs[1 + n_parts:1 + 2 * n_parts]
    bias_ref = refs[1 + 2 * n_parts]
    sink_ref = refs[2 + 2 * n_parts] if has_sink else None
    o_ref = refs[-1]
    i = pl.program_id(1)
    k = jnp.concatenate([r[0].astype(BF) for r in k_refs], axis=0)
    v = jnp.concatenate([r[0].astype(BF) for r in v_refs], axis=0)
    q = q_ref[0]
    span = k.shape[0]
    n_heads = 2 * (q.shape[1] // LANES)
    lane_lo = lax.broadcasted_iota(jnp.int32, (1, LANES), 1) < HEAD_DIM

    items = [(w, h) for w in range(len(windows)) for h in range(n_heads)]

    def sweep(mask_start):
        def scores(w, h):
            qs, qr, ks, kr = windows[w]
            p = h // 2
            qp = q[qs:qs + qr, p * LANES:(p + 1) * LANES]
            qm = jnp.where(lane_lo if h % 2 == 0 else jnp.logical_not(lane_lo), qp, jnp.zeros_like(qp))
            kp = k[ks:ks + kr, pair_map[p] * LANES:(pair_map[p] + 1) * LANES]
            s = lax.dot_general(qm, kp, (((1,), (1,)), ((), ())), preferred_element_type=F32) + bias_ref[h]
            if mask_start and ks < n_prev_rows:
                s = jnp.where(lax.broadcasted_iota(jnp.int32, (1, kr), 1) >= (n_prev_rows - ks - i * Tq), s, NEG)
            return s

        pending = [scores(*it) for it in items[:BAND_AHEAD]]
        halves = []
        for n, (w, h) in enumerate(items):
            qs, qr, ks, kr = windows[w]
            s = pending.pop(0)
            if n + BAND_AHEAD < len(items):
                pending.append(scores(*items[n + BAND_AHEAD]))
            p = h // 2
            vp = v[ks:ks + kr, pair_map[p] * LANES:(pair_map[p] + 1) * LANES]
            m = jnp.max(s, axis=-1, keepdims=True)
            if has_sink:
                sink = sink_ref[h] * LOG2E
                m = jnp.maximum(m, sink)
            e = jnp.exp2(s - m)
            den = jnp.sum(e, axis=-1, keepdims=True)
            if has_sink:
                den = den + jnp.exp2(sink - m)
            o = jnp.dot(e.astype(BF), vp, preferred_element_type=F32)
            halves.append(o * (1.0 / den))
            if h % 2 == 1:
                o_ref[0, qs:qs + qr, p * LANES:(p + 1) * LANES] = jnp.where(lane_lo, halves[0], halves[1]).astype(BF)
                halves = []

    if n_prev_rows:
        pl.when(i * Tq < n_prev_rows)(lambda: sweep(True))
        pl.when(i * Tq >= n_prev_rows)(lambda: sweep(False))
    else:
        sweep(False)


def _band_attn(q, kparts, vparts, bias, sinks, *, Tq, n_prev_rows, pair_map, windows=None):
    nseq, T, W = q.shape
    if windows is None:
        windows = ((0, Tq, 0, sum(p[1] for p in kparts)),)

    def part_spec(arr, rows, off):
        if off is None:
            return pl.BlockSpec((1, rows, arr.shape[2]), lambda b, i: (b, 0, 0))
        return pl.BlockSpec((1, rows, arr.shape[2]), lambda b, i: (b, jnp.maximum(i + off, 0), 0))

    in_specs = [pl.BlockSpec((1, Tq, W), lambda b, i: (b, i, 0))]
    in_specs += [part_spec(*p) for p in kparts] + [part_spec(*p) for p in vparts]
    in_specs.append(_const_spec(bias.shape))
    args = [q] + [p[0] for p in kparts] + [p[0] for p in vparts] + [bias]
    if sinks is not None:
        in_specs.append(pl.BlockSpec(memory_space=pltpu.SMEM))
        args.append(sinks.astype(F32))
    return pl.pallas_call(
        functools.partial(_band_body, n_parts=len(kparts), n_prev_rows=n_prev_rows,
                          pair_map=pair_map, has_sink=sinks is not None, Tq=Tq, windows=tuple(windows)),
        grid=(nseq, T // Tq),
        in_specs=in_specs,
        out_specs=pl.BlockSpec((1, Tq, W), lambda b, i: (b, i, 0)),
        out_shape=jax.ShapeDtypeStruct((nseq, T, W), BF),
        compiler_params=_cparams(2),
        name="band_attn",
    )(*args)


def _proj_c_body(x_ref, sh_ref, sc_ref, g_ref, win_ref, qn_ref, kvn_ref, wqb_ref, cos_ref, sin_ref,
                 cost_ref, sint_ref, *rest, q_transposed, expand):
    if expand:
        wk_ref, wvt_ref, ones_ref, q_ref, kv32_ref, kr32_ref, k_ref, vt_ref = rest
    else:
        q_ref, kv32_ref, kr32_ref = rest
    x = x_ref[...]
    G, R, D = x.shape
    hb = _norm_mod(x, g_ref[...], sh_ref[...], sc_ref[...]).reshape(G * R, D).astype(BF)
    res = jnp.dot(hb, win_ref[...], preferred_element_type=F32)
    q_lat = res[:, :C_Q_LORA]
    kv_lat = res[:, C_Q_LORA:C_Q_LORA + C_KV_LORA]
    krg = res[:, C_Q_LORA + C_KV_LORA:]
    cos = cos_ref[...]
    sin = sin_ref[...]
    qn = (_rms(q_lat) * qn_ref[...]).astype(BF)
    scale = (C_NOPE + C_ROPE) ** -0.5 * math.log2(math.e)
    nh = q_ref.shape[1]
    if q_transposed:
        qt = lax.dot_general(wqb_ref[...], qn, (((1,), (1,)), ((), ())), preferred_element_type=F32)
        cost, sint = cost_ref[...], sint_ref[...]
        for h in range(nh):
            blk = qt[h * C_HEAD_W:(h + 1) * C_HEAD_W]
            roped = blk[C_NOPE:C_NOPE + C_ROPE] * cost + blk[C_NOPE + C_ROPE:] * sint
            q_ref[0, h] = (jnp.concatenate([blk[:C_NOPE], roped, jnp.zeros_like(roped)], axis=0) * scale).astype(BF)
    else:
        qr = jnp.dot(qn, wqb_ref[...], preferred_element_type=F32)
        qrot = pltpu.roll(qr, qr.shape[1] - C_ROPE, 1)
        for h in range(nh):
            a = qr[:, h * C_HEAD_W:(h + 1) * C_HEAD_W].reshape(G, R, C_HEAD_W)
            b = qrot[:, h * C_HEAD_W:(h + 1) * C_HEAD_W].reshape(G, R, C_HEAD_W)
            q_ref[:, h] = ((a * cos + b * sin) * scale).astype(BF)
    kvn = _rms(kv_lat) * kvn_ref[...]
    kv32_ref[...] = kvn.reshape(G, R, C_KV_LORA)
    krot = pltpu.roll(krg, C_HEAD_W - C_ROPE, 1)
    krf = krg.reshape(G, R, C_HEAD_W) * cos + krot.reshape(G, R, C_HEAD_W) * sin
    kr32_ref[...] = krf[:, :, C_NOPE:C_NOPE + C_ROPE]
    if expand:
        kvb = kvn.astype(BF)
        kx = jnp.dot(kvb, wk_ref[...], preferred_element_type=F32)
        for h in range(nh):
            k_ref[0, h] = (kx[:, h * C_HEAD_W:(h + 1) * C_HEAD_W] + krf[0]).astype(BF)
        vt = lax.dot_general(wvt_ref[...], kvb, (((1,), (1,)), ((), ())), preferred_element_type=F32)
        vt_ref[0] = (vt + ones_ref[...]).astype(BF)


def _proj_c(x, mods, blk_off, g, win, qn_g, kvn_g, wqb, tables, G, R, q_transposed, expand_w=None):
    nseq, T, D = x.shape
    nh = C_HEADS
    cos_t, sin_t, cost_t, sint_t = tables
    tab = pl.BlockSpec((1, R, C_HEAD_W), lambda i, j: (0, j, 0))
    tabt = pl.BlockSpec((C_ROPE, R), lambda i, j: (0, j))
    if q_transposed:
        assert G == 1
        wqb = wqb.T
        q_spec = pl.BlockSpec((1, nh, C_HEAD_W, R), lambda i, j: (i, 0, 0, j))
        q_shape = jax.ShapeDtypeStruct((nseq, nh, C_HEAD_W, T), BF)
    else:
        q_spec = pl.BlockSpec((G, nh, R, C_HEAD_W), lambda i, j: (i, 0, j, 0))
        q_shape = jax.ShapeDtypeStruct((nseq, nh, T, C_HEAD_W), BF)
    args = [x, mods, mods, g.reshape(1, D), win, qn_g.reshape(1, -1), kvn_g.reshape(1, -1), wqb,
            cos_t, sin_t, cost_t, sint_t]
    in_specs = [pl.BlockSpec((G, R, D), lambda i, j: (i, j, 0)),
                _mod_spec(G, blk_off, 3, D), _mod_spec(G, blk_off, 4, D), _const_spec((1, D)),
                _const_spec(win.shape), _const_spec((1, C_Q_LORA)), _const_spec((1, C_KV_LORA)),
                _const_spec(wqb.shape), tab, tab, tabt, tabt]
    out_specs = [q_spec,
                 pl.BlockSpec((G, R, C_KV_LORA), lambda i, j: (i, j, 0)),
                 pl.BlockSpec((G, R, C_ROPE), lambda i, j: (i, j, 0))]
    out_shape = [q_shape,
                 jax.ShapeDtypeStruct((nseq, T, C_KV_LORA), F32),
                 jax.ShapeDtypeStruct((nseq, T, C_ROPE), F32)]
    if expand_w is not None:
        assert G == 1
        wk, wvt = expand_w
        ones = np.zeros((nh, C_VT_ROWS, 1), np.float32)
        ones[:, C_V] = 1.0
        ones = jnp.asarray(ones.reshape(nh * C_VT_ROWS, 1))
        args += [wk, wvt, ones]
        in_specs += [_const_spec(wk.shape), _const_spec(wvt.shape), _const_spec(ones.shape)]
        out_specs += [pl.BlockSpec((1, nh, R, C_HEAD_W), lambda i, j: (i, 0, j, 0)),
                      pl.BlockSpec((1, nh * C_VT_ROWS, R), lambda i, j: (i, 0, j))]
        out_shape += [jax.ShapeDtypeStruct((nseq, nh, T, C_HEAD_W), BF),
                      jax.ShapeDtypeStruct((nseq, nh * C_VT_ROWS, T), BF)]
    return pl.pallas_call(
        functools.partial(_proj_c_body, q_transposed=q_transposed, expand=expand_w is not None),
        grid=(nseq // G, T // R),
        in_specs=in_specs,
        out_specs=out_specs,
        out_shape=out_shape,
        compiler_params=_cparams(2),
        name="proj_c",
    )(*args)


def _flash_body(qi_ref, kj_ref, q_ref, k_ref, vt_ref, o_ref, acc_ref, m_ref,
                *, T, SK, QW):
    t = pl.program_id(1)
    i = qi_ref[t]
    j = kj_ref[t]
    nh = q_ref.shape[1]
    n_stripes = T // QW

    @pl.when(j == 0)
    def _():
        m_ref[...] = jnp.full(m_ref.shape, NEG, F32)
        acc_ref[...] = jnp.zeros(acc_ref.shape, F32)

    def sweep(diag):
        tiles = []
        for r in range(n_stripes):
            for c in range(T // SK):
                if diag and c * SK >= (r + 1) * QW:
                    continue
                tiles.append((c, r, diag and (c + 1) * SK > r * QW + CHUNK))

        def head_group(g, carry):
            heads = [g * MLA_HU + u for u in range(MLA_HU)]
            stream = [(u, c, r, mask) for u in range(MLA_HU) for (c, r, mask) in tiles]
            state = {(u, r): (m_ref[heads[u], :, r * QW:(r + 1) * QW], acc_ref[heads[u], :, r * QW:(r + 1) * QW])
                     for u in range(MLA_HU) for r in range(n_stripes)}

            def scores(u, c, r, mask):
                s = jnp.dot(k_ref[0, heads[u], c * SK:(c + 1) * SK, :], q_ref[0, heads[u], :, r * QW:(r + 1) * QW],
                            preferred_element_type=F32)
                if mask:
                    kpos = lax.broadcasted_iota(jnp.int32, (SK, 1), 0) + c * SK
                    qpos = lax.broadcasted_iota(jnp.int32, (1, QW), 1) + r * QW
                    s = jnp.where(kpos // CHUNK <= qpos // CHUNK, s, NEG)
                return s

            pending = [scores(*tl) for tl in stream[:MLA_AHEAD]]
            for n, (u, c, r, _) in enumerate(stream):
                s = pending.pop(0)
                if n + MLA_AHEAD < len(stream):
                    pending.append(scores(*stream[n + MLA_AHEAD]))
                m, acc = state[u, r]
                m_new = jnp.maximum(m, jnp.max(s, axis=0, keepdims=True))
                p = jnp.exp2(s - m_new).astype(BF)
                acc = jnp.exp2(m - m_new) * acc + jnp.dot(vt_ref[0, heads[u], :, c * SK:(c + 1) * SK], p,
                                                          preferred_element_type=F32)
                state[u, r] = (m_new, acc)
            for (u, r), (m, acc) in state.items():
                m_ref[heads[u], :, r * QW:(r + 1) * QW] = m
                acc_ref[heads[u], :, r * QW:(r + 1) * QW] = acc
            return carry
        lax.fori_loop(0, nh // MLA_HU, head_group, 0)

    pl.when(j == i)(lambda: sweep(True))
    pl.when(j != i)(lambda: sweep(False))

    @pl.when(j == i)
    def _():
        for p in range(nh // 2):
            halves = [acc_ref[h, :C_V, :] * (1.0 / acc_ref[h, C_V:C_V + 1, :]) for h in (2 * p, 2 * p + 1)]
            o_ref[0, :, p * LANES:(p + 1) * LANES] = jnp.concatenate(halves, axis=0).T.astype(BF)


def _mla_attn(q, k, vt, *, T, SK, QW):
    nseq, nh, _, S = q.shape
    assert S % T == 0 and T % QW == 0 and T % SK == 0 and SK % CHUNK == 0 and QW % CHUNK == 0
    Tq = Tk = T
    pairs = [(i, j) for i in range(S // T) for j in range(i + 1)]
    qi = jnp.asarray([p[0] for p in pairs], jnp.int32)
    kj = jnp.asarray([p[1] for p in pairs], jnp.int32)
    grid_spec = pltpu.PrefetchScalarGridSpec(
        num_scalar_prefetch=2,
        grid=(nseq, len(pairs)),
        in_specs=[pl.BlockSpec((1, nh, C_HEAD_W, Tq), lambda b, t, qi, kj: (b, 0, 0, qi[t])),
                  pl.BlockSpec((1, nh, Tk, C_HEAD_W), lambda b, t, qi, kj: (b, 0, kj[t], 0)),
                  pl.BlockSpec((1, nh, C_VT_ROWS, Tk), lambda b, t, qi, kj: (b, 0, 0, kj[t]))],
        out_specs=pl.BlockSpec((1, Tq, nh * C_V), lambda b, t, qi, kj: (b, qi[t], 0)),
        scratch_shapes=[pltpu.VMEM((nh, C_VT_ROWS, Tq), F32), pltpu.VMEM((nh, 1, Tq), F32)],
    )
    return pl.pallas_call(
        functools.partial(_flash_body, T=T, SK=SK, QW=QW),
        grid_spec=grid_spec,
        out_shape=jax.ShapeDtypeStruct((nseq, S, nh * C_V), BF),
        compiler_params=_cparams(2),
        name="mla_attn",
    )(qi, kj, q, k, vt)


def _decode_body(q_ref, ckv_ref, ckr_ref, nkv_ref, nkr_ref, wk_ref, wv_ref, place_ref, o_ref, *, n_pad):
    nh, tq, _ = q_ref.shape[1:]
    n_keys = ckv_ref.shape[1] + nkv_ref.shape[1]
    kv = jnp.concatenate([ckv_ref[0].astype(BF), nkv_ref[0].astype(BF),
                          jnp.zeros((n_pad - n_keys, C_KV_LORA), BF)], axis=0)
    kr = jnp.concatenate([ckr_ref[0].astype(BF), nkr_ref[0].astype(BF),
                          jnp.zeros((n_pad - n_keys, C_ROPE), BF)], axis=0)
    krf = jnp.dot(kr, place_ref[...], preferred_element_type=F32).astype(BF)
    keys = jnp.concatenate([kv, krf], axis=1)
    q = q_ref[0]
    q_lat = jnp.concatenate([jnp.dot(q[h], wk_ref[h], preferred_element_type=F32) for h in range(nh)], axis=0)
    queries = jnp.concatenate([q_lat.astype(BF), q.reshape(nh * tq, C_HEAD_W)], axis=1)
    s = lax.dot_general(queries, keys, (((1,), (1,)), ((), ())), preferred_element_type=F32)
    s = jnp.where(lax.broadcasted_iota(jnp.int32, (1, n_pad), 1) < n_keys, s, NEG)
    m = jnp.max(s, axis=-1, keepdims=True)
    e = jnp.exp2(s - m)
    den = jnp.sum(e, axis=-1, keepdims=True)
    lat = (jnp.dot(e.astype(BF), kv, preferred_element_type=F32) * (1.0 / den)).astype(BF)
    for p in range(nh // 2):
        pair = jnp.concatenate([lat[2 * p * tq:(2 * p + 1) * tq], lat[(2 * p + 1) * tq:(2 * p + 2) * tq]], axis=1)
        o_ref[0, :, p * LANES:(p + 1) * LANES] = jnp.dot(pair, wv_ref[p], preferred_element_type=F32).astype(BF)


def _mla_decode(q, cache_kv, cache_kr, new_kv, new_kr, wk_heads, wv_pairs):
    nseq, nh, tq, _ = q.shape
    past = cache_kv.shape[1]
    n_pad = -(-(past + tq) // LANES) * LANES
    place = np.zeros((C_ROPE, C_HEAD_W), np.float32)
    place[np.arange(C_ROPE), C_NOPE + np.arange(C_ROPE)] = 1.0
    per_seq = lambda a: pl.BlockSpec((1,) + a.shape[1:], lambda b: (b,) + (0,) * (a.ndim - 1))
    return pl.pallas_call(
        functools.partial(_decode_body, n_pad=n_pad),
        grid=(nseq,),
        in_specs=[per_seq(q), per_seq(cache_kv), per_seq(cache_kr), per_seq(new_kv), per_seq(new_kr),
                  _const_spec(wk_heads.shape), _const_spec(wv_pairs.shape), _const_spec(place.shape)],
        out_specs=pl.BlockSpec((1, tq, nh * C_V), lambda b: (b, 0, 0)),
        out_shape=jax.ShapeDtypeStruct((nseq, tq, nh * C_V), BF),
        compiler_params=_cparams(1),
        name="mla_decode",
    )(q, cache_kv, cache_kr, new_kv, new_kr, wk_heads, wv_pairs, jnp.asarray(place, BF))


def _rope_tables(pos):
    half = C_ROPE // 2
    inv = (np.float32(ROPE_BASE) ** (-np.arange(half, dtype=np.float32) / np.float32(half))).astype(np.float32)
    ang = (pos.astype(np.float32)[:, None] * inv[None, :]).astype(np.float32).astype(np.float64)
    n = pos.shape[0]
    cos = np.zeros((1, n, C_HEAD_W), np.float32)
    sin = np.zeros((1, n, C_HEAD_W), np.float32)
    cos[0, :, :C_NOPE] = 1.0
    cos[0, :, C_NOPE:C_NOPE + C_ROPE] = np.concatenate([np.cos(ang), np.cos(ang)], axis=1)
    sin[0, :, C_NOPE:C_NOPE + C_ROPE] = np.concatenate([np.sin(ang), np.sin(ang)], axis=1)
    cos32, sin32 = cos[0, :, C_NOPE:C_NOPE + C_ROPE].T, sin[0, :, C_NOPE:C_NOPE + C_ROPE].T
    return jnp.asarray(cos), jnp.asarray(sin), jnp.asarray(cos32), jnp.asarray(sin32)


def _rot_half_cols(w):
    half = w.shape[-1] // 2
    return jnp.concatenate([-w[..., half:], w[..., :half]], axis=-1)


def kernel(x_prompt, x_sample, c_prompt, c_sample, cache_a_k, cache_a_v, cache_b_k, cache_b_v, cache_c_kv, cache_c_kr, w_ada, b_ada, norm_g, final_norm_g, ffn_w_gate, ffn_w_up, ffn_w_down, w_in_ab, w_out_ab, rel_bias_a, t5_bias, sinks_b, w_in_c, c_q_norm_g, c_kv_norm_g, w_qb, w_kvb, w_out_c):
    nb, S, D = x_prompt.shape
    ns, TS, _ = x_sample.shape
    la_c, lb_c = cache_a_k.shape[2], cache_b_k.shape[2]
    past = cache_c_kv.shape[2]
    la_p, lb_p = min(A_PREV_CHUNKS * CHUNK, S), min(B_PREV_CHUNKS * CHUNK, S)
    assert la_p <= ROW_TILE and S % MLA_T == 0 and S % ROW_TILE == 0 and TS % 8 == 0

    n_cond = ns + nb
    n_cond_pad = -(-n_cond // 8) * 8
    c_all = jnp.zeros((n_cond_pad, D), F32).at[:ns].set(c_sample).at[ns:n_cond].set(c_prompt)
    mods_all = _adaln(c_all, w_ada, b_ada)
    groups = {
        "p": dict(x=x_prompt, blk_off=ns, G=1, R=ROW_TILE),
        "s": dict(x=x_sample, blk_off=0, G=ns, R=TS),
    }

    ffn_w = tuple(w.astype(BF) for w in (ffn_w_gate, ffn_w_up, ffn_w_down))
    w_ab = w_in_ab[0].astype(BF)
    w_oa, w_ob = w_out_ab[0, :A_W].astype(BF), w_out_ab[0, A_W:].astype(BF)
    pair_a = tuple(range(A_HEADS // 2))
    pair_b = tuple(p // (B_Q_HEADS // B_KV_HEADS // 2) for p in range(B_Q_HEADS // 2))
    a_back, b_back = A_PREV_CHUNKS * CHUNK, B_PREV_CHUNKS * CHUNK
    a_prev_blocks, b_prev_blocks = -(-a_back // BAND_TQ), -(-b_back // BAND_TQ)
    a_prev_rows, b_prev_rows = a_prev_blocks * BAND_TQ, b_prev_blocks * BAND_TQ
    assert BAND_TQ % BAND_WIN == 0 and BAND_WIN % CHUNK == 0
    win_starts = range(0, BAND_TQ, BAND_WIN)
    windows_a = tuple((qs, BAND_WIN, a_prev_rows + qs - a_back, a_back + BAND_WIN) for qs in win_starts)
    windows_b = tuple((qs, BAND_WIN, b_prev_rows + qs - b_back, b_back + BAND_WIN) for qs in win_starts)
    bias_ap = _bias_tile(rel_bias_a[0], "clip", BAND_WIN, a_back + BAND_WIN, a_back, A_PREV_CHUNKS)
    bias_bp = _bias_tile(t5_bias, "t5", BAND_WIN, b_back + BAND_WIN, b_back, B_PREV_CHUNKS)
    bias_as = _bias_tile(rel_bias_a[0], "clip", TS, la_c + TS, la_c, None)
    bias_bs = _bias_tile(t5_bias, "t5", TS, lb_c + TS, lb_c, None)
    dup = lambda c: jnp.repeat(c, 2, axis=2).reshape(c.shape[0], c.shape[1], 2 * BKV_W)
    cak = cache_a_k[0].reshape(ns, la_c, A_W)
    cav = cache_a_v[0].reshape(ns, la_c, A_W)
    cbk, cbv = dup(cache_b_k[0]), dup(cache_b_v[0])

    mods = mods_all[0].reshape(n_cond_pad, 1, -1)
    state0 = {}
    for name, gr in groups.items():
        x, off, G, R = gr["x"], gr["blk_off"] // gr["G"], gr["G"], gr["R"]
        x = _ffn(x, mods, off, 0, norm_g[0, 0], ffn_w, 0, 0, G, R)
        ta, tb = (la_p, lb_p) if name == "p" else (TS, TS)
        qa, ka, va, qb, kb, vb, ka32, va32, kb32, vb32 = _proj_ab(x, mods, off, norm_g[0, 1], w_ab, G, R, ta, tb)
        if name == "p":
            prev = lambda arr, n: [(arr, BAND_TQ, d - n) for d in range(n + 1)]
            oa = _band_attn(qa, prev(ka, a_prev_blocks), prev(va, a_prev_blocks), bias_ap, None,
                            Tq=BAND_TQ, n_prev_rows=a_prev_rows, pair_map=pair_a, windows=windows_a)
            ob = _band_attn(qb, prev(kb, b_prev_blocks), prev(vb, b_prev_blocks), bias_bp, sinks_b[0],
                            Tq=BAND_TQ, n_prev_rows=b_prev_rows, pair_map=pair_b, windows=windows_b)
        else:
            oa = _band_attn(qa, [(cak, la_c, None), (ka, TS, None)], [(cav, la_c, None), (va, TS, None)],
                            bias_as, None, Tq=TS, n_prev_rows=0, pair_map=pair_a)
            ob = _band_attn(qb, [(cbk, lb_c, None), (kb, TS, None)], [(cbv, lb_c, None), (vb, TS, None)],
                            bias_bs, sinks_b[0], Tq=TS, n_prev_rows=0, pair_map=pair_b)
        x = _ffn(x, mods, off, 6, norm_g[0, 2], ffn_w, 0, 1, G, R, attn=([oa, ob], [w_oa, w_ob]))
        gr["x"] = x
        n_out = x.shape[0]
        state0[name] = (ka32.reshape(1, n_out, ta, A_HEADS, HEAD_DIM), va32.reshape(1, n_out, ta, A_HEADS, HEAD_DIM),
                        kb32.reshape(1, n_out, tb, B_KV_HEADS, HEAD_DIM), vb32.reshape(1, n_out, tb, B_KV_HEADS, HEAD_DIM))

    hw = C_NOPE + C_ROPE
    w_in = w_in_c[0]
    w_kr = w_in[:, C_Q_LORA + C_KV_LORA:]
    win_ext = jnp.concatenate([w_in[:, :C_Q_LORA + C_KV_LORA], jnp.zeros((D, C_NOPE), F32),
                               w_kr, _rot_half_cols(w_kr)], axis=1).astype(BF)
    wq3 = w_qb[0].reshape(C_Q_LORA, C_HEADS, hw)
    wqb_ext = jnp.concatenate([wq3, _rot_half_cols(wq3[..., C_NOPE:])], axis=-1
                              ).reshape(C_Q_LORA, C_HEADS * C_HEAD_W).astype(BF)
    wkv3 = w_kvb[0].reshape(C_KV_LORA, C_HEADS, C_NOPE + C_V)
    wk_ext = jnp.concatenate([wkv3[..., :C_NOPE], jnp.zeros((C_KV_LORA, C_HEADS, C_HEAD_W - C_NOPE), F32)],
                             axis=-1).reshape(C_KV_LORA, C_HEADS * C_HEAD_W).astype(BF)
    wvt = jnp.concatenate([wkv3[..., C_NOPE:], jnp.zeros((C_KV_LORA, C_HEADS, C_VT_ROWS - C_V), F32)], axis=-1
                          ).reshape(C_KV_LORA, C_HEADS * C_VT_ROWS).T.astype(BF)
    w_oc = w_out_c[0].astype(BF)
    wk_heads = jnp.concatenate([jnp.transpose(wkv3[..., :C_NOPE], (1, 2, 0)),
                                jnp.zeros((C_HEADS, C_HEAD_W - C_NOPE, C_KV_LORA), F32)], axis=1).astype(BF)
    wv4 = jnp.transpose(wkv3[..., C_NOPE:], (1, 0, 2)).reshape(C_HEADS // 2, 2, C_KV_LORA, C_V)
    zv = jnp.zeros((C_HEADS // 2, C_KV_LORA, C_V), F32)
    wv_pairs = jnp.concatenate([jnp.concatenate([wv4[:, 0], zv], axis=2),
                                jnp.concatenate([zv, wv4[:, 1]], axis=2)], axis=1).astype(BF)
    tables = {"p": _rope_tables(np.arange(S)), "s": _rope_tables(past + np.arange(TS))}

    mods = mods_all[1].reshape(n_cond_pad, 1, -1)
    state1 = {}
    for name, gr in groups.items():
        x, off, G, R = gr["x"], gr["blk_off"] // gr["G"], gr["G"], gr["R"]
        x = _ffn(x, mods, off, 0, norm_g[1, 0], ffn_w, 1, 0, G, R)
        if name == "p":
            q, kv32, kr32, k, vt = _proj_c(x, mods, off, norm_g[1, 1], win_ext, c_q_norm_g[0], c_kv_norm_g[0],
                                           wqb_ext, tables[name], G, R, q_transposed=True,
                                           expand_w=(wk_ext, wvt))
            o = _mla_attn(q, k, vt.reshape(nb, C_HEADS, C_VT_ROWS, S), T=MLA_T, SK=MLA_SK, QW=MLA_QW)
        else:
            q, kv32, kr32 = _proj_c(x, mods, off, norm_g[1, 1], win_ext, c_q_norm_g[0], c_kv_norm_g[0],
                                    wqb_ext, tables[name], G, R, q_transposed=False)
            o = _mla_decode(q, cache_c_kv[0], cache_c_kr[0], kv32, kr32, wk_heads, wv_pairs)
        x = _ffn(x, mods, off, 6, norm_g[1, 2], ffn_w, 1, 1, G, R, final_g=final_norm_g, attn=([o], [w_oc]))
        gr["x"] = x
        state1[name] = (kv32[None], kr32[None])

    return (groups["p"]["x"], groups["s"]["x"],
            *state0["p"], *state1["p"], *state0["s"], *state1["s"])
```

```python
import functools
import math

import numpy as np
import jax
import jax.numpy as jnp
from jax import lax
from jax.experimental import pallas as pl
from jax.experimental.pallas import tpu as pltpu

F32 = jnp.float32
BF = jnp.bfloat16

CHUNK = 64
HEAD_DIM = 64
A_HEADS = 8
A_PREV_CHUNKS = 8
A_REL_CLIP = 128
B_Q_HEADS = 8
B_KV_HEADS = 2
B_PREV_CHUNKS = 2
T5_BUCKETS = 32
T5_MAX_DIST = 128
C_HEADS = 16
C_Q_LORA = 384
C_KV_LORA = 256
C_NOPE = 64
C_ROPE = 32
C_V = 64
ROPE_BASE = 10000.0
EPS = 1e-6
NEG = -1e30
A_W = A_HEADS * HEAD_DIM
BQ_W = B_Q_HEADS * HEAD_DIM
BKV_W = B_KV_HEADS * HEAD_DIM

LANES = 128
VMEM_LIMIT = 56 * 1024 * 1024

ROW_TILE = 1024
FF_CHUNK = 256
BAND_TQ = 512
BAND_WIN = 128
BAND_AHEAD = 2
LOG2E = math.log2(math.e)
MLA_T = 1024
MLA_SK = 256
MLA_QW = 256
MLA_HU = 8
MLA_AHEAD = 5
C_VT_ROWS = 80
C_IN_EXT = 768
C_HEAD_W = 128


def _cparams(n_axes, vmem=VMEM_LIMIT):
    return pltpu.CompilerParams(dimension_semantics=("arbitrary",) * n_axes,
                                vmem_limit_bytes=vmem)


def _const_spec(shape):
    n = len(shape)
    return pl.BlockSpec(shape, lambda *_: (0,) * n)


def _rms(x):
    return x * lax.rsqrt(jnp.mean(x * x, axis=-1, keepdims=True) + EPS)


def _norm_mod(x, g, shift, scale):
    return (_rms(x) * g) * (1.0 + scale) + shift


def _mod_spec(G, blk_off, k, D):
    return pl.BlockSpec((G, 1, D), lambda i, j: (blk_off + i, 0, k))


def _adaln_body(c_ref, w_ref, b_ref, o_ref):
    c = c_ref[...]
    s = (c * jax.nn.sigmoid(c)).astype(BF)
    o_ref[0] = jnp.dot(s, w_ref[0].astype(BF), preferred_element_type=F32) + b_ref[0]


def _adaln(c_all, w_ada, b_ada):
    L, D, N = w_ada.shape
    R = c_all.shape[0]
    tn = 1024
    return pl.pallas_call(
        _adaln_body,
        grid=(L, N // tn),
        in_specs=[pl.BlockSpec((R, D), lambda l, j: (0, 0)),
                  pl.BlockSpec((1, D, tn), lambda l, j: (l, 0, j)),
                  pl.BlockSpec((1, 1, tn), lambda l, j: (l, 0, j))],
        out_specs=pl.BlockSpec((1, R, tn), lambda l, j: (l, 0, j)),
        out_shape=jax.ShapeDtypeStruct((L, R, N), F32),
        compiler_params=_cparams(2),
        name="adaln",
    )(c_all, w_ada, b_ada.reshape(L, 1, N))


def _ffn_body(*refs, n_attn, final):
    attn_refs, wo_refs = refs[:n_attn], refs[n_attn:2 * n_attn]
    refs = refs[2 * n_attn:]
    if n_attn:
        ga_ref, refs = refs[0], refs[1:]
    x_ref, sh_ref, sc_ref, gt_ref, g_ref, wg_ref, wu_ref, wd_ref = refs[:8]
    if final:
        fg_ref, o_ref, a_ref = refs[8:]
    else:
        o_ref, a_ref = refs[8:]
    x = x_ref[...]
    G, R, D = x.shape
    F = wg_ref.shape[1]
    if n_attn:
        mix = None
        for at_ref, wo_ref in zip(attn_refs, wo_refs):
            y = jnp.dot(at_ref[...].reshape(G * R, at_ref.shape[2]), wo_ref[...], preferred_element_type=F32)
            mix = y if mix is None else mix + y
        x = x + ga_ref[...] * mix.reshape(G, R, D)
    hb = _norm_mod(x, g_ref[...], sh_ref[...], sc_ref[...]).reshape(G * R, D).astype(BF)
    for c in range(F // FF_CHUNK):
        lo, hi = c * FF_CHUNK, (c + 1) * FF_CHUNK
        g = jnp.dot(hb, wg_ref[:, lo:hi], preferred_element_type=F32)
        u = jnp.dot(hb, wu_ref[:, lo:hi], preferred_element_type=F32)
        a_ref[:, lo:hi] = (g * jax.nn.sigmoid(g) * u).astype(BF)
    ff = jnp.dot(a_ref[...], wd_ref[...], preferred_element_type=F32)
    y = x + (0.5 * gt_ref[...]) * ff.reshape(G, R, D)
    if final:
        y = _rms(y) * fg_ref[...]
    o_ref[...] = y


def _ffn(x, mods, blk_off, kbase, g, weights, layer, which, G, R, final_g=None, attn=None):
    nseq, T, D = x.shape
    wg, wu, wd = weights
    F = wg.shape[-1]
    final = final_g is not None
    row = lambda W: pl.BlockSpec((G, R, W), lambda i, j: (i, j, 0))
    stacked = lambda w: pl.BlockSpec((None, None) + w.shape[2:], lambda i, j: (layer, which, 0, 0),
                                     pipeline_mode=pl.Buffered(1))
    in_specs, args = [], []
    n_attn = 0
    if attn is not None:
        os_, ws = attn
        n_attn = len(os_)
        in_specs += [row(o.shape[2]) for o in os_] + [_const_spec(w.shape) for w in ws]
        in_specs.append(_mod_spec(G, blk_off, 5, D))
        args += [*os_, *ws, mods]
    in_specs += [row(D), _mod_spec(G, blk_off, kbase, D), _mod_spec(G, blk_off, kbase + 1, D),
                 _mod_spec(G, blk_off, kbase + 2, D), _const_spec((1, D)), stacked(wg), stacked(wu), stacked(wd)]
    args += [x, mods, mods, mods, g.reshape(1, D), wg, wu, wd]
    if final:
        in_specs.append(_const_spec((1, D)))
        args.append(final_g.reshape(1, D))
    return pl.pallas_call(
        functools.partial(_ffn_body, n_attn=n_attn, final=final),
        grid=(nseq // G, T // R),
        in_specs=in_specs,
        out_specs=pl.BlockSpec((G, R, D), lambda i, j: (i, j, 0)),
        out_shape=jax.ShapeDtypeStruct((nseq, T, D), F32),
        scratch_shapes=[pltpu.VMEM((G * R, F), BF)],
        compiler_params=_cparams(2),
        name="ffn",
    )(*args)


def _dup_halves(k):
    rolled = pltpu.roll(k, HEAD_DIM, 1)
    lo = lax.broadcasted_iota(jnp.int32, k.shape, 1) < HEAD_DIM
    return jnp.concatenate([jnp.where(lo, k, rolled), jnp.where(lo, rolled, k)], axis=1)


def _proj_ab_body(x_ref, sh_ref, sc_ref, g_ref, w_ref,
                  qa_ref, ka_ref, va_ref, qb_ref, kb_ref, vb_ref,
                  ka32_ref, va32_ref, kb32_ref, vb32_ref, *, ta, tb):
    x = x_ref[...]
    G, R, D = x.shape
    hb = _norm_mod(x, g_ref[...], sh_ref[...], sc_ref[...]).reshape(G * R, D).astype(BF)
    res = jnp.dot(hb, w_ref[...], preferred_element_type=F32)
    o_ka, o_va, o_qb, o_kb, o_vb = A_W, 2 * A_W, 3 * A_W, 3 * A_W + BQ_W, 3 * A_W + BQ_W + BKV_W
    qscale = HEAD_DIM ** -0.5 * LOG2E
    qa_ref[...] = (res[:, :o_ka] * qscale).astype(BF).reshape(G, R, A_W)
    ka_ref[...] = res[:, o_ka:o_va].astype(BF).reshape(G, R, A_W)
    va_ref[...] = res[:, o_va:o_qb].astype(BF).reshape(G, R, A_W)
    qb_ref[...] = (res[:, o_qb:o_kb] * qscale).astype(BF).reshape(G, R, BQ_W)
    kb = res[:, o_kb:o_vb]
    vb = res[:, o_vb:o_vb + BKV_W]
    kb_ref[...] = _dup_halves(kb).astype(BF).reshape(G, R, 2 * BKV_W)
    vb_ref[...] = _dup_halves(vb).astype(BF).reshape(G, R, 2 * BKV_W)

    @pl.when(pl.program_id(1) == pl.num_programs(1) - 1)
    def _():
        ka32_ref[...] = res[:, o_ka:o_va].reshape(G, R, A_W)[:, R - ta:, :]
        va32_ref[...] = res[:, o_va:o_qb].reshape(G, R, A_W)[:, R - ta:, :]
        kb32_ref[...] = kb.reshape(G, R, BKV_W)[:, R - tb:, :]
        vb32_ref[...] = vb.reshape(G, R, BKV_W)[:, R - tb:, :]


def _proj_ab(x, mods, blk_off, g, w, G, R, ta, tb):
    nseq, T, D = x.shape
    row = lambda W: pl.BlockSpec((G, R, W), lambda i, j: (i, j, 0))
    tail = lambda t, W: pl.BlockSpec((G, t, W), lambda i, j: (i, 0, 0))
    bshape = lambda W: jax.ShapeDtypeStruct((nseq, T, W), BF)
    return pl.pallas_call(
        functools.partial(_proj_ab_body, ta=ta, tb=tb),
        grid=(nseq // G, T // R),
        in_specs=[row(D), _mod_spec(G, blk_off, 3, D), _mod_spec(G, blk_off, 4, D),
                  _const_spec((1, D)), _const_spec(w.shape)],
        out_specs=[row(A_W), row(A_W), row(A_W), row(BQ_W), row(2 * BKV_W), row(2 * BKV_W),
                   tail(ta, A_W), tail(ta, A_W), tail(tb, BKV_W), tail(tb, BKV_W)],
        out_shape=[bshape(A_W), bshape(A_W), bshape(A_W), bshape(BQ_W), bshape(2 * BKV_W), bshape(2 * BKV_W),
                   jax.ShapeDtypeStruct((nseq, ta, A_W), F32), jax.ShapeDtypeStruct((nseq, ta, A_W), F32),
                   jax.ShapeDtypeStruct((nseq, tb, BKV_W), F32), jax.ShapeDtypeStruct((nseq, tb, BKV_W), F32)],
        compiler_params=_cparams(2),
        name="proj_ab",
    )(x, mods, mods, g.reshape(1, D), w)


def _t5_bucket_np(rel):
    nb = T5_BUCKETS // 2
    ret = np.where(rel > 0, nb, 0)
    n = np.abs(rel)
    max_exact = nb // 2
    ratio = np.maximum(n, 1).astype(np.float32) / np.float32(max_exact)
    large = max_exact + (np.log(ratio).astype(np.float32) / np.float32(math.log(T5_MAX_DIST / max_exact))
                         * np.float32(nb - max_exact)).astype(np.int32)
    large = np.minimum(large, nb - 1)
    return ret + np.where(n < max_exact, n, large)


def _bias_body(thi_ref, tlo_ref, idx_ref, o_ref, *, Tq, span, prev_chunks, band_chunks):
    E = thi_ref.shape[1]
    L = idx_ref.shape[1]
    onehot = (lax.broadcasted_iota(jnp.int32, (E, L), 0) == idx_ref[...]).astype(BF)
    t = (jnp.dot(thi_ref[...], onehot, preferred_element_type=F32)
         + jnp.dot(tlo_ref[...], onehot, preferred_element_type=F32)) * LOG2E
    if band_chunks is not None:
        qc = lax.broadcasted_iota(jnp.int32, (Tq, span), 0) // CHUNK
        kc = lax.broadcasted_iota(jnp.int32, (Tq, span), 1) // CHUNK - prev_chunks
        valid = (kc <= qc) & (kc >= qc - band_chunks)
    for h in range(o_ref.shape[0]):
        x = jnp.broadcast_to(t[h:h + 1, :], (Tq, L))
        y = pltpu.roll(x, L - Tq + 1, 1, stride=1, stride_axis=0)[:, :span]
        if band_chunks is not None:
            y = jnp.where(valid, y, NEG)
        o_ref[h] = y


def _bias_tile(table, kind, Tq, span, n_prev_rows, band_chunks):
    H = table.shape[1]
    L = -(-(Tq + span - 1) // LANES) * LANES
    rel = np.arange(L) - (Tq - 1) - n_prev_rows
    if kind == "clip":
        idx = np.clip(rel, -A_REL_CLIP, A_REL_CLIP) + A_REL_CLIP
    else:
        idx = _t5_bucket_np(rel)
    E = -(-table.shape[0] // LANES) * LANES
    tt = jnp.zeros((H, E), F32).at[:, :table.shape[0]].set(table.T.astype(F32))
    thi = tt.astype(BF)
    tlo = (tt - thi.astype(F32)).astype(BF)
    return pl.pallas_call(
        functools.partial(_bias_body, Tq=Tq, span=span, prev_chunks=n_prev_rows // CHUNK,
                          band_chunks=band_chunks),
        out_shape=jax.ShapeDtypeStruct((H, Tq, span), F32),
        compiler_params=pltpu.CompilerParams(vmem_limit_bytes=VMEM_LIMIT),
        name="bias_tile",
    )(thi, tlo, jnp.asarray(idx.reshape(1, L), jnp.int32))


def _band_body(*refs, n_parts, n_prev_rows, pair_map, has_sink, Tq, windows):
    q_ref = refs[0]
    k_refs = refs[1:1 + n_parts]
    v_refs = refs[1 + n_parts:1 + 2 * n_parts]
    bias_ref = refs[1 + 2 * n_parts]
    sink_ref = refs[2 + 2 * n_parts] if has_sink else None
    o_ref = refs[-1]
    i = pl.program_id(1)
    k = jnp.concatenate([r[0].astype(BF) for r in k_refs], axis=0)
    v = jnp.concatenate([r[0].astype(BF) for r in v_refs], axis=0)
    q = q_ref[0]
    span = k.shape[0]
    n_heads = 2 * (q.shape[1] // LANES)
    lane_lo = lax.broadcasted_iota(jnp.int32, (1, LANES), 1) < HEAD_DIM

    items = [(w, h) for w in range(len(windows)) for h in range(n_heads)]

    def sweep(mask_start):
        def scores(w, h):
            qs, qr, ks, kr = windows[w]
            p = h // 2
            qp = q[qs:qs + qr, p * LANES:(p + 1) * LANES]
            qm = jnp.where(lane_lo if h % 2 == 0 else jnp.logical_not(lane_lo), qp, jnp.zeros_like(qp))
            kp = k[ks:ks + kr, pair_map[p] * LANES:(pair_map[p] + 1) * LANES]
            s = lax.dot_general(qm, kp, (((1,), (1,)), ((), ())), preferred_element_type=F32) + bias_ref[h]
            if mask_start and ks < n_prev_rows:
                s = jnp.where(lax.broadcasted_iota(jnp.int32, (1, kr), 1) >= (n_prev_rows - ks - i * Tq), s, NEG)
            return s

        pending = [scores(*it) for it in items[:BAND_AHEAD]]
        halves = []
        for n, (w, h) in enumerate(items):
            qs, qr, ks, kr = windows[w]
            s = pending.pop(0)
            if n + BAND_AHEAD < len(items):
                pending.append(scores(*items[n + BAND_AHEAD]))
            p = h // 2
            vp = v[ks:ks + kr, pair_map[p] * LANES:(pair_map[p] + 1) * LANES]
            m = jnp.max(s, axis=-1, keepdims=True)
            if has_sink:
                sink = sink_ref[h] * LOG2E
                m = jnp.maximum(m, sink)
            e = jnp.exp2(s - m)
            den = jnp.sum(e, axis=-1, keepdims=True)
            if has_sink:
                den = den + jnp.exp2(sink - m)
            o = jnp.dot(e.astype(BF), vp, preferred_element_type=F32)
            halves.append(o * (1.0 / den))
            if h % 2 == 1:
                o_ref[0, qs:qs + qr, p * LANES:(p + 1) * LANES] = jnp.where(lane_lo, halves[0], halves[1]).astype(BF)
                halves = []

    if n_prev_rows:
        pl.when(i * Tq < n_prev_rows)(lambda: sweep(True))
        pl.when(i * Tq >= n_prev_rows)(lambda: sweep(False))
    else:
        sweep(False)


def _band_attn(q, kparts, vparts, bias, sinks, *, Tq, n_prev_rows, pair_map, windows=None):
    nseq, T, W = q.shape
    if windows is None:
        windows = ((0, Tq, 0, sum(p[1] for p in kparts)),)

    def part_spec(arr, rows, off):
        if off is None:
            return pl.BlockSpec((1, rows, arr.shape[2]), lambda b, i: (b, 0, 0))
        return pl.BlockSpec((1, rows, arr.shape[2]), lambda b, i: (b, jnp.maximum(i + off, 0), 0))

    in_specs = [pl.BlockSpec((1, Tq, W), lambda b, i: (b, i, 0))]
    in_specs += [part_spec(*p) for p in kparts] + [part_spec(*p) for p in vparts]
    in_specs.append(_const_spec(bias.shape))
    args = [q] + [p[0] for p in kparts] + [p[0] for p in vparts] + [bias]
    if sinks is not None:
        in_specs.append(pl.BlockSpec(memory_space=pltpu.SMEM))
        args.append(sinks.astype(F32))
    return pl.pallas_call(
        functools.partial(_band_body, n_parts=len(kparts), n_prev_rows=n_prev_rows,
                          pair_map=pair_map, has_sink=sinks is not None, Tq=Tq, windows=tuple(windows)),
        grid=(nseq, T // Tq),
        in_specs=in_specs,
        out_specs=pl.BlockSpec((1, Tq, W), lambda b, i: (b, i, 0)),
        out_shape=jax.ShapeDtypeStruct((nseq, T, W), BF),
        compiler_params=_cparams(2),
        name="band_attn",
    )(*args)


def _proj_c_body(x_ref, sh_ref, sc_ref, g_ref, win_ref, qn_ref, kvn_ref, wqb_ref, cos_ref, sin_ref,
                 cost_ref, sint_ref, *rest, q_transposed, expand):
    if expand:
        wk_ref, wvt_ref, ones_ref, q_ref, kv32_ref, kr32_ref, k_ref, vt_ref = rest
    else:
        q_ref, kv32_ref, kr32_ref = rest
    x = x_ref[...]
    G, R, D = x.shape
    hb = _norm_mod(x, g_ref[...], sh_ref[...], sc_ref[...]).reshape(G * R, D).astype(BF)
    res = jnp.dot(hb, win_ref[...], preferred_element_type=F32)
    q_lat = res[:, :C_Q_LORA]
    kv_lat = res[:, C_Q_LORA:C_Q_LORA + C_KV_LORA]
    krg = res[:, C_Q_LORA + C_KV_LORA:]
    cos = cos_ref[...]
    sin = sin_ref[...]
    qn = (_rms(q_lat) * qn_ref[...]).astype(BF)
    scale = (C_NOPE + C_ROPE) ** -0.5 * math.log2(math.e)
    nh = q_ref.shape[1]
    if q_transposed:
        qt = lax.dot_general(wqb_ref[...], qn, (((1,), (1,)), ((), ())), preferred_element_type=F32)
        cost, sint = cost_ref[...], sint_ref[...]
        for h in range(nh):
            blk = qt[h * C_HEAD_W:(h + 1) * C_HEAD_W]
            roped = blk[C_NOPE:C_NOPE + C_ROPE] * cost + blk[C_NOPE + C_ROPE:] * sint
            q_ref[0, h] = (jnp.concatenate([blk[:C_NOPE], roped, jnp.zeros_like(roped)], axis=0) * scale).astype(BF)
    else:
        qr = jnp.dot(qn, wqb_ref[...], preferred_element_type=F32)
        qrot = pltpu.roll(qr, qr.shape[1] - C_ROPE, 1)
        for h in range(nh):
            a = qr[:, h * C_HEAD_W:(h + 1) * C_HEAD_W].reshape(G, R, C_HEAD_W)
            b = qrot[:, h * C_HEAD_W:(h + 1) * C_HEAD_W].reshape(G, R, C_HEAD_W)
            q_ref[:, h] = ((a * cos + b * sin) * scale).astype(BF)
    kvn = _rms(kv_lat) * kvn_ref[...]
    kv32_ref[...] = kvn.reshape(G, R, C_KV_LORA)
    krot = pltpu.roll(krg, C_HEAD_W - C_ROPE, 1)
    krf = krg.reshape(G, R, C_HEAD_W) * cos + krot.reshape(G, R, C_HEAD_W) * sin
    kr32_ref[...] = krf[:, :, C_NOPE:C_NOPE + C_ROPE]
    if expand:
        kvb = kvn.astype(BF)
        kx = jnp.dot(kvb, wk_ref[...], preferred_element_type=F32)
        for h in range(nh):
            k_ref[0, h] = (kx[:, h * C_HEAD_W:(h + 1) * C_HEAD_W] + krf[0]).astype(BF)
        vt = lax.dot_general(wvt_ref[...], kvb, (((1,), (1,)), ((), ())), preferred_element_type=F32)
        vt_ref[0] = (vt + ones_ref[...]).astype(BF)


def _proj_c(x, mods, blk_off, g, win, qn_g, kvn_g, wqb, tables, G, R, q_transposed, expand_w=None):
    nseq, T, D = x.shape
    nh = C_HEADS
    cos_t, sin_t, cost_t, sint_t = tables
    tab = pl.BlockSpec((1, R, C_HEAD_W), lambda i, j: (0, j, 0))
    tabt = pl.BlockSpec((C_ROPE, R), lambda i, j: (0, j))
    if q_transposed:
        assert G == 1
        wqb = wqb.T
        q_spec = pl.BlockSpec((1, nh, C_HEAD_W, R), lambda i, j: (i, 0, 0, j))
        q_shape = jax.ShapeDtypeStruct((nseq, nh, C_HEAD_W, T), BF)
    else:
        q_spec = pl.BlockSpec((G, nh, R, C_HEAD_W), lambda i, j: (i, 0, j, 0))
        q_shape = jax.ShapeDtypeStruct((nseq, nh, T, C_HEAD_W), BF)
    args = [x, mods, mods, g.reshape(1, D), win, qn_g.reshape(1, -1), kvn_g.reshape(1, -1), wqb,
            cos_t, sin_t, cost_t, sint_t]
    in_specs = [pl.BlockSpec((G, R, D), lambda i, j: (i, j, 0)),
                _mod_spec(G, blk_off, 3, D), _mod_spec(G, blk_off, 4, D), _const_spec((1, D)),
                _const_spec(win.shape), _const_spec((1, C_Q_LORA)), _const_spec((1, C_KV_LORA)),
                _const_spec(wqb.shape), tab, tab, tabt, tabt]
    out_specs = [q_spec,
                 pl.BlockSpec((G, R, C_KV_LORA), lambda i, j: (i, j, 0)),
                 pl.BlockSpec((G, R, C_ROPE), lambda i, j: (i, j, 0))]
    out_shape = [q_shape,
                 jax.ShapeDtypeStruct((nseq, T, C_KV_LORA), F32),
                 jax.ShapeDtypeStruct((nseq, T, C_ROPE), F32)]
    if expand_w is not None:
        assert G == 1
        wk, wvt = expand_w
        ones = np.zeros((nh, C_VT_ROWS, 1), np.float32)
        ones[:, C_V] = 1.0
        ones = jnp.asarray(ones.reshape(nh * C_VT_ROWS, 1))
        args += [wk, wvt, ones]
        in_specs += [_const_spec(wk.shape), _const_spec(wvt.shape), _const_spec(ones.shape)]
        out_specs += [pl.BlockSpec((1, nh, R, C_HEAD_W), lambda i, j: (i, 0, j, 0)),
                      pl.BlockSpec((1, nh * C_VT_ROWS, R), lambda i, j: (i, 0, j))]
        out_shape += [jax.ShapeDtypeStruct((nseq, nh, T, C_HEAD_W), BF),
                      jax.ShapeDtypeStruct((nseq, nh * C_VT_ROWS, T), BF)]
    return pl.pallas_call(
        functools.partial(_proj_c_body, q_transposed=q_transposed, expand=expand_w is not None),
        grid=(nseq // G, T // R),
        in_specs=in_specs,
        out_specs=out_specs,
        out_shape=out_shape,
        compiler_params=_cparams(2),
        name="proj_c",
    )(*args)


def _flash_body(qi_ref, kj_ref, q_ref, k_ref, vt_ref, o_ref, acc_ref, m_ref,
                *, T, SK, QW):
    t = pl.program_id(1)
    i = qi_ref[t]
    j = kj_ref[t]
    nh = q_ref.shape[1]
    n_stripes = T // QW

    @pl.when(j == 0)
    def _():
        m_ref[...] = jnp.full(m_ref.shape, NEG, F32)
        acc_ref[...] = jnp.zeros(acc_ref.shape, F32)

    def sweep(diag):
        tiles = []
        for r in range(n_stripes):
            for c in range(T // SK):
                if diag and c * SK >= (r + 1) * QW:
                    continue
                tiles.append((c, r, diag and (c + 1) * SK > r * QW + CHUNK))

        def head_group(g, carry):
            heads = [g * MLA_HU + u for u in range(MLA_HU)]
            stream = [(u, c, r, mask) for u in range(MLA_HU) for (c, r, mask) in tiles]
            state = {(u, r): (m_ref[heads[u], :, r * QW:(r + 1) * QW], acc_ref[heads[u], :, r * QW:(r + 1) * QW])
                     for u in range(MLA_HU) for r in range(n_stripes)}

            def scores(u, c, r, mask):
                s = jnp.dot(k_ref[0, heads[u], c * SK:(c + 1) * SK, :], q_ref[0, heads[u], :, r * QW:(r + 1) * QW],
                            preferred_element_type=F32)
                if mask:
                    kpos = lax.broadcasted_iota(jnp.int32, (SK, 1), 0) + c * SK
                    qpos = lax.broadcasted_iota(jnp.int32, (1, QW), 1) + r * QW
                    s = jnp.where(kpos // CHUNK <= qpos // CHUNK, s, NEG)
                return s

            pending = [scores(*tl) for tl in stream[:MLA_AHEAD]]
            for n, (u, c, r, _) in enumerate(stream):
                s = pending.pop(0)
                if n + MLA_AHEAD < len(stream):
                    pending.append(scores(*stream[n + MLA_AHEAD]))
                m, acc = state[u, r]
                m_new = jnp.maximum(m, jnp.max(s, axis=0, keepdims=True))
                p = jnp.exp2(s - m_new).astype(BF)
                acc = jnp.exp2(m - m_new) * acc + jnp.dot(vt_ref[0, heads[u], :, c * SK:(c + 1) * SK], p,
                                                          preferred_element_type=F32)
                state[u, r] = (m_new, acc)
            for (u, r), (m, acc) in state.items():
                m_ref[heads[u], :, r * QW:(r + 1) * QW] = m
                acc_ref[heads[u], :, r * QW:(r + 1) * QW] = acc
            return carry
        lax.fori_loop(0, nh // MLA_HU, head_group, 0)

    pl.when(j == i)(lambda: sweep(True))
    pl.when(j != i)(lambda: sweep(False))

    @pl.when(j == i)
    def _():
        for p in range(nh // 2):
            halves = [acc_ref[h, :C_V, :] * (1.0 / acc_ref[h, C_V:C_V + 1, :]) for h in (2 * p, 2 * p + 1)]
            o_ref[0, :, p * LANES:(p + 1) * LANES] = jnp.concatenate(halves, axis=0).T.astype(BF)


def _mla_attn(q, k, vt, *, T, SK, QW):
    nseq, nh, _, S = q.shape
    assert S % T == 0 and T % QW == 0 and T % SK == 0 and SK % CHUNK == 0 and QW % CHUNK == 0
    Tq = Tk = T
    pairs = [(i, j) for i in range(S // T) for j in range(i + 1)]
    qi = jnp.asarray([p[0] for p in pairs], jnp.int32)
    kj = jnp.asarray([p[1] for p in pairs], jnp.int32)
    grid_spec = pltpu.PrefetchScalarGridSpec(
        num_scalar_prefetch=2,
        grid=(nseq, len(pairs)),
        in_specs=[pl.BlockSpec((1, nh, C_HEAD_W, Tq), lambda b, t, qi, kj: (b, 0, 0, qi[t])),
                  pl.BlockSpec((1, nh, Tk, C_HEAD_W), lambda b, t, qi, kj: (b, 0, kj[t], 0)),
                  pl.BlockSpec((1, nh, C_VT_ROWS, Tk), lambda b, t, qi, kj: (b, 0, 0, kj[t]))],
        out_specs=pl.BlockSpec((1, Tq, nh * C_V), lambda b, t, qi, kj: (b, qi[t], 0)),
        scratch_shapes=[pltpu.VMEM((nh, C_VT_ROWS, Tq), F32), pltpu.VMEM((nh, 1, Tq), F32)],
    )
    return pl.pallas_call(
        functools.partial(_flash_body, T=T, SK=SK, QW=QW),
        grid_spec=grid_spec,
        out_shape=jax.ShapeDtypeStruct((nseq, S, nh * C_V), BF),
        compiler_params=_cparams(2),
        name="mla_attn",
    )(qi, kj, q, k, vt)


def _decode_body(q_ref, ckv_ref, ckr_ref, nkv_ref, nkr_ref, wk_ref, wv_ref, place_ref, o_ref, *, n_pad):
    nh, tq, _ = q_ref.shape[1:]
    n_keys = ckv_ref.shape[1] + nkv_ref.shape[1]
    kv = jnp.concatenate([ckv_ref[0].astype(BF), nkv_ref[0].astype(BF),
                          jnp.zeros((n_pad - n_keys, C_KV_LORA), BF)], axis=0)
    kr = jnp.concatenate([ckr_ref[0].astype(BF), nkr_ref[0].astype(BF),
                          jnp.zeros((n_pad - n_keys, C_ROPE), BF)], axis=0)
    krf = jnp.dot(kr, place_ref[...], preferred_element_type=F32).astype(BF)
    keys = jnp.concatenate([kv, krf], axis=1)
    q = q_ref[0]
    q_lat = jnp.concatenate([jnp.dot(q[h], wk_ref[h], preferred_element_type=F32) for h in range(nh)], axis=0)
    queries = jnp.concatenate([q_lat.astype(BF), q.reshape(nh * tq, C_HEAD_W)], axis=1)
    s = lax.dot_general(queries, keys, (((1,), (1,)), ((), ())), preferred_element_type=F32)
    s = jnp.where(lax.broadcasted_iota(jnp.int32, (1, n_pad), 1) < n_keys, s, NEG)
    m = jnp.max(s, axis=-1, keepdims=True)
    e = jnp.exp2(s - m)
    den = jnp.sum(e, axis=-1, keepdims=True)
    lat = (jnp.dot(e.astype(BF), kv, preferred_element_type=F32) * (1.0 / den)).astype(BF)
    for p in range(nh // 2):
        pair = jnp.concatenate([lat[2 * p * tq:(2 * p + 1) * tq], lat[(2 * p + 1) * tq:(2 * p + 2) * tq]], axis=1)
        o_ref[0, :, p * LANES:(p + 1) * LANES] = jnp.dot(pair, wv_ref[p], preferred_element_type=F32).astype(BF)


def _mla_decode(q, cache_kv, cache_kr, new_kv, new_kr, wk_heads, wv_pairs):
    nseq, nh, tq, _ = q.shape
    past = cache_kv.shape[1]
    n_pad = -(-(past + tq) // LANES) * LANES
    place = np.zeros((C_ROPE, C_HEAD_W), np.float32)
    place[np.arange(C_ROPE), C_NOPE + np.arange(C_ROPE)] = 1.0
    per_seq = lambda a: pl.BlockSpec((1,) + a.shape[1:], lambda b: (b,) + (0,) * (a.ndim - 1))
    return pl.pallas_call(
        functools.partial(_decode_body, n_pad=n_pad),
        grid=(nseq,),
        in_specs=[per_seq(q), per_seq(cache_kv), per_seq(cache_kr), per_seq(new_kv), per_seq(new_kr),
                  _const_spec(wk_heads.shape), _const_spec(wv_pairs.shape), _const_spec(place.shape)],
        out_specs=pl.BlockSpec((1, tq, nh * C_V), lambda b: (b, 0, 0)),
        out_shape=jax.ShapeDtypeStruct((nseq, tq, nh * C_V), BF),
        compiler_params=_cparams(1),
        name="mla_decode",
    )(q, cache_kv, cache_kr, new_kv, new_kr, wk_heads, wv_pairs, jnp.asarray(place, BF))


def _rope_tables(pos):
    half = C_ROPE // 2
    inv = (np.float32(ROPE_BASE) ** (-np.arange(half, dtype=np.float32) / np.float32(half))).astype(np.float32)
    ang = (pos.astype(np.float32)[:, None] * inv[None, :]).astype(np.float32).astype(np.float64)
    n = pos.shape[0]
    cos = np.zeros((1, n, C_HEAD_W), np.float32)
    sin = np.zeros((1, n, C_HEAD_W), np.float32)
    cos[0, :, :C_NOPE] = 1.0
    cos[0, :, C_NOPE:C_NOPE + C_ROPE] = np.concatenate([np.cos(ang), np.cos(ang)], axis=1)
    sin[0, :, C_NOPE:C_NOPE + C_ROPE] = np.concatenate([np.sin(ang), np.sin(ang)], axis=1)
    cos32, sin32 = cos[0, :, C_NOPE:C_NOPE + C_ROPE].T, sin[0, :, C_NOPE:C_NOPE + C_ROPE].T
    return jnp.asarray(cos), jnp.asarray(sin), jnp.asarray(cos32), jnp.asarray(sin32)


def _rot_half_cols(w):
    half = w.shape[-1] // 2
    return jnp.concatenate([-w[..., half:], w[..., :half]], axis=-1)


def kernel(x_prompt, x_sample, c_prompt, c_sample, cache_a_k, cache_a_v, cache_b_k, cache_b_v, cache_c_kv, cache_c_kr, w_ada, b_ada, norm_g, final_norm_g, ffn_w_gate, ffn_w_up, ffn_w_down, w_in_ab, w_out_ab, rel_bias_a, t5_bias, sinks_b, w_in_c, c_q_norm_g, c_kv_norm_g, w_qb, w_kvb, w_out_c):
    nb, S, D = x_prompt.shape
    ns, TS, _ = x_sample.shape
    la_c, lb_c = cache_a_k.shape[2], cache_b_k.shape[2]
    past = cache_c_kv.shape[2]
    la_p, lb_p = min(A_PREV_CHUNKS * CHUNK, S), min(B_PREV_CHUNKS * CHUNK, S)
    assert la_p <= ROW_TILE and S % MLA_T == 0 and S % ROW_TILE == 0 and TS % 8 == 0

    n_cond = ns + nb
    n_cond_pad = -(-n_cond // 8) * 8
    c_all = jnp.zeros((n_cond_pad, D), F32).at[:ns].set(c_sample).at[ns:n_cond].set(c_prompt)
    mods_all = _adaln(c_all, w_ada, b_ada)
    groups = {
        "p": dict(x=x_prompt, blk_off=ns, G=1, R=ROW_TILE),
        "s": dict(x=x_sample, blk_off=0, G=ns, R=TS),
    }

    ffn_w = tuple(w.astype(BF) for w in (ffn_w_gate, ffn_w_up, ffn_w_down))
    w_ab = w_in_ab[0].astype(BF)
    w_oa, w_ob = w_out_ab[0, :A_W].astype(BF), w_out_ab[0, A_W:].astype(BF)
    pair_a = tuple(range(A_HEADS // 2))
    pair_b = tuple(p // (B_Q_HEADS // B_KV_HEADS // 2) for p in range(B_Q_HEADS // 2))
    a_back, b_back = A_PREV_CHUNKS * CHUNK, B_PREV_CHUNKS * CHUNK
    a_prev_blocks, b_prev_blocks = -(-a_back // BAND_TQ), -(-b_back // BAND_TQ)
    a_prev_rows, b_prev_rows = a_prev_blocks * BAND_TQ, b_prev_blocks * BAND_TQ
    assert BAND_TQ % BAND_WIN == 0 and BAND_WIN % CHUNK == 0
    win_starts = range(0, BAND_TQ, BAND_WIN)
    windows_a = tuple((qs, BAND_WIN, a_prev_rows + qs - a_back, a_back + BAND_WIN) for qs in win_starts)
    windows_b = tuple((qs, BAND_WIN, b_prev_rows + qs - b_back, b_back + BAND_WIN) for qs in win_starts)
    bias_ap = _bias_tile(rel_bias_a[0], "clip", BAND_WIN, a_back + BAND_WIN, a_back, A_PREV_CHUNKS)
    bias_bp = _bias_tile(t5_bias, "t5", BAND_WIN, b_back + BAND_WIN, b_back, B_PREV_CHUNKS)
    bias_as = _bias_tile(rel_bias_a[0], "clip", TS, la_c + TS, la_c, None)
    bias_bs = _bias_tile(t5_bias, "t5", TS, lb_c + TS, lb_c, None)
    dup = lambda c: jnp.repeat(c.astype(BF), 2, axis=2).reshape(c.shape[0], c.shape[1], 2 * BKV_W)
    cak = cache_a_k[0].astype(BF).reshape(ns, la_c, A_W)
    cav = cache_a_v[0].astype(BF).reshape(ns, la_c, A_W)
    cbk, cbv = dup(cache_b_k[0]), dup(cache_b_v[0])

    mods = mods_all[0].reshape(n_cond_pad, 1, -1)
    state0 = {}
    for name, gr in groups.items():
        x, off, G, R = gr["x"], gr["blk_off"] // gr["G"], gr["G"], gr["R"]
        x = _ffn(x, mods, off, 0, norm_g[0, 0], ffn_w, 0, 0, G, R)
        ta, tb = (la_p, lb_p) if name == "p" else (TS, TS)
        qa, ka, va, qb, kb, vb, ka32, va32, kb32, vb32 = _proj_ab(x, mods, off, norm_g[0, 1], w_ab, G, R, ta, tb)
        if name == "p":
            prev = lambda arr, n: [(arr, BAND_TQ, d - n) for d in range(n + 1)]
            oa = _band_attn(qa, prev(ka, a_prev_blocks), prev(va, a_prev_blocks), bias_ap, None,
                            Tq=BAND_TQ, n_prev_rows=a_prev_rows, pair_map=pair_a, windows=windows_a)
            ob = _band_attn(qb, prev(kb, b_prev_blocks), prev(vb, b_prev_blocks), bias_bp, sinks_b[0],
                            Tq=BAND_TQ, n_prev_rows=b_prev_rows, pair_map=pair_b, windows=windows_b)
        else:
            oa = _band_attn(qa, [(cak, la_c, None), (ka, TS, None)], [(cav, la_c, None), (va, TS, None)],
                            bias_as, None, Tq=TS, n_prev_rows=0, pair_map=pair_a)
            ob = _band_attn(qb, [(cbk, lb_c, None), (kb, TS, None)], [(cbv, lb_c, None), (vb, TS, None)],
                            bias_bs, sinks_b[0], Tq=TS, n_prev_rows=0, pair_map=pair_b)
        x = _ffn(x, mods, off, 6, norm_g[0, 2], ffn_w, 0, 1, G, R, attn=([oa, ob], [w_oa, w_ob]))
        gr["x"] = x
        n_out = x.shape[0]
        state0[name] = (ka32.reshape(1, n_out, ta, A_HEADS, HEAD_DIM), va32.reshape(1, n_out, ta, A_HEADS, HEAD_DIM),
                        kb32.reshape(1, n_out, tb, B_KV_HEADS, HEAD_DIM), vb32.reshape(1, n_out, tb, B_KV_HEADS, HEAD_DIM))

    hw = C_NOPE + C_ROPE
    w_in = w_in_c[0]
    w_kr = w_in[:, C_Q_LORA + C_KV_LORA:]
    win_ext = jnp.concatenate([w_in[:, :C_Q_LORA + C_KV_LORA], jnp.zeros((D, C_NOPE), F32),
                               w_kr, _rot_half_cols(w_kr)], axis=1).astype(BF)
    wq3 = w_qb[0].reshape(C_Q_LORA, C_HEADS, hw)
    wqb_ext = jnp.concatenate([wq3, _rot_half_cols(wq3[..., C_NOPE:])], axis=-1
                              ).reshape(C_Q_LORA, C_HEADS * C_HEAD_W).astype(BF)
    wkv3 = w_kvb[0].reshape(C_KV_LORA, C_HEADS, C_NOPE + C_V)
    wk_ext = jnp.concatenate([wkv3[..., :C_NOPE], jnp.zeros((C_KV_LORA, C_HEADS, C_HEAD_W - C_NOPE), F32)],
                             axis=-1).reshape(C_KV_LORA, C_HEADS * C_HEAD_W).astype(BF)
    wvt = jnp.concatenate([wkv3[..., C_NOPE:], jnp.zeros((C_KV_LORA, C_HEADS, C_VT_ROWS - C_V), F32)], axis=-1
                          ).reshape(C_KV_LORA, C_HEADS * C_VT_ROWS).T.astype(BF)
    w_oc = w_out_c[0].astype(BF)
    wk_heads = jnp.concatenate([jnp.transpose(wkv3[..., :C_NOPE], (1, 2, 0)),
                                jnp.zeros((C_HEADS, C_HEAD_W - C_NOPE, C_KV_LORA), F32)], axis=1).astype(BF)
    wv4 = jnp.transpose(wkv3[..., C_NOPE:], (1, 0, 2)).reshape(C_HEADS // 2, 2, C_KV_LORA, C_V)
    zv = jnp.zeros((C_HEADS // 2, C_KV_LORA, C_V), F32)
    wv_pairs = jnp.concatenate([jnp.concatenate([wv4[:, 0], zv], axis=2),
                                jnp.concatenate([zv, wv4[:, 1]], axis=2)], axis=1).astype(BF)
    tables = {"p": _rope_tables(np.arange(S)), "s": _rope_tables(past + np.arange(TS))}

    mods = mods_all[1].reshape(n_cond_pad, 1, -1)
    state1 = {}
    for name, gr in groups.items():
        x, off, G, R = gr["x"], gr["blk_off"] // gr["G"], gr["G"], gr["R"]
        x = _ffn(x, mods, off, 0, norm_g[1, 0], ffn_w, 1, 0, G, R)
        if name == "p":
            q, kv32, kr32, k, vt = _proj_c(x, mods, off, norm_g[1, 1], win_ext, c_q_norm_g[0], c_kv_norm_g[0],
                                           wqb_ext, tables[name], G, R, q_transposed=True,
                                           expand_w=(wk_ext, wvt))
            o = _mla_attn(q, k, vt.reshape(nb, C_HEADS, C_VT_ROWS, S), T=MLA_T, SK=MLA_SK, QW=MLA_QW)
        else:
            q, kv32, kr32 = _proj_c(x, mods, off, norm_g[1, 1], win_ext, c_q_norm_g[0], c_kv_norm_g[0],
                                    wqb_ext, tables[name], G, R, q_transposed=False)
            o = _mla_decode(q, cache_c_kv[0], cache_c_kr[0], kv32, kr32, wk_heads, wv_pairs)
        x = _ffn(x, mods, off, 6, norm_g[1, 2], ffn_w, 1, 1, G, R, final_g=final_norm_g, attn=([o], [w_oc]))
        gr["x"] = x
        state1[name] = (kv32[None], kr32[None])

    return (groups["p"]["x"], groups["s"]["x"],
            *state0["p"], *state1["p"], *state0["s"], *state1["s"])
```
